```python
import math
import jax, jax.numpy as jnp
from jax import lax
import numpy as np

D_MODEL = 2048
BATCH = 2
SEQ = 4096
DEPTH = 2

GLA_HEADS = 4
GLA_HEAD_K = D_MODEL // (2 * GLA_HEADS)
GLA_HEAD_V = D_MODEL // GLA_HEADS
GLA_GATE_RANK = 16
GLA_GATE_NORMALIZER = 16.0
GLA_CHUNK = 64
GLA_QK_W = GLA_HEADS * GLA_HEAD_K
GLA_V_W = GLA_HEADS * GLA_HEAD_V

DSA_HEADS = 16
DSA_HEAD_DIM = D_MODEL // DSA_HEADS
DSA_LATENT = 256
DSA_TOPK = 256
DSA_Q_BLOCK = 128
IDX_HEADS = 16
IDX_DIM = 64
DSA_Q_W = DSA_HEADS * DSA_LATENT

REL_BUCKETS = 32
REL_MAX_DIST = 128

D_FF = 4 * D_MODEL
N_MOD = 6
EPS = 1e-6

IN_SIZES = (GLA_QK_W, GLA_QK_W, GLA_V_W, GLA_GATE_RANK, GLA_V_W,
            DSA_Q_W, DSA_LATENT, IDX_HEADS * IDX_DIM, IDX_DIM, IDX_HEADS,
            D_MODEL, D_MODEL)
N_IN = sum(IN_SIZES)

kernel_name = "hybrid_gla_dsa_gated_parallel"


def rms_norm(x, g):
    xf = x.astype(jnp.float32)
    y = xf * lax.rsqrt(jnp.mean(xf * xf, axis=-1, keepdims=True) + EPS)
    return (y * g.astype(jnp.float32)).astype(x.dtype)


def split_cols(z):
    parts, off = [], 0
    for s in IN_SIZES:
        parts.append(z[..., off:off + s])
        off += s
    return parts


def rel_bucket(dist):
    max_exact = REL_BUCKETS // 2
    d = jnp.maximum(dist, 0)
    df = jnp.maximum(d, 1).astype(jnp.float32)
    large = max_exact + (jnp.log(df / max_exact) / math.log(REL_MAX_DIST / max_exact)
                         * (REL_BUCKETS - max_exact)).astype(jnp.int32)
    large = jnp.minimum(large, REL_BUCKETS - 1)
    return jnp.where(d < max_exact, d, large)


def gla_branch(q, k, v, a_low, r, w_a2, b_a, g_norm):
    B, S, _ = q.shape
    H, dk, dv, C = GLA_HEADS, GLA_HEAD_K, GLA_HEAD_V, GLA_CHUNK
    n = S // C
    f32 = jnp.float32
    qf = q.reshape(B, S, H, dk).astype(f32) * (dk ** -0.5)
    kf = k.reshape(B, S, H, dk).astype(f32)
    vf = v.reshape(B, S, H, dv).astype(f32)
    g = jax.nn.log_sigmoid((a_low @ w_a2 + b_a).astype(f32)) / GLA_GATE_NORMALIZER
    g = g.reshape(B, S, H, dk)

    def to_chunks(t):
        return t.reshape(B, n, C, H, t.shape[-1]).transpose(1, 0, 3, 2, 4)

    tril = jnp.tril(jnp.ones((C, C), dtype=bool))

    def step(state, inp):
        qc, kc, vc, gc = inp
        b = jnp.cumsum(gc, axis=2)
        o_inter = jnp.einsum('bhcd,bhde->bhce', qc * jnp.exp(b), state)
        diff = b[:, :, :, None, :] - b[:, :, None, :, :]
        decay = jnp.exp(jnp.where(tril[:, :, None], diff, -jnp.inf))
        attn = jnp.sum(qc[:, :, :, None, :] * kc[:, :, None, :, :] * decay, axis=-1)
        o = o_inter + jnp.einsum('bhij,bhje->bhie', attn, vc)
        b_last = b[:, :, -1:, :]
        state = (jnp.exp(b_last[:, :, 0, :])[..., None] * state
                 + jnp.einsum('bhcd,bhce->bhde', kc * jnp.exp(b_last - b), vc))
        return state, o

    state0 = jnp.zeros((B, H, dk, dv), f32)
    _, o = lax.scan(step, state0, (to_chunks(qf), to_chunks(kf), to_chunks(vf), to_chunks(g)))
    o = o.transpose(1, 0, 3, 2, 4).reshape(B, S, H, dv)
    o = o * lax.rsqrt(jnp.mean(o * o, axis=-1, keepdims=True) + EPS) * g_norm.astype(f32)
    o = o.reshape(B, S, GLA_V_W) * jax.nn.silu(r.astype(f32))
    return o.astype(q.dtype)


def dsa_branch(q_lat, kv_lat, iq, ik, iw, kv_g, w_uv, rel_bias):
    B, S, _ = q_lat.shape
    k_sel = min(DSA_TOPK, S // 4)
    nb = S // DSA_Q_BLOCK
    f32 = jnp.float32
    kv = rms_norm(kv_lat, kv_g)
    q4 = q_lat.reshape(B, S, DSA_HEADS, DSA_LATENT)
    iq4 = iq.reshape(B, S, IDX_HEADS, IDX_DIM)
    kpos = jnp.arange(S, dtype=jnp.int32)

    def blocks(t):
        return t.reshape(B, nb, DSA_Q_BLOCK, *t.shape[2:]).swapaxes(0, 1)

    def one_block(inp):
        qb, iqb, iwb, start = inp
        qpos = start + jnp.arange(DSA_Q_BLOCK, dtype=jnp.int32)
        idx_logits = jax.nn.relu(jnp.einsum('bqhd,bkd->bqhk', iqb, ik).astype(f32) * (IDX_DIM ** -0.5))
        score = jnp.einsum('bqh,bqhk->bqk', iwb.astype(f32) * (IDX_HEADS ** -0.5), idx_logits)
        score = jnp.where(kpos[None, None, :] <= qpos[None, :, None], score, -jnp.inf)
        _, idx = lax.top_k(score, k_sel)
        sel = jax.vmap(lambda kvb, ib: kvb[ib])(kv, idx)
        rel = qpos[None, :, None] - idx
        bias = jnp.moveaxis(rel_bias[rel_bucket(rel)], -1, 2)
        logits = (jnp.einsum('bqhl,bqkl->bqhk', qb, sel).astype(f32) * (DSA_LATENT ** -0.5)
                  + bias.astype(f32))
        valid = (rel >= 0)[:, :, None, :]
        probs = jax.nn.softmax(jnp.where(valid, logits, -jnp.inf), axis=-1).astype(sel.dtype)
        o_lat = jnp.einsum('bqhk,bqkl->bqhl', probs, sel)
        o = jnp.einsum('bqhl,hld->bqhd', o_lat, w_uv)
        return o.reshape(B, DSA_Q_BLOCK, DSA_HEADS * DSA_HEAD_DIM)

    starts = jnp.arange(nb, dtype=jnp.int32) * DSA_Q_BLOCK
    out = lax.map(one_block, (blocks(q4), blocks(iq4), blocks(iw), starts))
    return out.swapaxes(0, 1).reshape(B, S, DSA_HEADS * DSA_HEAD_DIM)


def setup_inputs(seed: int = 0) -> dict:
    key = jax.random.key(seed)
    ks = jax.random.split(key, 17)
    f32 = jnp.float32
    D = D_MODEL
    nrm = lambda k, shape, s: jax.random.normal(k, shape, f32) * s
    return {
        "x": nrm(ks[0], (BATCH, SEQ, D), 1.0),
        "c": nrm(ks[1], (BATCH, D), 1.0),
        "w_mod": nrm(ks[2], (DEPTH, D, N_MOD * D), 0.5 * D ** -0.5),
        "b_mod": nrm(ks[3], (DEPTH, N_MOD * D), 0.02),
        "norm1_g": 1.0 + nrm(ks[4], (DEPTH, D), 0.02),
        "w_in": nrm(ks[5], (DEPTH, D, N_IN), D ** -0.5),
        "w_gate_up": nrm(ks[6], (DEPTH, GLA_GATE_RANK, GLA_QK_W), GLA_GATE_RANK ** -0.5),
        "b_gate": nrm(ks[7], (DEPTH, GLA_QK_W), 0.1),
        "gla_norm_g": 1.0 + nrm(ks[8], (DEPTH, GLA_HEAD_V), 0.02),
        "kv_norm_g": 1.0 + nrm(ks[9], (DEPTH, DSA_LATENT), 0.02),
        "w_uv": nrm(ks[10], (DEPTH, DSA_HEADS, DSA_LATENT, DSA_HEAD_DIM), DSA_LATENT ** -0.5),
        "w_out": nrm(ks[11], (DEPTH, D, D), D ** -0.5),
        "norm2_g": 1.0 + nrm(ks[12], (DEPTH, D), 0.02),
        "w_ff1": nrm(ks[13], (DEPTH, D, D_FF), D ** -0.5),
        "w_ff2": nrm(ks[14], (DEPTH, D_FF, D), D_FF ** -0.5),
        "rel_bias": nrm(ks[15], (REL_BUCKETS, DSA_HEADS), 0.5),
        "final_g": 1.0 + nrm(ks[16], (D,), 0.02),
    }


def reference(x, c, w_mod, b_mod, norm1_g, w_in, w_gate_up, b_gate, gla_norm_g, kv_norm_g,
              w_uv, w_out, norm2_g, w_ff1, w_ff2, rel_bias, final_g):
    c_act = jax.nn.silu(c)
    for l in range(DEPTH):
        mod = c_act @ w_mod[l] + b_mod[l]
        sh1, sc1, gt1, sh2, sc2, gt2 = [m[:, None, :] for m in jnp.split(mod, N_MOD, axis=-1)]

        h = rms_norm(x, norm1_g[l]) * (1.0 + sc1) + sh1
        (q_g, k_g, v_g, a_low, r_g, q_lat, kv_lat, iq, ik, iw,
         g_gla, g_dsa) = split_cols(h @ w_in[l])
        y_gla = gla_branch(q_g, k_g, v_g, a_low, r_g, w_gate_up[l], b_gate[l], gla_norm_g[l])
        y_dsa = dsa_branch(q_lat, kv_lat, iq, ik, iw, kv_norm_g[l], w_uv[l], rel_bias)
        merged = jax.nn.sigmoid(g_gla) * y_gla + jax.nn.sigmoid(g_dsa) * y_dsa
        x = x + gt1 * (merged @ w_out[l])

        h = rms_norm(x, norm2_g[l]) * (1.0 + sc2) + sh2
        u = jnp.square(jax.nn.relu(h @ w_ff1[l]))
        x = x + gt2 * (u @ w_ff2[l])
    return rms_norm(x, final_g)
```

```python
import functools
import math

import numpy as np
import jax
import jax.numpy as jnp
from jax import lax
from jax.experimental import pallas as pl
from jax.experimental.pallas import tpu as pltpu

F32 = jnp.float32
BF16 = jnp.bfloat16

D_MODEL = 2048
GLA_HEADS = 4
GLA_HEAD_K = 256
GLA_HEAD_V = 512
GLA_GATE_RANK = 16
GLA_GATE_NORMALIZER = 16.0
DSA_HEADS = 16
DSA_HEAD_DIM = 128
DSA_LATENT = 256
DSA_TOPK = 256
IDX_HEADS = 16
IDX_DIM = 64
REL_BUCKETS = 32
REL_MAX_DIST = 128
D_FF = 4 * D_MODEL
N_MOD = 6
EPS = 1e-6

GLA_QK_W = GLA_HEADS * GLA_HEAD_K
GLA_V_W = GLA_HEADS * GLA_HEAD_V
DSA_Q_W = DSA_HEADS * DSA_LATENT
IN_SIZES = (GLA_QK_W, GLA_QK_W, GLA_V_W, GLA_GATE_RANK, GLA_V_W,
            DSA_Q_W, DSA_LATENT, IDX_HEADS * IDX_DIM, IDX_DIM, IDX_HEADS,
            D_MODEL, D_MODEL)

BIG_QLAT = 0
BIG_QG = BIG_QLAT + DSA_Q_W
BIG_KG = BIG_QG + GLA_QK_W
BIG_VG = BIG_KG + GLA_QK_W
BIG_RG = BIG_VG + GLA_V_W
BIG_GGLA = BIG_RG + GLA_V_W
BIG_GDSA = BIG_GGLA + D_MODEL
BIG_W = BIG_GDSA + D_MODEL
SM_IQ = 0
SM_KV = SM_IQ + IDX_HEADS * IDX_DIM
SM_MISC = SM_KV + DSA_LATENT
MISC_IK = 0
MISC_ALOW = MISC_IK + IDX_DIM
MISC_IW = MISC_ALOW + GLA_GATE_RANK
LANE = 128
SM_W = SM_MISC + LANE

MOD_SH1, MOD_SC1, MOD_GT1, MOD_SH2, MOD_SC2, MOD_GT2 = range(6)

GLA_CHUNK = 64
GLA_SUB = 16
GLA_BLOCK = 256
DSA_TQ = 256
DSA_HB = 4
INT_MIN = -2 ** 31
NEG_INF_KEY = 0x807FFFFF - 2 ** 32
NEG_BIG = -1e30

VMEM_LIMIT = 56 * 1024 * 1024


def _cparams(sem):
    return pltpu.CompilerParams(dimension_semantics=sem, vmem_limit_bytes=VMEM_LIMIT)


def _rms_mod(x, g, sc, sh):
    ms = jnp.mean(x * x, axis=-1, keepdims=True)
    return (x * lax.rsqrt(ms + EPS) * g) * (1.0 + sc) + sh


def _mod_kernel(c_ref, w_ref, b_ref, o_ref):
    c = c_ref[...]
    c_act = (c * jax.nn.sigmoid(c)).astype(BF16)
    o_ref[0] = jnp.dot(c_act, w_ref[0].astype(BF16), preferred_element_type=F32) + b_ref[0]


def _modulation(c, w_mod, b_mod):
    depth, d, n = w_mod.shape
    tn = 1024
    c8 = jnp.zeros((8, d), F32).at[:c.shape[0]].set(c)
    return pl.pallas_call(
        _mod_kernel,
        grid=(depth, n // tn),
        in_specs=[pl.BlockSpec((8, d), lambda l, j: (0, 0)),
                  pl.BlockSpec((1, d, tn), lambda l, j: (l, 0, j)),
                  pl.BlockSpec((1, 1, tn), lambda l, j: (l, 0, j))],
        out_specs=pl.BlockSpec((1, 8, tn), lambda l, j: (l, 0, j)),
        out_shape=jax.ShapeDtypeStruct((depth, 8, n), F32),
        compiler_params=_cparams(("arbitrary", "arbitrary")),
        name="adaln_mod",
    )(c8, w_mod, b_mod.reshape(depth, 1, n))


def _norm_mm_kernel(x_ref, mod_ref, g_ref, w_ref, o_ref, h_ref):
    @pl.when(pl.program_id(1) == 0)
    def _():
        h = _rms_mod(x_ref[...], g_ref[...], mod_ref[0, MOD_SC1:MOD_SC1 + 1, :],
                     mod_ref[0, MOD_SH1:MOD_SH1 + 1, :])
        h_ref[...] = h.astype(BF16)

    o_ref[...] = jnp.dot(h_ref[...], w_ref[...], preferred_element_type=F32).astype(o_ref.dtype)


def _norm_matmul(x, mod, g, w, out_dtype, seq, tm, tn):
    m, d = x.shape
    n = w.shape[1]
    tpb = seq // tm
    return pl.pallas_call(
        _norm_mm_kernel,
        grid=(m // tm, n // tn),
        in_specs=[pl.BlockSpec((tm, d), lambda i, j: (i, 0)),
                  pl.BlockSpec((1, N_MOD, d), lambda i, j: (i // tpb, 0, 0)),
                  pl.BlockSpec((1, d), lambda i, j: (0, 0)),
                  pl.BlockSpec((d, tn), lambda i, j: (0, j))],
        out_specs=pl.BlockSpec((tm, tn), lambda i, j: (i, j)),
        out_shape=jax.ShapeDtypeStruct((m, n), out_dtype),
        scratch_shapes=[pltpu.VMEM((tm, d), BF16)],
        compiler_params=_cparams(("arbitrary", "arbitrary")),
        name="norm_inproj",
    )(x, mod, g, w)


def _gla_kernel(q_ref, k_ref, v_ref, r_ref, misc_ref, wa_ref, ba_ref, gn_ref, o_ref,
                s_ref, a_ref, *, block, chunk):
    dk, dv, sub = GLA_HEAD_K, GLA_HEAD_V, GLA_SUB
    nsub = chunk // sub

    @pl.when(pl.program_id(2) == 0)
    def _():
        s_ref[...] = jnp.zeros_like(s_ref)

    a_ref[...] = jnp.zeros_like(a_ref)

    a_low = misc_ref[:, MISC_ALOW:MISC_ALOW + GLA_GATE_RANK].astype(BF16)
    xg = jnp.dot(a_low, wa_ref[...].astype(BF16), preferred_element_type=F32) + ba_ref[...]
    g_all = (jnp.minimum(xg, 0.0) - jnp.log1p(jnp.exp(-jnp.abs(xg)))) * (1.0 / GLA_GATE_NORMALIZER)

    r_i = lax.broadcasted_iota(jnp.int32, (chunk, chunk), 0)
    c_i = lax.broadcasted_iota(jnp.int32, (chunk, chunk), 1)
    tril = (r_i >= c_i).astype(BF16)
    row_in_chunk = lax.broadcasted_iota(jnp.int32, (chunk, dk), 0)
    lane3 = lax.broadcasted_iota(jnp.int32, (nsub, sub, sub), 2)
    row3 = lax.broadcasted_iota(jnp.int32, (nsub, sub, sub), 1)
    gn = gn_ref[...]

    for c in range(block // chunk):
        lo = c * chunk
        qf = q_ref[lo:lo + chunk, :].astype(F32) * (dk ** -0.5)
        kf = k_ref[lo:lo + chunk, :].astype(F32)
        vb = v_ref[lo:lo + chunk, :]
        g = g_all[lo:lo + chunk, :]

        g_hi = g.astype(BF16)
        r1 = g - g_hi.astype(F32)
        g_mid = r1.astype(BF16)
        g_lo = (r1 - g_mid.astype(F32)).astype(BF16)
        b = (jnp.dot(tril, g_hi, preferred_element_type=F32)
             + jnp.dot(tril, g_mid, preferred_element_type=F32)
             + jnp.dot(tril, g_lo, preferred_element_type=F32))

        state = s_ref[...]
        o = jnp.dot((qf * jnp.exp(b)).astype(BF16), state.astype(BF16), preferred_element_type=F32)

        m = chunk // 2
        while m >= sub:
            nseg = chunk // (2 * m)
            pieces = []
            for s in range(nseg):
                ref_row = b[s * 2 * m + m - 1:s * 2 * m + m, :]
                pieces.append(jnp.broadcast_to(ref_row, (2 * m, dk)))
            ref_full = pieces[0] if nseg == 1 else jnp.concatenate(pieces, axis=0)
            fac = jnp.exp(-jnp.abs(b - ref_full))
            is_q = (row_in_chunk & m) != 0
            xs = (jnp.where(is_q, qf, kf) * fac).astype(BF16)
            for s in range(nseg):
                base = s * 2 * m
                blk = lax.dot_general(xs[base + m:base + 2 * m, :], xs[base:base + m, :],
                                      (((1,), (1,)), ((), ())), preferred_element_type=F32)
                a_ref[base + m:base + 2 * m, base:base + m] = blk
            m //= 2

        q4 = qf.reshape(nsub, sub, dk)
        k4 = kf.reshape(nsub, sub, dk)
        b4 = b.reshape(nsub, sub, dk)
        ad = jnp.zeros((nsub, sub, sub), F32)
        for j in range(sub):
            e = jnp.exp(jnp.minimum(b4 - b4[:, j:j + 1, :], 0.0))
            col = jnp.sum(q4 * e * k4[:, j:j + 1, :], axis=-1, keepdims=True)
            ad = jnp.where(lane3 == j, col, ad)
        ad = jnp.where(row3 >= lane3, ad, 0.0)
        for i in range(nsub):
            a_ref[i * sub:(i + 1) * sub, i * sub:(i + 1) * sub] = ad[i]

        o = o + jnp.dot(a_ref[...].astype(BF16), vb, preferred_element_type=F32)

        b_last = b[chunk - 1:chunk, :]
        kd = (kf * jnp.exp(b_last - b)).astype(BF16)
        upd = lax.dot_general(kd, vb, (((0,), (0,)), ((), ())), preferred_element_type=F32)
        decay_col = jnp.transpose(jnp.broadcast_to(jnp.exp(b_last), (LANE, dk)))[:, 0:1]
        s_ref[...] = decay_col * state + upd

        o = o * lax.rsqrt(jnp.mean(o * o, axis=-1, keepdims=True) + EPS) * gn
        r = r_ref[lo:lo + chunk, :].astype(F32)
        o_ref[lo:lo + chunk, :] = (o * (r * jax.nn.sigmoid(r))).astype(o_ref.dtype)


def _gla(z_big, z_small, w_gate_up, b_gate, gla_norm_g, batch, seq):
    blk = min(GLA_BLOCK, seq)
    nb = seq // blk
    dk, dv = GLA_HEAD_K, GLA_HEAD_V
    row = lambda b, h, t: b * nb + t
    kern = functools.partial(_gla_kernel, block=blk, chunk=GLA_CHUNK)
    return pl.pallas_call(
        kern,
        grid=(batch, GLA_HEADS, nb),
        in_specs=[pl.BlockSpec((blk, dk), lambda b, h, t: (row(b, h, t), BIG_QG // dk + h)),
                  pl.BlockSpec((blk, dk), lambda b, h, t: (row(b, h, t), BIG_KG // dk + h)),
                  pl.BlockSpec((blk, dv), lambda b, h, t: (row(b, h, t), BIG_VG // dv + h)),
                  pl.BlockSpec((blk, dv), lambda b, h, t: (row(b, h, t), BIG_RG // dv + h)),
                  pl.BlockSpec((blk, LANE), lambda b, h, t: (row(b, h, t), SM_MISC // LANE)),
                  pl.BlockSpec((GLA_GATE_RANK, dk), lambda b, h, t: (0, h)),
                  pl.BlockSpec((1, dk), lambda b, h, t: (0, h)),
                  pl.BlockSpec((1, dv), lambda b, h, t: (0, 0))],
        out_specs=pl.BlockSpec((blk, dv), lambda b, h, t: (row(b, h, t), h)),
        out_shape=jax.ShapeDtypeStruct((batch * seq, GLA_V_W), BF16),
        scratch_shapes=[pltpu.VMEM((dk, dv), F32), pltpu.VMEM((GLA_CHUNK, GLA_CHUNK), F32)],
        compiler_params=_cparams(("arbitrary", "arbitrary", "arbitrary")),
        name="gla",
    )(z_big, z_big, z_big, z_big, z_small, w_gate_up, b_gate.reshape(1, -1), gla_norm_g.reshape(1, -1))


def _rel_bucket_np(d):
    max_exact = REL_BUCKETS // 2
    d = np.maximum(d, 0)
    df = np.maximum(d, 1).astype(np.float32)
    large = max_exact + (np.log(df / np.float32(max_exact)) / np.float32(math.log(REL_MAX_DIST / max_exact))
                         * np.float32(REL_BUCKETS - max_exact)).astype(np.int32)
    large = np.minimum(large, REL_BUCKETS - 1)
    return np.where(d < max_exact, d, large).astype(np.int32)


def _bias_kernel(rb_ref, bucket_ref, o_ref):
    h = pl.program_id(0)
    far = rb_ref[REL_BUCKETS - 1, h]
    for u in range(2):
        bk = bucket_ref[u]
        acc = jnp.zeros(bk.shape, F32)
        for bb in range(REL_BUCKETS):
            acc = jnp.where(bk == bb, rb_ref[bb, h], acc)
        o_ref[0, u] = acc - far


def _bias_tiles(rel_bias):
    i = np.arange(LANE)[:, None]
    j = np.arange(LANE)[None, :]
    bucket = np.stack([_rel_bucket_np(i - j), _rel_bucket_np(LANE + i - j)]).astype(np.int32)
    assert int(_rel_bucket_np(np.array([LANE]))[0]) == REL_BUCKETS - 1
    return pl.pallas_call(
        _bias_kernel,
        grid=(DSA_HEADS,),
        in_specs=[pl.BlockSpec(memory_space=pltpu.SMEM),
                  pl.BlockSpec((2, LANE, LANE), lambda h: (0, 0, 0))],
        out_specs=pl.BlockSpec((1, 2, LANE, LANE), lambda h: (h, 0, 0, 0)),
        out_shape=jax.ShapeDtypeStruct((DSA_HEADS, 2, LANE, LANE), F32),
        compiler_params=_cparams(("arbitrary",)),
        name="t5_bias_tiles",
    )(rel_bias, jnp.asarray(bucket))


def _key_to_float(key):
    return pltpu.bitcast(key ^ ((key >> 31) & 0x7FFFFFFF), F32)


def _dsa_kernel(q_ref, iq_ref, kv_ref, misc_all_ref, misc_q_ref, kvg_ref, wuv_ref, bias_ref, o_ref,
                kvn_ref, ikb_ref, iqh_ref, sc_ref, thr_ref, acc_ref, *, seq, k_sel):
    tq, hb, lat = DSA_TQ, DSA_HB, DSA_LATENT
    qi = pl.program_id(1)
    hg = pl.program_id(2)
    row_i = lax.broadcasted_iota(jnp.int32, (tq, tq), 0)
    col_i = lax.broadcasted_iota(jnp.int32, (tq, tq), 1)
    causal = col_i <= row_i

    @pl.when((qi == 0) & (hg == 0))
    def _():
        kv = kv_ref[...]
        ms = jnp.mean(kv * kv, axis=-1, keepdims=True)
        kvn_ref[...] = (kv * lax.rsqrt(ms + EPS) * kvg_ref[...]).astype(BF16)
        ikb_ref[...] = misc_all_ref[:, MISC_IK:MISC_IK + IDX_DIM].astype(BF16)

    @pl.when(hg == 0)
    def _():
        for h in range(IDX_HEADS):
            iqh_ref[h] = iq_ref[:, h * IDX_DIM:(h + 1) * IDX_DIM].astype(BF16)
        iw = misc_q_ref[:, MISC_IW:MISC_IW + IDX_HEADS] * (IDX_HEADS ** -0.5)

        def score_tile(kt):
            off = pl.multiple_of(kt * tq, tq)
            ik_t = ikb_ref[pl.ds(off, tq), :]
            sc = jnp.zeros((tq, tq), F32)
            for h in range(IDX_HEADS):
                y = lax.dot_general(iqh_ref[h], ik_t, (((1,), (1,)), ((), ())), preferred_element_type=F32)
                sc = sc + jnp.maximum(y * (IDX_DIM ** -0.5), 0.0) * iw[:, h:h + 1]
            return sc

        def far_body(kt, carry):
            sc_ref[kt] = score_tile(kt)
            return carry

        lax.fori_loop(0, qi, far_body, 0)
        sc_ref[qi] = jnp.where(causal, score_tile(qi), -jnp.inf)

        def bit_body(it, key):
            cand_key = key + (jnp.int32(1) << (31 - it))
            cand = _key_to_float(cand_key)

            def cnt_body(kt, cnt):
                hit = jnp.where(sc_ref[kt] >= cand, 1, 0)
                return cnt + hit[:, :LANE] + hit[:, LANE:]

            cnt = lax.fori_loop(0, qi + 1, cnt_body, jnp.zeros((tq, LANE), jnp.int32))
            tot = jnp.sum(cnt, axis=-1, keepdims=True)
            return jnp.where((tot >= k_sel) | (cand_key <= NEG_INF_KEY), cand_key, key)

        key = lax.fori_loop(0, 32, bit_body, jnp.full((tq, 1), INT_MIN, jnp.int32))
        thr_ref[...] = jnp.broadcast_to(_key_to_float(key), thr_ref.shape)

    thr = thr_ref[:, 0:1]
    zero_blk = jnp.zeros((LANE, LANE), F32)

    for hh in range(hb):
        q = q_ref[:, hh * lat:(hh + 1) * lat] * (lat ** -0.5)
        t0 = bias_ref[hh, 0]
        t1 = bias_ref[hh, 1]

        def tile(kt, carry, bias=None, mask_causal=False):
            m_run, l_run = carry
            off = pl.multiple_of(kt * tq, tq)
            kv_t = kvn_ref[pl.ds(off, tq), :]
            s = lax.dot_general(q, kv_t, (((1,), (1,)), ((), ())), preferred_element_type=F32)
            if bias is not None:
                s = s + bias
            sel = sc_ref[kt] >= thr
            if mask_causal:
                sel = sel & causal
            s = jnp.where(sel, s, NEG_BIG)
            m_new = jnp.maximum(m_run, jnp.max(s, axis=-1, keepdims=True))
            alpha = jnp.exp(m_run - m_new)
            p = jnp.exp(s - m_new)
            l_new = alpha * l_run + jnp.sum(p, axis=-1, keepdims=True)
            acc_ref[...] = alpha * acc_ref[...] + jnp.dot(p.astype(BF16), kv_t, preferred_element_type=F32)
            return m_new, l_new

        acc_ref[...] = jnp.zeros_like(acc_ref)
        carry = (jnp.full((tq, 1), NEG_BIG, F32), jnp.zeros((tq, 1), F32))
        carry = lax.fori_loop(0, jnp.maximum(qi - 1, 0), tile, carry)

        bias_prev = jnp.concatenate([jnp.concatenate([zero_blk, t1], axis=1),
                                     jnp.concatenate([zero_blk, zero_blk], axis=1)], axis=0)
        bias_diag = jnp.concatenate([jnp.concatenate([t0, zero_blk], axis=1),
                                     jnp.concatenate([t1, t0], axis=1)], axis=0)
        carry = lax.cond(qi > 0,
                         lambda cr: tile(qi - 1, cr, bias=bias_prev),
                         lambda cr: cr, carry)
        m_run, l_run = tile(qi, carry, bias=bias_diag, mask_causal=True)

        o_lat = (acc_ref[...] / l_run).astype(BF16)
        o_ref[:, hh * DSA_HEAD_DIM:(hh + 1) * DSA_HEAD_DIM] = jnp.dot(
            o_lat, wuv_ref[hh], preferred_element_type=F32).astype(o_ref.dtype)


def _dsa(z_big, z_small, kv_norm_g, w_uv_bf16, bias_tiles, batch, seq):
    tq, hb, lat = DSA_TQ, DSA_HB, DSA_LATENT
    nq = seq // tq
    k_sel = min(DSA_TOPK, seq // 4)
    kern = functools.partial(_dsa_kernel, seq=seq, k_sel=k_sel)
    return pl.pallas_call(
        kern,
        grid=(batch, nq, DSA_HEADS // hb),
        in_specs=[pl.BlockSpec((tq, hb * lat), lambda b, i, g: (b * nq + i, BIG_QLAT // (hb * lat) + g)),
                  pl.BlockSpec((tq, IDX_HEADS * IDX_DIM), lambda b, i, g: (b * nq + i, 0)),
                  pl.BlockSpec((seq, lat), lambda b, i, g: (b, SM_KV // lat)),
                  pl.BlockSpec((seq, LANE), lambda b, i, g: (b, SM_MISC // LANE)),
                  pl.BlockSpec((tq, LANE), lambda b, i, g: (b * nq + i, SM_MISC // LANE)),
                  pl.BlockSpec((1, lat), lambda b, i, g: (0, 0)),
                  pl.BlockSpec((hb, lat, DSA_HEAD_DIM), lambda b, i, g: (g, 0, 0)),
                  pl.BlockSpec((hb, 2, LANE, LANE), lambda b, i, g: (g, 0, 0, 0))],
        out_specs=pl.BlockSpec((tq, hb * DSA_HEAD_DIM), lambda b, i, g: (b * nq + i, g)),
        out_shape=jax.ShapeDtypeStruct((batch * seq, DSA_HEADS * DSA_HEAD_DIM), BF16),
        scratch_shapes=[pltpu.VMEM((seq, lat), BF16),
                        pltpu.VMEM((seq, IDX_DIM), BF16),
                        pltpu.VMEM((IDX_HEADS, tq, IDX_DIM), BF16),
                        pltpu.VMEM((nq, tq, tq), F32),
                        pltpu.VMEM((tq, LANE), F32),
                        pltpu.VMEM((tq, lat), F32)],
        compiler_params=_cparams(("arbitrary", "arbitrary", "arbitrary")),
        name="dsa",
    )(z_big, z_small, z_small, z_small, z_small, kv_norm_g.reshape(1, -1), w_uv_bf16, bias_tiles)


def _merge_mm_kernel(gg_ref, gd_ref, yg_ref, yd_ref, w_ref, x_ref, mod_ref, o_ref, m_ref):
    @pl.when(pl.program_id(1) == 0)
    def _():
        merged = (jax.nn.sigmoid(gg_ref[...].astype(F32)) * yg_ref[...].astype(F32)
                  + jax.nn.sigmoid(gd_ref[...].astype(F32)) * yd_ref[...].astype(F32))
        m_ref[...] = merged.astype(BF16)

    y = jnp.dot(m_ref[...], w_ref[...], preferred_element_type=F32)
    o_ref[...] = x_ref[...] + mod_ref[0, MOD_GT1:MOD_GT1 + 1, :] * y


def _merge_outproj(z_big, y_gla, y_dsa, w_out_bf16, x, mod, seq, tm, tn):
    m, d = x.shape
    tpb = seq // tm
    return pl.pallas_call(
        _merge_mm_kernel,
        grid=(m // tm, d // tn),
        in_specs=[pl.BlockSpec((tm, d), lambda i, j: (i, BIG_GGLA // d)),
                  pl.BlockSpec((tm, d), lambda i, j: (i, BIG_GDSA // d)),
                  pl.BlockSpec((tm, d), lambda i, j: (i, 0)),
                  pl.BlockSpec((tm, d), lambda i, j: (i, 0)),
                  pl.BlockSpec((d, tn), lambda i, j: (0, j)),
                  pl.BlockSpec((tm, tn), lambda i, j: (i, j)),
                  pl.BlockSpec((1, N_MOD, tn), lambda i, j: (i // tpb, 0, j))],
        out_specs=pl.BlockSpec((tm, tn), lambda i, j: (i, j)),
        out_shape=jax.ShapeDtypeStruct((m, d), F32),
        scratch_shapes=[pltpu.VMEM((tm, d), BF16)],
        compiler_params=_cparams(("arbitrary", "arbitrary")),
        name="merge_outproj",
    )(z_big, z_big, y_gla, y_dsa, w_out_bf16, x, mod)


def _ffn_kernel(x_ref, mod_ref, g_ref, w1_ref, w2_ref, fg_ref, o_ref, h_ref, acc_ref, *, final):
    j = pl.program_id(1)

    @pl.when(j == 0)
    def _():
        h = _rms_mod(x_ref[...], g_ref[...], mod_ref[0, MOD_SC2:MOD_SC2 + 1, :],
                     mod_ref[0, MOD_SH2:MOD_SH2 + 1, :])
        h_ref[...] = h.astype(BF16)
        acc_ref[...] = jnp.zeros_like(acc_ref)

    u = jnp.dot(h_ref[...], w1_ref[...], preferred_element_type=F32)
    u = jnp.square(jnp.maximum(u, 0.0)).astype(BF16)
    acc_ref[...] += jnp.dot(u, w2_ref[...], preferred_element_type=F32)

    @pl.when(j == pl.num_programs(1) - 1)
    def _():
        y = x_ref[...] + mod_ref[0, MOD_GT2:MOD_GT2 + 1, :] * acc_ref[...]
        if final:
            y = y * lax.rsqrt(jnp.mean(y * y, axis=-1, keepdims=True) + EPS) * fg_ref[...]
        o_ref[...] = y


def _ffn(x, mod, g, w1, w2, final_g, seq, tm, tf, final):
    m, d = x.shape
    f = w1.shape[1]
    tpb = seq // tm
    return pl.pallas_call(
        functools.partial(_ffn_kernel, final=final),
        grid=(m // tm, f // tf),
        in_specs=[pl.BlockSpec((tm, d), lambda i, j: (i, 0)),
                  pl.BlockSpec((1, N_MOD, d), lambda i, j: (i // tpb, 0, 0)),
                  pl.BlockSpec((1, d), lambda i, j: (0, 0)),
                  pl.BlockSpec((d, tf), lambda i, j: (0, j)),
                  pl.BlockSpec((tf, d), lambda i, j: (j, 0)),
                  pl.BlockSpec((1, d), lambda i, j: (0, 0))],
        out_specs=pl.BlockSpec((tm, d), lambda i, j: (i, 0)),
        out_shape=jax.ShapeDtypeStruct((m, d), F32),
        scratch_shapes=[pltpu.VMEM((tm, d), BF16), pltpu.VMEM((tm, d), F32)],
        compiler_params=_cparams(("arbitrary", "arbitrary")),
        name="ffn",
    )(x, mod, g, w1, w2, final_g)


def _split_w_in(w):
    parts, off = [], 0
    for s in IN_SIZES:
        parts.append(w[:, off:off + s])
        off += s
    q_g, k_g, v_g, a_low, r_g, q_lat, kv_lat, iq, ik, iw, g_gla, g_dsa = parts
    big = jnp.concatenate([q_lat, q_g, k_g, v_g, r_g, g_gla, g_dsa], axis=1).astype(BF16)
    pad = jnp.zeros((w.shape[0], SM_W - SM_MISC - IDX_DIM - GLA_GATE_RANK - IDX_HEADS), w.dtype)
    small = jnp.concatenate([iq, kv_lat, ik, a_low, iw, pad], axis=1).astype(BF16)
    return big, small


def kernel(x, c, w_mod, b_mod, norm1_g, w_in, w_gate_up, b_gate, gla_norm_g, kv_norm_g,
           w_uv, w_out, norm2_g, w_ff1, w_ff2, rel_bias, final_g):
    batch, seq, d = x.shape
    depth = w_mod.shape[0]
    m = batch * seq
    tm = min(1024, seq)

    mods = _modulation(c, w_mod, b_mod).reshape(depth, 8, N_MOD, d)
    bias_tiles = _bias_tiles(rel_bias)
    xf = x.reshape(m, d)
    fg = final_g.reshape(1, d)

    for l in range(depth):
        w_big, w_small = _split_w_in(w_in[l])
        mod = mods[l]
        g1 = norm1_g[l].reshape(1, d)
        z_big = _norm_matmul(xf, mod, g1, w_big, BF16, seq, tm, 512)
        z_small = _norm_matmul(xf, mod, g1, w_small, F32, seq, tm, LANE)
        y_gla = _gla(z_big, z_small, w_gate_up[l], b_gate[l], gla_norm_g[l], batch, seq)
        y_dsa = _dsa(z_big, z_small, kv_norm_g[l], w_uv[l].astype(BF16), bias_tiles, batch, seq)
        xf = _merge_outproj(z_big, y_gla, y_dsa, w_out[l].astype(BF16), xf, mod, seq, tm, 512)
        xf = _ffn(xf, mod, norm2_g[l].reshape(1, d), w_ff1[l].astype(BF16), w_ff2[l].astype(BF16),
                  fg, seq, min(512, seq), 512, final=(l == depth - 1))
    return xf.reshape(batch, seq, d)
```

```python
import functools
import math

import numpy as np
import jax
import jax.numpy as jnp
from jax import lax
from jax.experimental import pallas as pl
from jax.experimental.pallas import tpu as pltpu

F32 = jnp.float32
BF16 = jnp.bfloat16

D_MODEL = 2048
GLA_HEADS = 4
GLA_HEAD_K = 256
GLA_HEAD_V = 512
GLA_GATE_RANK = 16
GLA_GATE_NORMALIZER = 16.0
DSA_HEADS = 16
DSA_HEAD_DIM = 128
DSA_LATENT = 256
DSA_TOPK = 256
IDX_HEADS = 16
IDX_DIM = 64
REL_BUCKETS = 32
REL_MAX_DIST = 128
D_FF = 4 * D_MODEL
N_MOD = 6
EPS = 1e-6

GLA_QK_W = GLA_HEADS * GLA_HEAD_K
GLA_V_W = GLA_HEADS * GLA_HEAD_V
DSA_Q_W = DSA_HEADS * DSA_LATENT
IN_SIZES = (GLA_QK_W, GLA_QK_W, GLA_V_W, GLA_GATE_RANK, GLA_V_W,
            DSA_Q_W, DSA_LATENT, IDX_HEADS * IDX_DIM, IDX_DIM, IDX_HEADS,
            D_MODEL, D_MODEL)

BIG_QLAT = 0
BIG_QG = BIG_QLAT + DSA_Q_W
BIG_KG = BIG_QG + GLA_QK_W
BIG_VG = BIG_KG + GLA_QK_W
BIG_RG = BIG_VG + GLA_V_W
BIG_GGLA = BIG_RG + GLA_V_W
BIG_GDSA = BIG_GGLA + D_MODEL
BIG_W = BIG_GDSA + D_MODEL
SM_IQ = 0
SM_KV = SM_IQ + IDX_HEADS * IDX_DIM
SM_MISC = SM_KV + DSA_LATENT
MISC_IK = 0
MISC_ALOW = MISC_IK + IDX_DIM
MISC_IW = MISC_ALOW + GLA_GATE_RANK
LANE = 128
SM_W = SM_MISC + LANE

MOD_SH1, MOD_SC1, MOD_GT1, MOD_SH2, MOD_SC2, MOD_GT2 = range(6)

GLA_CHUNK = 64
GLA_SUB = 16
GLA_BLOCK = 256
DSA_TQ = 256
DSA_HB = 4
INT_MIN = -2 ** 31
NEG_INF_KEY = 0x807FFFFF - 2 ** 32
NEG_BIG = -1e30

VMEM_LIMIT = 56 * 1024 * 1024


def _cparams(sem):
    return pltpu.CompilerParams(dimension_semantics=sem, vmem_limit_bytes=VMEM_LIMIT)


def _rms_mod(x, g, sc, sh):
    ms = jnp.mean(x * x, axis=-1, keepdims=True)
    return (x * lax.rsqrt(ms + EPS) * g) * (1.0 + sc) + sh


def _mod_kernel(c_ref, w_ref, b_ref, o_ref):
    c = c_ref[...]
    c_act = (c * jax.nn.sigmoid(c)).astype(BF16)
    o_ref[0] = jnp.dot(c_act, w_ref[0].astype(BF16), preferred_element_type=F32) + b_ref[0]


def _modulation(c, w_mod, b_mod):
    depth, d, n = w_mod.shape
    tn = 1024
    c8 = jnp.zeros((8, d), F32).at[:c.shape[0]].set(c)
    return pl.pallas_call(
        _mod_kernel,
        grid=(depth, n // tn),
        in_specs=[pl.BlockSpec((8, d), lambda l, j: (0, 0)),
                  pl.BlockSpec((1, d, tn), lambda l, j: (l, 0, j)),
                  pl.BlockSpec((1, 1, tn), lambda l, j: (l, 0, j))],
        out_specs=pl.BlockSpec((1, 8, tn), lambda l, j: (l, 0, j)),
        out_shape=jax.ShapeDtypeStruct((depth, 8, n), F32),
        compiler_params=_cparams(("arbitrary", "arbitrary")),
        name="adaln_mod",
    )(c8, w_mod, b_mod.reshape(depth, 1, n))


def _norm_mm_kernel(x_ref, mod_ref, g_ref, w_ref, o_ref, h_ref):
    @pl.when(pl.program_id(1) == 0)
    def _():
        h = _rms_mod(x_ref[...], g_ref[...], mod_ref[0, MOD_SC1:MOD_SC1 + 1, :],
                     mod_ref[0, MOD_SH1:MOD_SH1 + 1, :])
        h_ref[...] = h.astype(BF16)

    o_ref[...] = jnp.dot(h_ref[...], w_ref[...], preferred_element_type=F32).astype(o_ref.dtype)


def _norm_matmul(x, mod, g, w, out_dtype, seq, tm, tn):
    m, d = x.shape
    n = w.shape[1]
    tpb = seq // tm
    return pl.pallas_call(
        _norm_mm_kernel,
        grid=(m // tm, n // tn),
        in_specs=[pl.BlockSpec((tm, d), lambda i, j: (i, 0)),
                  pl.BlockSpec((1, N_MOD, d), lambda i, j: (i // tpb, 0, 0)),
                  pl.BlockSpec((1, d), lambda i, j: (0, 0)),
                  pl.BlockSpec((d, tn), lambda i, j: (0, j))],
        out_specs=pl.BlockSpec((tm, tn), lambda i, j: (i, j)),
        out_shape=jax.ShapeDtypeStruct((m, n), out_dtype),
        scratch_shapes=[pltpu.VMEM((tm, d), BF16)],
        compiler_params=_cparams(("arbitrary", "arbitrary")),
        name="norm_inproj",
    )(x, mod, g, w)


def _gla_kernel(q_ref, k_ref, v_ref, r_ref, misc_ref, wa_ref, ba_ref, gn_ref, o_ref,
                s_ref, a_ref, *, block, chunk):
    dk, dv, sub = GLA_HEAD_K, GLA_HEAD_V, GLA_SUB
    nsub = chunk // sub

    @pl.when(pl.program_id(2) == 0)
    def _():
        s_ref[...] = jnp.zeros_like(s_ref)

    a_ref[...] = jnp.zeros_like(a_ref)

    a_low = misc_ref[:, MISC_ALOW:MISC_ALOW + GLA_GATE_RANK].astype(BF16)
    xg = jnp.dot(a_low, wa_ref[...].astype(BF16), preferred_element_type=F32) + ba_ref[...]
    g_all = (jnp.minimum(xg, 0.0) - jnp.log1p(jnp.exp(-jnp.abs(xg)))) * (1.0 / GLA_GATE_NORMALIZER)

    r_i = lax.broadcasted_iota(jnp.int32, (chunk, chunk), 0)
    c_i = lax.broadcasted_iota(jnp.int32, (chunk, chunk), 1)
    tril = (r_i >= c_i).astype(BF16)
    row_in_chunk = lax.broadcasted_iota(jnp.int32, (chunk, dk), 0)
    lane3 = lax.broadcasted_iota(jnp.int32, (nsub, sub, sub), 2)
    row3 = lax.broadcasted_iota(jnp.int32, (nsub, sub, sub), 1)
    gn = gn_ref[...]

    for c in range(block // chunk):
        lo = c * chunk
        qf = q_ref[lo:lo + chunk, :].astype(F32) * (dk ** -0.5)
        kf = k_ref[lo:lo + chunk, :].astype(F32)
        vb = v_ref[lo:lo + chunk, :]
        g = g_all[lo:lo + chunk, :]

        g_hi = g.astype(BF16)
        r1 = g - g_hi.astype(F32)
        g_mid = r1.astype(BF16)
        g_lo = (r1 - g_mid.astype(F32)).astype(BF16)
        b = (jnp.dot(tril, g_hi, preferred_element_type=F32)
             + jnp.dot(tril, g_mid, preferred_element_type=F32)
             + jnp.dot(tril, g_lo, preferred_element_type=F32))

        state = s_ref[...]
        o = jnp.dot((qf * jnp.exp(b)).astype(BF16), state.astype(BF16), preferred_element_type=F32)

        m = chunk // 2
        while m >= sub:
            nseg = chunk // (2 * m)
            pieces = []
            for s in range(nseg):
                ref_row = b[s * 2 * m + m - 1:s * 2 * m + m, :]
                pieces.append(jnp.broadcast_to(ref_row, (2 * m, dk)))
            ref_full = pieces[0] if nseg == 1 else jnp.concatenate(pieces, axis=0)
            fac = jnp.exp(-jnp.abs(b - ref_full))
            is_q = (row_in_chunk & m) != 0
            xs = (jnp.where(is_q, qf, kf) * fac).astype(BF16)
            for s in range(nseg):
                base = s * 2 * m
                blk = lax.dot_general(xs[base + m:base + 2 * m, :], xs[base:base + m, :],
                                      (((1,), (1,)), ((), ())), preferred_element_type=F32)
                a_ref[base + m:base + 2 * m, base:base + m] = blk
            m //= 2

        q4 = qf.reshape(nsub, sub, dk)
        k4 = kf.reshape(nsub, sub, dk)
        b4 = b.reshape(nsub, sub, dk)
        ad = jnp.zeros((nsub, sub, sub), F32)
        for j in range(sub):
            e = jnp.exp(jnp.minimum(b4 - b4[:, j:j + 1, :], 0.0))
            col = jnp.sum(q4 * e * k4[:, j:j + 1, :], axis=-1, keepdims=True)
            ad = jnp.where(lane3 == j, col, ad)
        ad = jnp.where(row3 >= lane3, ad, 0.0)
        for i in range(nsub):
            a_ref[i * sub:(i + 1) * sub, i * sub:(i + 1) * sub] = ad[i]

        o = o + jnp.dot(a_ref[...].astype(BF16), vb, preferred_element_type=F32)

        b_last = b[chunk - 1:chunk, :]
        kd = (kf * jnp.exp(b_last - b)).astype(BF16)
        upd = lax.dot_general(kd, vb, (((0,), (0,)), ((), ())), preferred_element_type=F32)
        decay_col = jnp.transpose(jnp.broadcast_to(jnp.exp(b_last), (LANE, dk)))[:, 0:1]
        s_ref[...] = decay_col * state + upd

        o = o * lax.rsqrt(jnp.mean(o * o, axis=-1, keepdims=True) + EPS) * gn
        r = r_ref[lo:lo + chunk, :].astype(F32)
        o_ref[lo:lo + chunk, :] = (o * (r * jax.nn.sigmoid(r))).astype(o_ref.dtype)


def _gla(z_big, z_small, w_gate_up, b_gate, gla_norm_g, batch, seq):
    blk = min(GLA_BLOCK, seq)
    nb = seq // blk
    dk, dv = GLA_HEAD_K, GLA_HEAD_V
    row = lambda b, h, t: b * nb + t
    kern = functools.partial(_gla_kernel, block=blk, chunk=GLA_CHUNK)
    return pl.pallas_call(
        kern,
        grid=(batch, GLA_HEADS, nb),
        in_specs=[pl.BlockSpec((blk, dk), lambda b, h, t: (row(b, h, t), BIG_QG // dk + h)),
                  pl.BlockSpec((blk, dk), lambda b, h, t: (row(b, h, t), BIG_KG // dk + h)),
                  pl.BlockSpec((blk, dv), lambda b, h, t: (row(b, h, t), BIG_VG // dv + h)),
                  pl.BlockSpec((blk, dv), lambda b, h, t: (row(b, h, t), BIG_RG // dv + h)),
                  pl.BlockSpec((blk, LANE), lambda b, h, t: (row(b, h, t), SM_MISC // LANE)),
                  pl.BlockSpec((GLA_GATE_RANK, dk), lambda b, h, t: (0, h)),
                  pl.BlockSpec((1, dk), lambda b, h, t: (0, h)),
                  pl.BlockSpec((1, dv), lambda b, h, t: (0, 0))],
        out_specs=pl.BlockSpec((blk, dv), lambda b, h, t: (row(b, h, t), h)),
        out_shape=jax.ShapeDtypeStruct((batch * seq, GLA_V_W), BF16),
        scratch_shapes=[pltpu.VMEM((dk, dv), F32), pltpu.VMEM((GLA_CHUNK, GLA_CHUNK), F32)],
        compiler_params=_cparams(("arbitrary", "arbitrary", "arbitrary")),
        name="gla",
    )(z_big, z_big, z_big, z_big, z_small, w_gate_up, b_gate.reshape(1, -1), gla_norm_g.reshape(1, -1))


def _rel_bucket_np(d):
    max_exact = REL_BUCKETS // 2
    d = np.maximum(d, 0)
    df = np.maximum(d, 1).astype(np.float32)
    large = max_exact + (np.log(df / np.float32(max_exact)) / np.float32(math.log(REL_MAX_DIST / max_exact))
                         * np.float32(REL_BUCKETS - max_exact)).astype(np.int32)
    large = np.minimum(large, REL_BUCKETS - 1)
    return np.where(d < max_exact, d, large).astype(np.int32)


def _bias_kernel(rb_ref, bucket_ref, o_ref):
    h = pl.program_id(0)
    far = rb_ref[REL_BUCKETS - 1, h]
    for u in range(2):
        bk = bucket_ref[u]
        acc = jnp.zeros(bk.shape, F32)
        for bb in range(REL_BUCKETS):
            acc = jnp.where(bk == bb, rb_ref[bb, h], acc)
        o_ref[0, u] = acc - far


def _bias_tiles(rel_bias):
    j = np.arange(LANE)[:, None]
    i = np.arange(LANE)[None, :]
    bucket = np.stack([_rel_bucket_np(i - j), _rel_bucket_np(LANE + i - j)]).astype(np.int32)
    assert int(_rel_bucket_np(np.array([LANE]))[0]) == REL_BUCKETS - 1
    return pl.pallas_call(
        _bias_kernel,
        grid=(DSA_HEADS,),
        in_specs=[pl.BlockSpec(memory_space=pltpu.SMEM),
                  pl.BlockSpec((2, LANE, LANE), lambda h: (0, 0, 0))],
        out_specs=pl.BlockSpec((1, 2, LANE, LANE), lambda h: (h, 0, 0, 0)),
        out_shape=jax.ShapeDtypeStruct((DSA_HEADS, 2, LANE, LANE), F32),
        compiler_params=_cparams(("arbitrary",)),
        name="t5_bias_tiles",
    )(rel_bias, jnp.asarray(bucket))


def _key_to_float(key):
    return pltpu.bitcast(key ^ ((key >> 31) & 0x7FFFFFFF), F32)


def _dsa_kernel(q_ref, iq_ref, kv_ref, misc_all_ref, misc_q_ref, kvg_ref, wuv_ref, bias_ref, o_ref,
                kvn_ref, kvt_ref, ikb_ref, iqh_ref, sc_ref, thr_ref, qs_ref, lga_ref, lgb_ref, acc_ref, *, seq, k_sel):
    tq, hb, lat = DSA_TQ, DSA_HB, DSA_LATENT
    qi = pl.program_id(1)
    hg = pl.program_id(2)
    key_i = lax.broadcasted_iota(jnp.int32, (tq, tq), 0)
    qry_i = lax.broadcasted_iota(jnp.int32, (tq, tq), 1)
    causal = key_i <= qry_i
    nt = (((1,), (1,)), ((), ()))

    @pl.when((qi == 0) & (hg == 0))
    def _():
        kv = kv_ref[...]
        ms = jnp.mean(kv * kv, axis=-1, keepdims=True)
        kvn = kv * lax.rsqrt(ms + EPS) * kvg_ref[...]
        kvn_ref[...] = kvn.astype(BF16)
        for t in range(seq // tq):
            kvt_ref[t] = jnp.transpose(kvn[t * tq:(t + 1) * tq, :]).astype(BF16)
        ikb_ref[...] = misc_all_ref[:, MISC_IK:MISC_IK + IDX_DIM].astype(BF16)

    @pl.when(hg == 0)
    def _():
        for h in range(IDX_HEADS):
            iqh_ref[h] = iq_ref[:, h * IDX_DIM:(h + 1) * IDX_DIM].astype(BF16)
        iw_t = jnp.transpose(misc_q_ref[...]) * (IDX_HEADS ** -0.5)

        def score_tile(kt):
            off = pl.multiple_of(kt * tq, tq)
            ik_t = ikb_ref[pl.ds(off, tq), :]
            sc = jnp.zeros((tq, tq), F32)
            for h in range(IDX_HEADS):
                y = lax.dot_general(ik_t, iqh_ref[h], nt, preferred_element_type=F32)
                sc = sc + jnp.maximum(y * (IDX_DIM ** -0.5), 0.0) * iw_t[MISC_IW + h:MISC_IW + h + 1, :]
            return sc

        def far_body(kt, carry):
            sc_ref[kt] = score_tile(kt)
            return carry

        lax.fori_loop(0, qi, far_body, 0)
        sc_ref[qi] = jnp.where(causal, score_tile(qi), -jnp.inf)

        def bit_body(it, key):
            cand_key = key + (jnp.int32(1) << (31 - it))
            cand = _key_to_float(cand_key)

            def cnt_body(kt, cnt):
                hit = jnp.where(sc_ref[kt] >= cand, 1, 0)
                return cnt + jnp.sum(hit.reshape(tq // 8, 8, tq), axis=0)

            cnt = lax.fori_loop(0, qi + 1, cnt_body, jnp.zeros((8, tq), jnp.int32))
            tot = jnp.sum(cnt, axis=0, keepdims=True)
            return jnp.where((tot >= k_sel) | (cand_key <= NEG_INF_KEY), cand_key, key)

        key = lax.fori_loop(0, 32, bit_body, jnp.full((1, tq), INT_MIN, jnp.int32))
        thr_ref[...] = jnp.broadcast_to(_key_to_float(key), thr_ref.shape)

    thr = thr_ref[0:1, :]
    zero_blk = jnp.zeros((LANE, LANE), F32)

    for hh in range(hb):
        qs_ref[hh] = q_ref[:, hh * lat:(hh + 1) * lat] * (lat ** -0.5)
    acc_ref[...] = jnp.zeros_like(acc_ref)

    def logits(kt, buf):
        off = pl.multiple_of(kt * tq, tq)
        kv_t = kvn_ref[pl.ds(off, tq), :]
        for hh in range(hb):
            buf[hh] = lax.dot_general(kv_t, qs_ref[hh], nt, preferred_element_type=F32)

    def tile(kt, carry, buf, near=None):
        kv_tt = kvt_ref[kt]
        sel = sc_ref[kt] >= thr
        if near == "diag":
            sel = sel & causal
        out = []
        for hh in range(hb):
            m_run, l_run = carry[hh]
            s = buf[hh]
            if near == "prev":
                s = s + jnp.concatenate([jnp.concatenate([zero_blk, zero_blk], axis=1),
                                         jnp.concatenate([bias_ref[hh, 1], zero_blk], axis=1)], axis=0)
            elif near == "diag":
                s = s + jnp.concatenate([jnp.concatenate([bias_ref[hh, 0], bias_ref[hh, 1]], axis=1),
                                         jnp.concatenate([zero_blk, bias_ref[hh, 0]], axis=1)], axis=0)
            s = jnp.where(sel, s, NEG_BIG)
            m_new = jnp.maximum(m_run, jnp.max(s, axis=0, keepdims=True))
            alpha = jnp.exp(m_run - m_new)
            p = jnp.exp(s - m_new)
            l_new = alpha * l_run + jnp.sum(p, axis=0, keepdims=True)
            acc_ref[hh] = alpha * acc_ref[hh] + jnp.dot(kv_tt, p.astype(BF16), preferred_element_type=F32)
            out.append((m_new, l_new))
        return tuple(out)

    carry = tuple((jnp.full((1, tq), NEG_BIG, F32), jnp.zeros((1, tq), F32)) for _ in range(hb))
    n_far = jnp.maximum(qi - 1, 0)

    def diag_and_prev(cr):
        logits(qi - 1, lgb_ref)
        cr = tile(qi, cr, lga_ref, near="diag")
        logits(jnp.maximum(qi - 2, 0), lga_ref)
        return tile(qi - 1, cr, lgb_ref, near="prev")

    def far_pair(j, cr):
        kt_a = qi - 2 - 2 * j
        kt_b = kt_a - 1
        logits(kt_b, lgb_ref)
        cr = tile(kt_a, cr, lga_ref)
        logits(jnp.maximum(kt_b - 1, 0), lga_ref)
        return tile(kt_b, cr, lgb_ref)

    logits(qi, lga_ref)
    carry = lax.cond(qi > 0, diag_and_prev, lambda cr: tile(qi, cr, lga_ref, near="diag"), carry)
    carry = lax.fori_loop(0, n_far // 2, far_pair, carry)
    carry = lax.cond(n_far % 2 == 1, lambda cr: tile(jnp.int32(0), cr, lga_ref), lambda cr: cr, carry)

    for hh in range(hb):
        o_lat_t = (acc_ref[hh] / carry[hh][1]).astype(BF16)
        y_t = lax.dot_general(wuv_ref[hh], o_lat_t, (((0,), (0,)), ((), ())), preferred_element_type=F32)
        o_ref[:, hh * DSA_HEAD_DIM:(hh + 1) * DSA_HEAD_DIM] = jnp.transpose(y_t).astype(o_ref.dtype)


def _dsa(z_big, z_small, kv_norm_g, w_uv_bf16, bias_tiles, batch, seq):
    tq, hb, lat = DSA_TQ, DSA_HB, DSA_LATENT
    nq = seq // tq
    k_sel = min(DSA_TOPK, seq // 4)
    kern = functools.partial(_dsa_kernel, seq=seq, k_sel=k_sel)
    return pl.pallas_call(
        kern,
        grid=(batch, nq, DSA_HEADS // hb),
        in_specs=[pl.BlockSpec((tq, hb * lat), lambda b, i, g: (b * nq + i, BIG_QLAT // (hb * lat) + g)),
                  pl.BlockSpec((tq, IDX_HEADS * IDX_DIM), lambda b, i, g: (b * nq + i, 0)),
                  pl.BlockSpec((seq, lat), lambda b, i, g: (b, SM_KV // lat)),
                  pl.BlockSpec((seq, LANE), lambda b, i, g: (b, SM_MISC // LANE)),
                  pl.BlockSpec((tq, LANE), lambda b, i, g: (b * nq + i, SM_MISC // LANE)),
                  pl.BlockSpec((1, lat), lambda b, i, g: (0, 0)),
                  pl.BlockSpec((hb, lat, DSA_HEAD_DIM), lambda b, i, g: (g, 0, 0)),
                  pl.BlockSpec((hb, 2, LANE, LANE), lambda b, i, g: (g, 0, 0, 0))],
        out_specs=pl.BlockSpec((tq, hb * DSA_HEAD_DIM), lambda b, i, g: (b * nq + i, g)),
        out_shape=jax.ShapeDtypeStruct((batch * seq, DSA_HEADS * DSA_HEAD_DIM), BF16),
        scratch_shapes=[pltpu.VMEM((seq, lat), BF16),
                        pltpu.VMEM((nq, lat, tq), BF16),
                        pltpu.VMEM((seq, IDX_DIM), BF16),
                        pltpu.VMEM((IDX_HEADS, tq, IDX_DIM), BF16),
                        pltpu.VMEM((nq, tq, tq), F32),
                        pltpu.VMEM((8, tq), F32),
                        pltpu.VMEM((hb, tq, lat), BF16),
                        pltpu.VMEM((hb, tq, tq), F32),
                        pltpu.VMEM((hb, tq, tq), F32),
                        pltpu.VMEM((hb, lat, tq), F32)],
        compiler_params=_cparams(("arbitrary", "arbitrary", "arbitrary")),
        name="dsa",
    )(z_big, z_small, z_small, z_small, z_small, kv_norm_g.reshape(1, -1), w_uv_bf16, bias_tiles)


def _merge_mm_kernel(gg_ref, gd_ref, yg_ref, yd_ref, w_ref, x_ref, mod_ref, o_ref, m_ref):
    @pl.when(pl.program_id(1) == 0)
    def _():
        merged = (jax.nn.sigmoid(gg_ref[...].astype(F32)) * yg_ref[...].astype(F32)
                  + jax.nn.sigmoid(gd_ref[...].astype(F32)) * yd_ref[...].astype(F32))
        m_ref[...] = merged.astype(BF16)

    y = jnp.dot(m_ref[...], w_ref[...], preferred_element_type=F32)
    o_ref[...] = x_ref[...] + mod_ref[0, MOD_GT1:MOD_GT1 + 1, :] * y


def _merge_outproj(z_big, y_gla, y_dsa, w_out_bf16, x, mod, seq, tm, tn):
    m, d = x.shape
    tpb = seq // tm
    return pl.pallas_call(
        _merge_mm_kernel,
        grid=(m // tm, d // tn),
        in_specs=[pl.BlockSpec((tm, d), lambda i, j: (i, BIG_GGLA // d)),
                  pl.BlockSpec((tm, d), lambda i, j: (i, BIG_GDSA // d)),
                  pl.BlockSpec((tm, d), lambda i, j: (i, 0)),
                  pl.BlockSpec((tm, d), lambda i, j: (i, 0)),
                  pl.BlockSpec((d, tn), lambda i, j: (0, j)),
                  pl.BlockSpec((tm, tn), lambda i, j: (i, j)),
                  pl.BlockSpec((1, N_MOD, tn), lambda i, j: (i // tpb, 0, j))],
        out_specs=pl.BlockSpec((tm, tn), lambda i, j: (i, j)),
        out_shape=jax.ShapeDtypeStruct((m, d), F32),
        scratch_shapes=[pltpu.VMEM((tm, d), BF16)],
        compiler_params=_cparams(("arbitrary", "arbitrary")),
        name="merge_outproj",
    )(z_big, z_big, y_gla, y_dsa, w_out_bf16, x, mod)


def _ffn_kernel(x_ref, mod_ref, g_ref, w1_ref, w2_ref, fg_ref, o_ref, h_ref, acc_ref, *, final):
    j = pl.program_id(1)

    @pl.when(j == 0)
    def _():
        h = _rms_mod(x_ref[...], g_ref[...], mod_ref[0, MOD_SC2:MOD_SC2 + 1, :],
                     mod_ref[0, MOD_SH2:MOD_SH2 + 1, :])
        h_ref[...] = h.astype(BF16)
        acc_ref[...] = jnp.zeros_like(acc_ref)

    u = jnp.dot(h_ref[...], w1_ref[...], preferred_element_type=F32)
    u = jnp.square(jnp.maximum(u, 0.0)).astype(BF16)
    acc_ref[...] += jnp.dot(u, w2_ref[...], preferred_element_type=F32)

    @pl.when(j == pl.num_programs(1) - 1)
    def _():
        y = x_ref[...] + mod_ref[0, MOD_GT2:MOD_GT2 + 1, :] * acc_ref[...]
        if final:
            y = y * lax.rsqrt(jnp.mean(y * y, axis=-1, keepdims=True) + EPS) * fg_ref[...]
        o_ref[...] = y


def _ffn(x, mod, g, w1, w2, final_g, seq, tm, tf, final):
    m, d = x.shape
    f = w1.shape[1]
    tpb = seq // tm
    return pl.pallas_call(
        functools.partial(_ffn_kernel, final=final),
        grid=(m // tm, f // tf),
        in_specs=[pl.BlockSpec((tm, d), lambda i, j: (i, 0)),
                  pl.BlockSpec((1, N_MOD, d), lambda i, j: (i // tpb, 0, 0)),
                  pl.BlockSpec((1, d), lambda i, j: (0, 0)),
                  pl.BlockSpec((d, tf), lambda i, j: (0, j)),
                  pl.BlockSpec((tf, d), lambda i, j: (j, 0)),
                  pl.BlockSpec((1, d), lambda i, j: (0, 0))],
        out_specs=pl.BlockSpec((tm, d), lambda i, j: (i, 0)),
        out_shape=jax.ShapeDtypeStruct((m, d), F32),
        scratch_shapes=[pltpu.VMEM((tm, d), BF16), pltpu.VMEM((tm, d), F32)],
        compiler_params=_cparams(("arbitrary", "arbitrary")),
        name="ffn",
    )(x, mod, g, w1, w2, final_g)


def _split_w_in(w):
    parts, off = [], 0
    for s in IN_SIZES:
        parts.append(w[:, off:off + s])
        off += s
    q_g, k_g, v_g, a_low, r_g, q_lat, kv_lat, iq, ik, iw, g_gla, g_dsa = parts
    big = jnp.concatenate([q_lat, q_g, k_g, v_g, r_g, g_gla, g_dsa], axis=1).astype(BF16)
    pad = jnp.zeros((w.shape[0], SM_W - SM_MISC - IDX_DIM - GLA_GATE_RANK - IDX_HEADS), w.dtype)
    small = jnp.concatenate([iq, kv_lat, ik, a_low, iw, pad], axis=1).astype(BF16)
    return big, small


def kernel(x, c, w_mod, b_mod, norm1_g, w_in, w_gate_up, b_gate, gla_norm_g, kv_norm_g,
           w_uv, w_out, norm2_g, w_ff1, w_ff2, rel_bias, final_g):
    batch, seq, d = x.shape
    depth = w_mod.shape[0]
    m = batch * seq
    tm = min(1024, seq)

    mods = _modulation(c, w_mod, b_mod).reshape(depth, 8, N_MOD, d)
    bias_tiles = _bias_tiles(rel_bias)
    xf = x.reshape(m, d)
    fg = final_g.reshape(1, d)

    for l in range(depth):
        w_big, w_small = _split_w_in(w_in[l])
        mod = mods[l]
        g1 = norm1_g[l].reshape(1, d)
        z_big = _norm_matmul(xf, mod, g1, w_big, BF16, seq, tm, 512)
        z_small = _norm_matmul(xf, mod, g1, w_small, F32, seq, tm, LANE)
        y_gla = _gla(z_big, z_small, w_gate_up[l], b_gate[l], gla_norm_g[l], batch, seq)
        y_dsa = _dsa(z_big, z_small, kv_norm_g[l], w_uv[l].astype(BF16), bias_tiles, batch, seq)
        xf = _merge_outproj(z_big, y_gla, y_dsa, w_out[l].astype(BF16), xf, mod, seq, tm, 512)
        xf = _ffn(xf, mod, norm2_g[l].reshape(1, d), w_ff1[l].astype(BF16), w_ff2[l].astype(BF16),
                  fg, seq, min(512, seq), 512, final=(l == depth - 1))
    return xf.reshape(batch, seq, d)
```

```python
import functools
import math

import numpy as np
import jax
import jax.numpy as jnp
from jax import lax
from jax.experimental import pallas as pl
from jax.experimental.pallas import tpu as pltpu

F32 = jnp.float32
BF16 = jnp.bfloat16

D_MODEL = 2048
GLA_HEADS = 4
GLA_HEAD_K = 256
GLA_HEAD_V = 512
GLA_GATE_RANK = 16
GLA_GATE_NORMALIZER = 16.0
DSA_HEADS = 16
DSA_HEAD_DIM = 128
DSA_LATENT = 256
DSA_TOPK = 256
IDX_HEADS = 16
IDX_DIM = 64
REL_BUCKETS = 32
REL_MAX_DIST = 128
D_FF = 4 * D_MODEL
N_MOD = 6
EPS = 1e-6

GLA_QK_W = GLA_HEADS * GLA_HEAD_K
GLA_V_W = GLA_HEADS * GLA_HEAD_V
DSA_Q_W = DSA_HEADS * DSA_LATENT
IN_SIZES = (GLA_QK_W, GLA_QK_W, GLA_V_W, GLA_GATE_RANK, GLA_V_W,
            DSA_Q_W, DSA_LATENT, IDX_HEADS * IDX_DIM, IDX_DIM, IDX_HEADS,
            D_MODEL, D_MODEL)

BIG_QLAT = 0
BIG_QG = BIG_QLAT + DSA_Q_W
BIG_KG = BIG_QG + GLA_QK_W
BIG_VG = BIG_KG + GLA_QK_W
BIG_RG = BIG_VG + GLA_V_W
BIG_GGLA = BIG_RG + GLA_V_W
BIG_GDSA = BIG_GGLA + D_MODEL
BIG_W = BIG_GDSA + D_MODEL
SM_IQ = 0
SM_KV = SM_IQ + IDX_HEADS * IDX_DIM
SM_MISC = SM_KV + DSA_LATENT
MISC_IK = 0
MISC_ALOW = MISC_IK + IDX_DIM
MISC_IW = MISC_ALOW + GLA_GATE_RANK
LANE = 128
SM_W = SM_MISC + LANE
INPROJ_TN = 512
SM_WP = -(-SM_W // INPROJ_TN) * INPROJ_TN

MOD_SH1, MOD_SC1, MOD_GT1, MOD_SH2, MOD_SC2, MOD_GT2 = range(6)

GLA_CHUNK = 256
GLA_SUB = 8
LOG2E = 1.4426950408889634
GLA_BLOCK = 256
DSA_TQ = 256
DSA_HB = 8
DSA_PAD = 16
INT_MIN = -2 ** 31
NEG_INF_KEY = 0x807FFFFF - 2 ** 32
NEG_BIG = -1e30

VMEM_LIMIT = 56 * 1024 * 1024


def _cparams(sem):
    return pltpu.CompilerParams(dimension_semantics=sem, vmem_limit_bytes=VMEM_LIMIT)


def _rms_mod(x, g, sc, sh):
    ms = jnp.mean(x * x, axis=-1, keepdims=True)
    return (x * lax.rsqrt(ms + EPS) * g) * (1.0 + sc) + sh


def _mod_kernel(c_ref, w_ref, b_ref, o_ref):
    c = c_ref[...]
    c_act = (c * jax.nn.sigmoid(c)).astype(BF16)
    o_ref[0] = jnp.dot(c_act, w_ref[0].astype(BF16), preferred_element_type=F32) + b_ref[0]


def _modulation(c, w_mod, b_mod):
    depth, d, n = w_mod.shape
    tn = 1024
    c8 = jnp.zeros((8, d), F32).at[:c.shape[0]].set(c)
    return pl.pallas_call(
        _mod_kernel,
        grid=(depth, n // tn),
        in_specs=[pl.BlockSpec((8, d), lambda l, j: (0, 0)),
                  pl.BlockSpec((1, d, tn), lambda l, j: (l, 0, j)),
                  pl.BlockSpec((1, 1, tn), lambda l, j: (l, 0, j))],
        out_specs=pl.BlockSpec((1, 8, tn), lambda l, j: (l, 0, j)),
        out_shape=jax.ShapeDtypeStruct((depth, 8, n), F32),
        compiler_params=_cparams(("arbitrary", "arbitrary")),
        name="adaln_mod",
    )(c8, w_mod, b_mod.reshape(depth, 1, n))


def _norm_mm_kernel(x_ref, mod_ref, g_ref, w_ref, big_ref, small_ref, h_ref, *, n_big):
    j = pl.program_id(1)

    @pl.when(j == 0)
    def _():
        h = _rms_mod(x_ref[...], g_ref[...], mod_ref[0, MOD_SC1:MOD_SC1 + 1, :],
                     mod_ref[0, MOD_SH1:MOD_SH1 + 1, :])
        h_ref[...] = h.astype(BF16)

    y = jnp.dot(h_ref[...], w_ref[...], preferred_element_type=F32)

    @pl.when(j < n_big)
    def _():
        big_ref[...] = y.astype(big_ref.dtype)

    @pl.when(j >= n_big)
    def _():
        small_ref[...] = y


def _norm_matmul(x, mod, g, w, seq, tm, tn):
    m, d = x.shape
    n_big, n_small = BIG_W // tn, SM_WP // tn
    tpb = seq // tm
    return pl.pallas_call(
        functools.partial(_norm_mm_kernel, n_big=n_big),
        grid=(m // tm, n_big + n_small),
        in_specs=[pl.BlockSpec((tm, d), lambda i, j: (i, 0)),
                  pl.BlockSpec((1, N_MOD, d), lambda i, j: (i // tpb, 0, 0)),
                  pl.BlockSpec((1, d), lambda i, j: (0, 0)),
                  pl.BlockSpec((d, tn), lambda i, j: (0, j))],
        out_specs=[pl.BlockSpec((tm, tn), lambda i, j: (i, jnp.minimum(j, n_big - 1))),
                   pl.BlockSpec((tm, tn), lambda i, j: (i, jnp.maximum(j - n_big, 0)))],
        out_shape=[jax.ShapeDtypeStruct((m, BIG_W), BF16), jax.ShapeDtypeStruct((m, SM_WP), F32)],
        scratch_shapes=[pltpu.VMEM((tm, d), BF16)],
        compiler_params=_cparams(("arbitrary", "arbitrary")),
        name="norm_inproj",
    )(x, mod, g, w)


def _gla_kernel(q_ref, k_ref, v_ref, r_ref, misc_ref, wa_ref, ba_ref, gn_ref, o_ref,
                s_ref, a_ref, *, block, chunk):
    dk, dv, sub = GLA_HEAD_K, GLA_HEAD_V, GLA_SUB
    nsub = chunk // sub

    @pl.when(pl.program_id(2) == 0)
    def _():
        s_ref[...] = jnp.zeros_like(s_ref)

    a_ref[...] = jnp.zeros_like(a_ref)

    a_low = misc_ref[:, MISC_ALOW:MISC_ALOW + GLA_GATE_RANK].astype(BF16)
    xg = jnp.dot(a_low, wa_ref[...].astype(BF16), preferred_element_type=F32) + ba_ref[...]
    g_all = (jnp.minimum(xg, 0.0) - jnp.log1p(jnp.exp(-jnp.abs(xg)))) * (LOG2E / GLA_GATE_NORMALIZER)

    r_i = lax.broadcasted_iota(jnp.int32, (chunk, chunk), 0)
    c_i = lax.broadcasted_iota(jnp.int32, (chunk, chunk), 1)
    tril = (r_i >= c_i).astype(BF16)
    row_in_chunk = lax.broadcasted_iota(jnp.int32, (chunk, dk), 0)
    lane3 = lax.broadcasted_iota(jnp.int32, (nsub, sub, sub), 2)
    row3 = lax.broadcasted_iota(jnp.int32, (nsub, sub, sub), 1)
    gn = gn_ref[...]

    for c in range(block // chunk):
        lo = c * chunk
        qf = q_ref[lo:lo + chunk, :].astype(F32) * (dk ** -0.5)
        kf = k_ref[lo:lo + chunk, :].astype(F32)
        vb = v_ref[lo:lo + chunk, :]
        g = g_all[lo:lo + chunk, :]

        g_hi = g.astype(BF16)
        r1 = g - g_hi.astype(F32)
        g_mid = r1.astype(BF16)
        g_lo = (r1 - g_mid.astype(F32)).astype(BF16)
        b = (jnp.dot(tril, g_hi, preferred_element_type=F32)
             + jnp.dot(tril, g_mid, preferred_element_type=F32)
             + jnp.dot(tril, g_lo, preferred_element_type=F32))

        state = s_ref[...]
        o = jnp.dot((qf * jnp.exp2(b)).astype(BF16), state.astype(BF16), preferred_element_type=F32)

        m = chunk // 2
        while m >= sub:
            nseg = chunk // (2 * m)
            pieces = []
            for s in range(nseg):
                ref_row = b[s * 2 * m + m - 1:s * 2 * m + m, :]
                pieces.append(jnp.broadcast_to(ref_row, (2 * m, dk)))
            ref_full = pieces[0] if nseg == 1 else jnp.concatenate(pieces, axis=0)
            fac = jnp.exp2(-jnp.abs(b - ref_full))
            is_q = (row_in_chunk & m) != 0
            xs = jnp.where(is_q, qf, kf) * fac
            for s in range(nseg):
                base = s * 2 * m
                blk = lax.dot_general(xs[base + m:base + 2 * m, :].astype(BF16), xs[base:base + m, :].astype(BF16),
                                      (((1,), (1,)), ((), ())), preferred_element_type=F32)
                a_ref[base + m:base + 2 * m, base:base + m] = blk
            m //= 2

        q4 = qf.reshape(nsub, sub, dk)
        k4 = kf.reshape(nsub, sub, dk)
        b4 = b.reshape(nsub, sub, dk)
        ad = jnp.zeros((nsub, sub, sub), F32)
        for j in range(sub):
            e = jnp.exp2(b4 - b4[:, j:j + 1, :])
            col = jnp.sum(q4 * e * k4[:, j:j + 1, :], axis=-1, keepdims=True)
            ad = jnp.where(lane3 == j, col, ad)
        ad = jnp.where(row3 >= lane3, ad, 0.0)
        for i in range(nsub):
            a_ref[i * sub:(i + 1) * sub, i * sub:(i + 1) * sub] = ad[i]

        o = o + jnp.dot(a_ref[...].astype(BF16), vb, preferred_element_type=F32)

        b_last = b[chunk - 1:chunk, :]
        kd = (kf * jnp.exp2(b_last - b)).astype(BF16)
        upd = lax.dot_general(kd, vb, (((0,), (0,)), ((), ())), preferred_element_type=F32)
        decay_col = jnp.transpose(jnp.broadcast_to(jnp.exp2(b_last), (LANE, dk)))[:, 0:1]
        s_ref[...] = decay_col * state + upd

        o = o * lax.rsqrt(jnp.mean(o * o, axis=-1, keepdims=True) + EPS) * gn
        r = r_ref[lo:lo + chunk, :].astype(F32)
        o_ref[lo:lo + chunk, :] = (o * (r * jax.nn.sigmoid(r))).astype(o_ref.dtype)


def _gla(z_big, z_small, w_gate_up, b_gate, gla_norm_g, batch, seq):
    blk = min(GLA_BLOCK, seq)
    nb = seq // blk
    dk, dv = GLA_HEAD_K, GLA_HEAD_V
    row = lambda b, h, t: b * nb + t
    kern = functools.partial(_gla_kernel, block=blk, chunk=GLA_CHUNK)
    return pl.pallas_call(
        kern,
        grid=(batch, GLA_HEADS, nb),
        in_specs=[pl.BlockSpec((blk, dk), lambda b, h, t: (row(b, h, t), BIG_QG // dk + h)),
                  pl.BlockSpec((blk, dk), lambda b, h, t: (row(b, h, t), BIG_KG // dk + h)),
                  pl.BlockSpec((blk, dv), lambda b, h, t: (row(b, h, t), BIG_VG // dv + h)),
                  pl.BlockSpec((blk, dv), lambda b, h, t: (row(b, h, t), BIG_RG // dv + h)),
                  pl.BlockSpec((blk, LANE), lambda b, h, t: (row(b, h, t), SM_MISC // LANE)),
                  pl.BlockSpec((GLA_GATE_RANK, dk), lambda b, h, t: (0, h)),
                  pl.BlockSpec((1, dk), lambda b, h, t: (0, h)),
                  pl.BlockSpec((1, dv), lambda b, h, t: (0, 0))],
        out_specs=pl.BlockSpec((blk, dv), lambda b, h, t: (row(b, h, t), h)),
        out_shape=jax.ShapeDtypeStruct((batch * seq, GLA_V_W), BF16),
        scratch_shapes=[pltpu.VMEM((dk, dv), F32), pltpu.VMEM((GLA_CHUNK, GLA_CHUNK), F32)],
        compiler_params=_cparams(("arbitrary", "arbitrary", "arbitrary")),
        name="gla",
    )(z_big, z_big, z_big, z_big, z_small, w_gate_up, b_gate.reshape(1, -1), gla_norm_g.reshape(1, -1))


def _rel_bucket_np(d):
    max_exact = REL_BUCKETS // 2
    d = np.maximum(d, 0)
    df = np.maximum(d, 1).astype(np.float32)
    large = max_exact + (np.log(df / np.float32(max_exact)) / np.float32(math.log(REL_MAX_DIST / max_exact))
                         * np.float32(REL_BUCKETS - max_exact)).astype(np.int32)
    large = np.minimum(large, REL_BUCKETS - 1)
    return np.where(d < max_exact, d, large).astype(np.int32)


def _bias_kernel(rb_ref, bucket_ref, o_ref):
    h = pl.program_id(0)
    far = rb_ref[REL_BUCKETS - 1, h]
    for u in range(2):
        bk = bucket_ref[u]
        acc = jnp.zeros(bk.shape, F32)
        for bb in range(REL_BUCKETS):
            acc = jnp.where(bk == bb, rb_ref[bb, h], acc)
        o_ref[0, u] = acc - far


def _bias_tiles(rel_bias):
    j = np.arange(LANE)[:, None]
    i = np.arange(LANE)[None, :]
    bucket = np.stack([_rel_bucket_np(i - j), _rel_bucket_np(LANE + i - j)]).astype(np.int32)
    assert int(_rel_bucket_np(np.array([LANE]))[0]) == REL_BUCKETS - 1
    return pl.pallas_call(
        _bias_kernel,
        grid=(DSA_HEADS,),
        in_specs=[pl.BlockSpec(memory_space=pltpu.SMEM),
                  pl.BlockSpec((2, LANE, LANE), lambda h: (0, 0, 0))],
        out_specs=pl.BlockSpec((1, 2, LANE, LANE), lambda h: (h, 0, 0, 0)),
        out_shape=jax.ShapeDtypeStruct((DSA_HEADS, 2, LANE, LANE), F32),
        compiler_params=_cparams(("arbitrary",)),
        name="t5_bias_tiles",
    )(rel_bias, jnp.asarray(bucket))


def _key_to_float(key):
    return pltpu.bitcast(key ^ ((key >> 31) & 0x7FFFFFFF), F32)


def _dsa_kernel(q_ref, iq_ref, kv_ref, misc_all_ref, misc_q_ref, kvg_ref, wuv_ref, bias_ref, o_ref,
                kvn_ref, kvt_ref, ikb_ref, iqh_ref, sc_ref, qs_ref, lga_ref, lgb_ref, acc_ref, *, seq, k_sel):
    tq, hb, lat = DSA_TQ, DSA_HB, DSA_LATENT
    qi = pl.program_id(1)
    hg = pl.program_id(2)
    key_i = lax.broadcasted_iota(jnp.int32, (tq, tq), 0)
    qry_i = lax.broadcasted_iota(jnp.int32, (tq, tq), 1)
    causal = key_i <= qry_i
    nt = (((1,), (1,)), ((), ()))

    @pl.when((qi == 0) & (hg == 0))
    def _():
        kv = kv_ref[...]
        ms = jnp.mean(kv * kv, axis=-1, keepdims=True)
        kvn = kv * lax.rsqrt(ms + EPS) * kvg_ref[...]
        kvn_ref[...] = kvn.astype(BF16)
        ones_rows = (lax.broadcasted_iota(jnp.int32, (DSA_PAD, tq), 0) == 0).astype(BF16)
        for t in range(seq // tq):
            kvt_ref[t, 0:lat, :] = jnp.transpose(kvn[t * tq:(t + 1) * tq, :]).astype(BF16)
            kvt_ref[t, lat:lat + DSA_PAD, :] = ones_rows
        ikb_ref[...] = misc_all_ref[:, MISC_IK:MISC_IK + IDX_DIM].astype(BF16)

    @pl.when(hg == 0)
    def _():
        for h in range(IDX_HEADS):
            iqh_ref[h] = (iq_ref[:, h * IDX_DIM:(h + 1) * IDX_DIM] * (IDX_DIM ** -0.5)).astype(BF16)
        iw_t = jnp.transpose(misc_q_ref[...]) * (IDX_HEADS ** -0.5)

        def score_tile(kt):
            off = pl.multiple_of(kt * tq, tq)
            ik_t = ikb_ref[pl.ds(off, tq), :]
            sc = jnp.zeros((tq, tq), F32)
            for h in range(IDX_HEADS):
                y = lax.dot_general(ik_t, iqh_ref[h], nt, preferred_element_type=F32)
                sc = sc + jnp.maximum(y, 0.0) * iw_t[MISC_IW + h:MISC_IW + h + 1, :]
            return sc

        def far_body(kt, carry):
            sc_ref[kt] = score_tile(kt)
            return carry

        lax.fori_loop(0, qi, far_body, 0)
        sc_ref[qi] = jnp.where(causal, score_tile(qi), -jnp.inf)

        def bit_body(it, key):
            cand_key = key + (jnp.int32(1) << (31 - it))
            cand = _key_to_float(cand_key)

            def cnt_body(kt, cnt):
                hit = jnp.where(sc_ref[kt] >= cand, 1, 0)
                return cnt + jnp.sum(hit.reshape(tq // 8, 8, tq), axis=0)

            cnt = lax.fori_loop(0, qi + 1, cnt_body, jnp.zeros((8, tq), jnp.int32))
            tot = jnp.sum(cnt, axis=0, keepdims=True)
            return jnp.where((tot >= k_sel) | (cand_key <= NEG_INF_KEY), cand_key, key)

        key = lax.fori_loop(0, 32, bit_body, jnp.full((1, tq), INT_MIN, jnp.int32))
        thr = _key_to_float(key)

        def mask_body(kt, carry):
            sc_ref[kt] = jnp.where(sc_ref[kt] >= thr, 0.0, NEG_BIG)
            return carry

        lax.fori_loop(0, qi, mask_body, 0)
        sc_ref[qi] = jnp.where(causal & (sc_ref[qi] >= thr), 0.0, NEG_BIG)

    zero_blk = jnp.zeros((LANE, LANE), F32)

    for hh in range(hb):
        qs_ref[hh] = q_ref[:, hh * lat:(hh + 1) * lat] * (lat ** -0.5)
    acc_ref[...] = jnp.zeros_like(acc_ref)

    def logits(kt, buf):
        off = pl.multiple_of(kt * tq, tq)
        kv_t = kvn_ref[pl.ds(off, tq), :]
        for hh in range(hb):
            buf[hh] = lax.dot_general(kv_t, qs_ref[hh], nt, preferred_element_type=F32)

    def tile(kt, carry, buf, near=None):
        kv_tt = kvt_ref[kt]
        mask = sc_ref[kt]
        out = []
        for hh in range(hb):
            m_run = carry[hh]
            s = buf[hh] + mask
            if near == "prev":
                s = s + jnp.concatenate([jnp.concatenate([zero_blk, zero_blk], axis=1),
                                         jnp.concatenate([bias_ref[hh, 1], zero_blk], axis=1)], axis=0)
            elif near == "diag":
                s = s + jnp.concatenate([jnp.concatenate([bias_ref[hh, 0], bias_ref[hh, 1]], axis=1),
                                         jnp.concatenate([zero_blk, bias_ref[hh, 0]], axis=1)], axis=0)
            m_new = jnp.maximum(m_run, jnp.max(s, axis=0, keepdims=True))
            alpha = jnp.exp(m_run - m_new)
            p = jnp.exp(s - m_new)
            acc_ref[hh] = alpha * acc_ref[hh] + jnp.dot(kv_tt, p.astype(BF16), preferred_element_type=F32)
            out.append(m_new)
        return tuple(out)

    carry = tuple(jnp.full((1, tq), NEG_BIG, F32) for _ in range(hb))
    n_far = jnp.maximum(qi - 1, 0)

    def diag_and_prev(cr):
        logits(qi - 1, lgb_ref)
        cr = tile(qi, cr, lga_ref, near="diag")
        logits(jnp.maximum(qi - 2, 0), lga_ref)
        return tile(qi - 1, cr, lgb_ref, near="prev")

    def far_pair(j, cr):
        kt_a = qi - 2 - 2 * j
        kt_b = kt_a - 1
        logits(kt_b, lgb_ref)
        cr = tile(kt_a, cr, lga_ref)
        logits(jnp.maximum(kt_b - 1, 0), lga_ref)
        return tile(kt_b, cr, lgb_ref)

    logits(qi, lga_ref)
    carry = lax.cond(qi > 0, diag_and_prev, lambda cr: tile(qi, cr, lga_ref, near="diag"), carry)
    carry = lax.fori_loop(0, n_far // 2, far_pair, carry)
    carry = lax.cond(n_far % 2 == 1, lambda cr: tile(jnp.int32(0), cr, lga_ref), lambda cr: cr, carry)

    for hh in range(hb):
        o_lat_t = (acc_ref[hh, 0:lat, :] / acc_ref[hh, lat:lat + 1, :]).astype(BF16)
        y_t = lax.dot_general(wuv_ref[hh], o_lat_t, (((0,), (0,)), ((), ())), preferred_element_type=F32)
        o_ref[:, hh * DSA_HEAD_DIM:(hh + 1) * DSA_HEAD_DIM] = jnp.transpose(y_t).astype(o_ref.dtype)


def _dsa(z_big, z_small, kv_norm_g, w_uv_bf16, bias_tiles, batch, seq):
    tq, hb, lat = DSA_TQ, DSA_HB, DSA_LATENT
    nq = seq // tq
    k_sel = min(DSA_TOPK, seq // 4)
    kern = functools.partial(_dsa_kernel, seq=seq, k_sel=k_sel)
    return pl.pallas_call(
        kern,
        grid=(batch, nq, DSA_HEADS // hb),
        in_specs=[pl.BlockSpec((tq, hb * lat), lambda b, i, g: (b * nq + i, BIG_QLAT // (hb * lat) + g)),
                  pl.BlockSpec((tq, IDX_HEADS * IDX_DIM), lambda b, i, g: (b * nq + i, 0)),
                  pl.BlockSpec((seq, lat), lambda b, i, g: (b, SM_KV // lat)),
                  pl.BlockSpec((seq, LANE), lambda b, i, g: (b, SM_MISC // LANE)),
                  pl.BlockSpec((tq, LANE), lambda b, i, g: (b * nq + i, SM_MISC // LANE)),
                  pl.BlockSpec((1, lat), lambda b, i, g: (0, 0)),
                  pl.BlockSpec((hb, lat, DSA_HEAD_DIM), lambda b, i, g: (g, 0, 0)),
                  pl.BlockSpec((hb, 2, LANE, LANE), lambda b, i, g: (g, 0, 0, 0))],
        out_specs=pl.BlockSpec((tq, hb * DSA_HEAD_DIM), lambda b, i, g: (b * nq + i, g)),
        out_shape=jax.ShapeDtypeStruct((batch * seq, DSA_HEADS * DSA_HEAD_DIM), BF16),
        scratch_shapes=[pltpu.VMEM((seq, lat), BF16),
                        pltpu.VMEM((nq, lat + DSA_PAD, tq), BF16),
                        pltpu.VMEM((seq, IDX_DIM), BF16),
                        pltpu.VMEM((IDX_HEADS, tq, IDX_DIM), BF16),
                        pltpu.VMEM((nq, tq, tq), F32),
                        pltpu.VMEM((hb, tq, lat), BF16),
                        pltpu.VMEM((hb, tq, tq), F32),
                        pltpu.VMEM((hb, tq, tq), F32),
                        pltpu.VMEM((hb, lat + DSA_PAD, tq), F32)],
        compiler_params=_cparams(("arbitrary", "arbitrary", "arbitrary")),
        name="dsa",
    )(z_big, z_small, z_small, z_small, z_small, kv_norm_g.reshape(1, -1), w_uv_bf16, bias_tiles)


def _merge_mm_kernel(gg_ref, gd_ref, yg_ref, yd_ref, w_ref, x_ref, mod_ref, o_ref, m_ref):
    @pl.when(pl.program_id(1) == 0)
    def _():
        merged = (jax.nn.sigmoid(gg_ref[...].astype(F32)) * yg_ref[...].astype(F32)
                  + jax.nn.sigmoid(gd_ref[...].astype(F32)) * yd_ref[...].astype(F32))
        m_ref[...] = merged.astype(BF16)

    y = jnp.dot(m_ref[...], w_ref[...], preferred_element_type=F32)
    o_ref[...] = x_ref[...] + mod_ref[0, MOD_GT1:MOD_GT1 + 1, :] * y


def _merge_outproj(z_big, y_gla, y_dsa, w_out_bf16, x, mod, seq, tm, tn):
    m, d = x.shape
    tpb = seq // tm
    return pl.pallas_call(
        _merge_mm_kernel,
        grid=(m // tm, d // tn),
        in_specs=[pl.BlockSpec((tm, d), lambda i, j: (i, BIG_GGLA // d)),
                  pl.BlockSpec((tm, d), lambda i, j: (i, BIG_GDSA // d)),
                  pl.BlockSpec((tm, d), lambda i, j: (i, 0)),
                  pl.BlockSpec((tm, d), lambda i, j: (i, 0)),
                  pl.BlockSpec((d, tn), lambda i, j: (0, j)),
                  pl.BlockSpec((tm, tn), lambda i, j: (i, j)),
                  pl.BlockSpec((1, N_MOD, tn), lambda i, j: (i // tpb, 0, j))],
        out_specs=pl.BlockSpec((tm, tn), lambda i, j: (i, j)),
        out_shape=jax.ShapeDtypeStruct((m, d), F32),
        scratch_shapes=[pltpu.VMEM((tm, d), BF16)],
        compiler_params=_cparams(("arbitrary", "arbitrary")),
        name="merge_outproj",
    )(z_big, z_big, y_gla, y_dsa, w_out_bf16, x, mod)


def _ffn_kernel(x_ref, mod_ref, g_ref, w1_ref, w2_ref, fg_ref, o_ref, h_ref, acc_ref, *, final):
    j = pl.program_id(1)

    @pl.when(j == 0)
    def _():
        h = _rms_mod(x_ref[...], g_ref[...], mod_ref[0, MOD_SC2:MOD_SC2 + 1, :],
                     mod_ref[0, MOD_SH2:MOD_SH2 + 1, :])
        h_ref[...] = h.astype(BF16)
        acc_ref[...] = jnp.zeros_like(acc_ref)

    u = jnp.dot(h_ref[...], w1_ref[...], preferred_element_type=F32)
    u = jnp.square(jnp.maximum(u, 0.0)).astype(BF16)
    acc_ref[...] += jnp.dot(u, w2_ref[...], preferred_element_type=F32)

    @pl.when(j == pl.num_programs(1) - 1)
    def _():
        y = x_ref[...] + mod_ref[0, MOD_GT2:MOD_GT2 + 1, :] * acc_ref[...]
        if final:
            y = y * lax.rsqrt(jnp.mean(y * y, axis=-1, keepdims=True) + EPS) * fg_ref[...]
        o_ref[...] = y


def _ffn(x, mod, g, w1, w2, final_g, seq, tm, tf, final):
    m, d = x.shape
    f = w1.shape[1]
    tpb = seq // tm
    return pl.pallas_call(
        functools.partial(_ffn_kernel, final=final),
        grid=(m // tm, f // tf),
        in_specs=[pl.BlockSpec((tm, d), lambda i, j: (i, 0)),
                  pl.BlockSpec((1, N_MOD, d), lambda i, j: (i // tpb, 0, 0)),
                  pl.BlockSpec((1, d), lambda i, j: (0, 0)),
                  pl.BlockSpec((d, tf), lambda i, j: (0, j)),
                  pl.BlockSpec((tf, d), lambda i, j: (j, 0)),
                  pl.BlockSpec((1, d), lambda i, j: (0, 0))],
        out_specs=pl.BlockSpec((tm, d), lambda i, j: (i, 0)),
        out_shape=jax.ShapeDtypeStruct((m, d), F32),
        scratch_shapes=[pltpu.VMEM((tm, d), BF16), pltpu.VMEM((tm, d), F32)],
        compiler_params=_cparams(("arbitrary", "arbitrary")),
        name="ffn",
    )(x, mod, g, w1, w2, final_g)


def _split_w_in(w):
    parts, off = [], 0
    for s in IN_SIZES:
        parts.append(w[:, off:off + s])
        off += s
    q_g, k_g, v_g, a_low, r_g, q_lat, kv_lat, iq, ik, iw, g_gla, g_dsa = parts
    pad = jnp.zeros((w.shape[0], SM_WP - SM_MISC - IDX_DIM - GLA_GATE_RANK - IDX_HEADS), w.dtype)
    return jnp.concatenate([q_lat, q_g, k_g, v_g, r_g, g_gla, g_dsa,
                            iq, kv_lat, ik, a_low, iw, pad], axis=1).astype(BF16)


def kernel(x, c, w_mod, b_mod, norm1_g, w_in, w_gate_up, b_gate, gla_norm_g, kv_norm_g,
           w_uv, w_out, norm2_g, w_ff1, w_ff2, rel_bias, final_g):
    batch, seq, d = x.shape
    depth = w_mod.shape[0]
    m = batch * seq
    tm = min(1024, seq)

    mods = _modulation(c, w_mod, b_mod).reshape(depth, 8, N_MOD, d)
    bias_tiles = _bias_tiles(rel_bias)
    xf = x.reshape(m, d)
    fg = final_g.reshape(1, d)

    for l in range(depth):
        mod = mods[l]
        z_big, z_small = _norm_matmul(xf, mod, norm1_g[l].reshape(1, d), _split_w_in(w_in[l]), seq, tm, INPROJ_TN)
        y_gla = _gla(z_big, z_small, w_gate_up[l], b_gate[l], gla_norm_g[l], batch, seq)
        y_dsa = _dsa(z_big, z_small, kv_norm_g[l], w_uv[l].astype(BF16), bias_tiles, batch, seq)
        xf = _merge_outproj(z_big, y_gla, y_dsa, w_out[l].astype(BF16), xf, mod, seq, tm, 512)
        xf = _ffn(xf, mod, norm2_g[l].reshape(1, d), w_ff1[l].astype(BF16), w_ff2[l].astype(BF16),
                  fg, seq, min(512, seq), 512, final=(l == depth - 1))
    return xf.reshape(batch, seq, d)
```

```python
import functools
import math

import numpy as np
import jax
import jax.numpy as jnp
from jax import lax
from jax.experimental import pallas as pl
from jax.experimental.pallas import tpu as pltpu

F32 = jnp.float32
BF16 = jnp.bfloat16

D_MODEL = 2048
GLA_HEADS = 4
GLA_HEAD_K = 256
GLA_HEAD_V = 512
GLA_GATE_RANK = 16
GLA_GATE_NORMALIZER = 16.0
DSA_HEADS = 16
DSA_HEAD_DIM = 128
DSA_LATENT = 256
DSA_TOPK = 256
IDX_HEADS = 16
IDX_DIM = 64
REL_BUCKETS = 32
REL_MAX_DIST = 128
D_FF = 4 * D_MODEL
N_MOD = 6
EPS = 1e-6

GLA_QK_W = GLA_HEADS * GLA_HEAD_K
GLA_V_W = GLA_HEADS * GLA_HEAD_V
DSA_Q_W = DSA_HEADS * DSA_LATENT
IN_SIZES = (GLA_QK_W, GLA_QK_W, GLA_V_W, GLA_GATE_RANK, GLA_V_W,
            DSA_Q_W, DSA_LATENT, IDX_HEADS * IDX_DIM, IDX_DIM, IDX_HEADS,
            D_MODEL, D_MODEL)

BIG_QLAT = 0
BIG_QG = BIG_QLAT + DSA_Q_W
BIG_KG = BIG_QG + GLA_QK_W
BIG_VG = BIG_KG + GLA_QK_W
BIG_RG = BIG_VG + GLA_V_W
BIG_GGLA = BIG_RG + GLA_V_W
BIG_GDSA = BIG_GGLA + D_MODEL
BIG_W = BIG_GDSA + D_MODEL
SM_IQ = 0
SM_KV = SM_IQ + IDX_HEADS * IDX_DIM
SM_MISC = SM_KV + DSA_LATENT
MISC_IK = 0
MISC_ALOW = MISC_IK + IDX_DIM
MISC_IW = MISC_ALOW + GLA_GATE_RANK
LANE = 128
SM_W = SM_MISC + LANE
INPROJ_TN = 512
SM_WP = -(-SM_W // INPROJ_TN) * INPROJ_TN

MOD_SH1, MOD_SC1, MOD_GT1, MOD_SH2, MOD_SC2, MOD_GT2 = range(6)

GLA_CHUNK = 256
GLA_SUB = 8
LOG2E = 1.4426950408889634
GLA_BLOCK = 256
DSA_TQ = 256
DSA_HB = 8
DSA_PAD = 16
INT_MIN = -2 ** 31
NEG_INF_KEY = 0x807FFFFF - 2 ** 32
NEG_BIG = -1e30

VMEM_LIMIT = 56 * 1024 * 1024


def _cparams(sem):
    return pltpu.CompilerParams(dimension_semantics=sem, vmem_limit_bytes=VMEM_LIMIT)


def _rms_mod(x, g, sc, sh):
    ms = jnp.mean(x * x, axis=-1, keepdims=True)
    return (x * lax.rsqrt(ms + EPS) * g) * (1.0 + sc) + sh


def _mod_kernel(c_ref, w_ref, b_ref, o_ref):
    c = c_ref[...]
    c_act = (c * jax.nn.sigmoid(c)).astype(BF16)
    o_ref[0] = jnp.dot(c_act, w_ref[0].astype(BF16), preferred_element_type=F32) + b_ref[0]


def _modulation(c, w_mod, b_mod):
    depth, d, n = w_mod.shape
    tn = 1024
    c8 = jnp.zeros((8, d), F32).at[:c.shape[0]].set(c)
    return pl.pallas_call(
        _mod_kernel,
        grid=(depth, n // tn),
        in_specs=[pl.BlockSpec((8, d), lambda l, j: (0, 0)),
                  pl.BlockSpec((1, d, tn), lambda l, j: (l, 0, j)),
                  pl.BlockSpec((1, 1, tn), lambda l, j: (l, 0, j))],
        out_specs=pl.BlockSpec((1, 8, tn), lambda l, j: (l, 0, j)),
        out_shape=jax.ShapeDtypeStruct((depth, 8, n), F32),
        compiler_params=_cparams(("arbitrary", "arbitrary")),
        name="adaln_mod",
    )(c8, w_mod, b_mod.reshape(depth, 1, n))


def _norm_mm_kernel(x_ref, mod_ref, g_ref, w_ref, big_ref, small_ref, h_ref, *, n_big):
    j = pl.program_id(1)

    @pl.when(j == 0)
    def _():
        h = _rms_mod(x_ref[...], g_ref[...], mod_ref[0, MOD_SC1:MOD_SC1 + 1, :],
                     mod_ref[0, MOD_SH1:MOD_SH1 + 1, :])
        h_ref[...] = h.astype(BF16)

    y = jnp.dot(h_ref[...], w_ref[...], preferred_element_type=F32)

    @pl.when(j < n_big)
    def _():
        big_ref[...] = y.astype(big_ref.dtype)

    @pl.when(j >= n_big)
    def _():
        small_ref[...] = y


def _norm_matmul(x, mod, g, w, seq, tm, tn):
    m, d = x.shape
    n_big, n_small = BIG_W // tn, SM_WP // tn
    tpb = seq // tm
    return pl.pallas_call(
        functools.partial(_norm_mm_kernel, n_big=n_big),
        grid=(m // tm, n_big + n_small),
        in_specs=[pl.BlockSpec((tm, d), lambda i, j: (i, 0)),
                  pl.BlockSpec((1, N_MOD, d), lambda i, j: (i // tpb, 0, 0)),
                  pl.BlockSpec((1, d), lambda i, j: (0, 0)),
                  pl.BlockSpec((d, tn), lambda i, j: (0, j))],
        out_specs=[pl.BlockSpec((tm, tn), lambda i, j: (i, jnp.minimum(j, n_big - 1))),
                   pl.BlockSpec((tm, tn), lambda i, j: (i, jnp.maximum(j - n_big, 0)))],
        out_shape=[jax.ShapeDtypeStruct((m, BIG_W), BF16), jax.ShapeDtypeStruct((m, SM_WP), F32)],
        scratch_shapes=[pltpu.VMEM((tm, d), BF16)],
        compiler_params=_cparams(("arbitrary", "arbitrary")),
        name="norm_inproj",
    )(x, mod, g, w)


def _gla_kernel(q_ref, k_ref, v_ref, r_ref, misc_ref, wa_ref, ba_ref, gn_ref, o_ref,
                s_ref, a_ref, *, block, chunk):
    dk, dv, sub = GLA_HEAD_K, GLA_HEAD_V, GLA_SUB
    nsub = chunk // sub

    @pl.when(pl.program_id(2) == 0)
    def _():
        s_ref[...] = jnp.zeros_like(s_ref)

    a_ref[...] = jnp.zeros_like(a_ref)

    a_low = misc_ref[:, MISC_ALOW:MISC_ALOW + GLA_GATE_RANK].astype(BF16)
    xg = jnp.dot(a_low, wa_ref[...].astype(BF16), preferred_element_type=F32) + ba_ref[...]
    g_all = (jnp.minimum(xg, 0.0) - jnp.log1p(jnp.exp(-jnp.abs(xg)))) * (LOG2E / GLA_GATE_NORMALIZER)

    r_i = lax.broadcasted_iota(jnp.int32, (chunk, chunk), 0)
    c_i = lax.broadcasted_iota(jnp.int32, (chunk, chunk), 1)
    tril = (r_i >= c_i).astype(BF16)
    row_in_chunk = lax.broadcasted_iota(jnp.int32, (chunk, dk), 0)
    lane3 = lax.broadcasted_iota(jnp.int32, (nsub, sub, sub), 2)
    row3 = lax.broadcasted_iota(jnp.int32, (nsub, sub, sub), 1)
    gn = gn_ref[...]

    for c in range(block // chunk):
        lo = c * chunk
        qf = q_ref[lo:lo + chunk, :].astype(F32) * (dk ** -0.5)
        kf = k_ref[lo:lo + chunk, :].astype(F32)
        vb = v_ref[lo:lo + chunk, :]
        g = g_all[lo:lo + chunk, :]

        g_hi = g.astype(BF16)
        r1 = g - g_hi.astype(F32)
        g_mid = r1.astype(BF16)
        g_lo = (r1 - g_mid.astype(F32)).astype(BF16)
        b = (jnp.dot(tril, g_hi, preferred_element_type=F32)
             + jnp.dot(tril, g_mid, preferred_element_type=F32)
             + jnp.dot(tril, g_lo, preferred_element_type=F32))

        state = s_ref[...]
        o = jnp.dot((qf * jnp.exp2(b)).astype(BF16), state.astype(BF16), preferred_element_type=F32)

        m = chunk // 2
        while m >= sub:
            nseg = chunk // (2 * m)
            pieces = []
            for s in range(nseg):
                ref_row = b[s * 2 * m + m - 1:s * 2 * m + m, :]
                pieces.append(jnp.broadcast_to(ref_row, (2 * m, dk)))
            ref_full = pieces[0] if nseg == 1 else jnp.concatenate(pieces, axis=0)
            fac = jnp.exp2(-jnp.abs(b - ref_full))
            is_q = (row_in_chunk & m) != 0
            xs = jnp.where(is_q, qf, kf) * fac
            for s in range(nseg):
                base = s * 2 * m
                blk = lax.dot_general(xs[base + m:base + 2 * m, :].astype(BF16), xs[base:base + m, :].astype(BF16),
                                      (((1,), (1,)), ((), ())), preferred_element_type=F32)
                a_ref[base + m:base + 2 * m, base:base + m] = blk
            m //= 2

        q4 = qf.reshape(nsub, sub, dk)
        k4 = kf.reshape(nsub, sub, dk)
        b4 = b.reshape(nsub, sub, dk)
        ad = jnp.zeros((nsub, sub, sub), F32)
        for j in range(sub):
            e = jnp.exp2(b4 - b4[:, j:j + 1, :])
            col = jnp.sum(q4 * e * k4[:, j:j + 1, :], axis=-1, keepdims=True)
            ad = jnp.where(lane3 == j, col, ad)
        ad = jnp.where(row3 >= lane3, ad, 0.0)
        for i in range(nsub):
            a_ref[i * sub:(i + 1) * sub, i * sub:(i + 1) * sub] = ad[i]

        o = o + jnp.dot(a_ref[...].astype(BF16), vb, preferred_element_type=F32)

        b_last = b[chunk - 1:chunk, :]
        kd = (kf * jnp.exp2(b_last - b)).astype(BF16)
        upd = lax.dot_general(kd, vb, (((0,), (0,)), ((), ())), preferred_element_type=F32)
        decay_col = jnp.transpose(jnp.broadcast_to(jnp.exp2(b_last), (LANE, dk)))[:, 0:1]
        s_ref[...] = decay_col * state + upd

        o = o * lax.rsqrt(jnp.mean(o * o, axis=-1, keepdims=True) + EPS) * gn
        r = r_ref[lo:lo + chunk, :].astype(F32)
        o_ref[lo:lo + chunk, :] = (o * (r * jax.nn.sigmoid(r))).astype(o_ref.dtype)


def _gla(z_big, z_small, w_gate_up, b_gate, gla_norm_g, batch, seq):
    blk = min(GLA_BLOCK, seq)
    nb = seq // blk
    dk, dv = GLA_HEAD_K, GLA_HEAD_V
    row = lambda b, h, t: b * nb + t
    kern = functools.partial(_gla_kernel, block=blk, chunk=GLA_CHUNK)
    return pl.pallas_call(
        kern,
        grid=(batch, GLA_HEADS, nb),
        in_specs=[pl.BlockSpec((blk, dk), lambda b, h, t: (row(b, h, t), BIG_QG // dk + h)),
                  pl.BlockSpec((blk, dk), lambda b, h, t: (row(b, h, t), BIG_KG // dk + h)),
                  pl.BlockSpec((blk, dv), lambda b, h, t: (row(b, h, t), BIG_VG // dv + h)),
                  pl.BlockSpec((blk, dv), lambda b, h, t: (row(b, h, t), BIG_RG // dv + h)),
                  pl.BlockSpec((blk, LANE), lambda b, h, t: (row(b, h, t), SM_MISC // LANE)),
                  pl.BlockSpec((GLA_GATE_RANK, dk), lambda b, h, t: (0, h)),
                  pl.BlockSpec((1, dk), lambda b, h, t: (0, h)),
                  pl.BlockSpec((1, dv), lambda b, h, t: (0, 0))],
        out_specs=pl.BlockSpec((blk, dv), lambda b, h, t: (row(b, h, t), h)),
        out_shape=jax.ShapeDtypeStruct((batch * seq, GLA_V_W), BF16),
        scratch_shapes=[pltpu.VMEM((dk, dv), F32), pltpu.VMEM((GLA_CHUNK, GLA_CHUNK), F32)],
        compiler_params=_cparams(("arbitrary", "arbitrary", "arbitrary")),
        name="gla",
    )(z_big, z_big, z_big, z_big, z_small, w_gate_up, b_gate.reshape(1, -1), gla_norm_g.reshape(1, -1))


def _rel_bucket_np(d):
    max_exact = REL_BUCKETS // 2
    d = np.maximum(d, 0)
    df = np.maximum(d, 1).astype(np.float32)
    large = max_exact + (np.log(df / np.float32(max_exact)) / np.float32(math.log(REL_MAX_DIST / max_exact))
                         * np.float32(REL_BUCKETS - max_exact)).astype(np.int32)
    large = np.minimum(large, REL_BUCKETS - 1)
    return np.where(d < max_exact, d, large).astype(np.int32)


def _bias_kernel(rb_ref, bucket_ref, o_ref):
    h = pl.program_id(0)
    far = rb_ref[REL_BUCKETS - 1, h]
    for u in range(2):
        bk = bucket_ref[u]
        acc = jnp.zeros(bk.shape, F32)
        for bb in range(REL_BUCKETS):
            acc = jnp.where(bk == bb, rb_ref[bb, h], acc)
        o_ref[0, u] = acc - far


def _bias_tiles(rel_bias):
    j = np.arange(LANE)[:, None]
    i = np.arange(LANE)[None, :]
    bucket = np.stack([_rel_bucket_np(i - j), _rel_bucket_np(LANE + i - j)]).astype(np.int32)
    assert int(_rel_bucket_np(np.array([LANE]))[0]) == REL_BUCKETS - 1
    return pl.pallas_call(
        _bias_kernel,
        grid=(DSA_HEADS,),
        in_specs=[pl.BlockSpec(memory_space=pltpu.SMEM),
                  pl.BlockSpec((2, LANE, LANE), lambda h: (0, 0, 0))],
        out_specs=pl.BlockSpec((1, 2, LANE, LANE), lambda h: (h, 0, 0, 0)),
        out_shape=jax.ShapeDtypeStruct((DSA_HEADS, 2, LANE, LANE), F32),
        compiler_params=_cparams(("arbitrary",)),
        name="t5_bias_tiles",
    )(rel_bias, jnp.asarray(bucket))


def _key_to_float(key):
    return pltpu.bitcast(key ^ ((key >> 31) & 0x7FFFFFFF), F32)


def _dsa_kernel(q_ref, iq_ref, kv_ref, misc_all_ref, misc_q_ref, kvg_ref, wuv_ref, bias_ref, o_ref,
                kvn_ref, kvt_ref, ikb_ref, iqh_ref, sc_ref, qs_ref, lga_ref, lgb_ref, acc_ref, *, seq, k_sel):
    tq, hb, lat = DSA_TQ, DSA_HB, DSA_LATENT
    qi = pl.program_id(1)
    hg = pl.program_id(2)
    key_i = lax.broadcasted_iota(jnp.int32, (tq, tq), 0)
    qry_i = lax.broadcasted_iota(jnp.int32, (tq, tq), 1)
    causal = key_i <= qry_i
    nt = (((1,), (1,)), ((), ()))

    @pl.when((qi == 0) & (hg == 0))
    def _():
        kv = kv_ref[...]
        ms = jnp.mean(kv * kv, axis=-1, keepdims=True)
        kvn = kv * lax.rsqrt(ms + EPS) * kvg_ref[...]
        kvn_ref[...] = kvn.astype(BF16)
        ones_rows = (lax.broadcasted_iota(jnp.int32, (DSA_PAD, tq), 0) == 0).astype(BF16)
        for t in range(seq // tq):
            kvt_ref[t, 0:lat, :] = jnp.transpose(kvn[t * tq:(t + 1) * tq, :]).astype(BF16)
            kvt_ref[t, lat:lat + DSA_PAD, :] = ones_rows
        ikb_ref[...] = misc_all_ref[:, MISC_IK:MISC_IK + IDX_DIM].astype(BF16)

    @pl.when(hg == 0)
    def _():
        for h in range(IDX_HEADS):
            iqh_ref[h] = (iq_ref[:, h * IDX_DIM:(h + 1) * IDX_DIM] * (IDX_DIM ** -0.5)).astype(BF16)
        iw_t = jnp.transpose(misc_q_ref[...]) * (IDX_HEADS ** -0.5)

        def score_tile(kt):
            off = pl.multiple_of(kt * tq, tq)
            ik_t = ikb_ref[pl.ds(off, tq), :]
            sc = jnp.zeros((tq, tq), F32)
            for h in range(IDX_HEADS):
                y = lax.dot_general(ik_t, iqh_ref[h], nt, preferred_element_type=F32)
                sc = sc + jnp.maximum(y, 0.0) * iw_t[MISC_IW + h:MISC_IW + h + 1, :]
            return sc

        def far_body(kt, carry):
            sc_ref[kt] = score_tile(kt)
            return carry

        lax.fori_loop(0, qi, far_body, 0)
        sc_ref[qi] = jnp.where(causal, score_tile(qi), -jnp.inf)

        def bit_body(it, key):
            cand_key = key + (jnp.int32(1) << (31 - it))
            cand = _key_to_float(cand_key)

            def cnt_body(kt, cnt):
                hit = jnp.where(sc_ref[kt] >= cand, 1, 0)
                return cnt + jnp.sum(hit.reshape(tq // 8, 8, tq), axis=0)

            cnt = lax.fori_loop(0, qi + 1, cnt_body, jnp.zeros((8, tq), jnp.int32))
            tot = jnp.sum(cnt, axis=0, keepdims=True)
            return jnp.where((tot >= k_sel) | (cand_key <= NEG_INF_KEY), cand_key, key)

        key = lax.fori_loop(0, 32, bit_body, jnp.full((1, tq), INT_MIN, jnp.int32))
        thr = _key_to_float(key)

        def mask_body(kt, carry):
            sc_ref[kt] = jnp.where(sc_ref[kt] >= thr, 0.0, NEG_BIG)
            return carry

        lax.fori_loop(0, qi, mask_body, 0)
        sc_ref[qi] = jnp.where(causal & (sc_ref[qi] >= thr), 0.0, NEG_BIG)

    zero_blk = jnp.zeros((LANE, LANE), F32)

    for hh in range(hb):
        qs_ref[hh] = q_ref[:, hh * lat:(hh + 1) * lat] * (lat ** -0.5)
    acc_ref[...] = jnp.zeros_like(acc_ref)

    def logits(kt, buf):
        off = pl.multiple_of(kt * tq, tq)
        kv_t = kvn_ref[pl.ds(off, tq), :]
        for hh in range(hb):
            buf[hh] = lax.dot_general(kv_t, qs_ref[hh], nt, preferred_element_type=F32)

    def tile(kt, carry, buf, near=None):
        kv_tt = kvt_ref[kt]
        mask = sc_ref[kt]
        out = []
        for hh in range(hb):
            m_run = carry[hh]
            s = buf[hh] + mask
            if near == "prev":
                s = s + jnp.concatenate([jnp.concatenate([zero_blk, zero_blk], axis=1),
                                         jnp.concatenate([bias_ref[hh, 1], zero_blk], axis=1)], axis=0)
            elif near == "diag":
                s = s + jnp.concatenate([jnp.concatenate([bias_ref[hh, 0], bias_ref[hh, 1]], axis=1),
                                         jnp.concatenate([zero_blk, bias_ref[hh, 0]], axis=1)], axis=0)
            m_new = jnp.maximum(m_run, jnp.max(s, axis=0, keepdims=True))
            alpha = jnp.exp(m_run - m_new)
            p = jnp.exp(s - m_new)
            acc_ref[hh] = alpha * acc_ref[hh] + jnp.dot(kv_tt, p.astype(BF16), preferred_element_type=F32)
            out.append(m_new)
        return tuple(out)

    carry = tuple(jnp.full((1, tq), NEG_BIG, F32) for _ in range(hb))
    n_far = jnp.maximum(qi - 1, 0)

    def diag_and_prev(cr):
        logits(qi - 1, lgb_ref)
        cr = tile(qi, cr, lga_ref, near="diag")
        logits(jnp.maximum(qi - 2, 0), lga_ref)
        return tile(qi - 1, cr, lgb_ref, near="prev")

    def far_pair(j, cr):
        kt_a = qi - 2 - 2 * j
        kt_b = kt_a - 1
        logits(kt_b, lgb_ref)
        cr = tile(kt_a, cr, lga_ref)
        logits(jnp.maximum(kt_b - 1, 0), lga_ref)
        return tile(kt_b, cr, lgb_ref)

    logits(qi, lga_ref)
    carry = lax.cond(qi > 0, diag_and_prev, lambda cr: tile(qi, cr, lga_ref, near="diag"), carry)
    carry = lax.fori_loop(0, n_far // 2, far_pair, carry)
    carry = lax.cond(n_far % 2 == 1, lambda cr: tile(jnp.int32(0), cr, lga_ref), lambda cr: cr, carry)

    for hh in range(hb):
        o_lat_t = (acc_ref[hh, 0:lat, :] / acc_ref[hh, lat:lat + 1, :]).astype(BF16)
        y_t = lax.dot_general(wuv_ref[hh], o_lat_t, (((0,), (0,)), ((), ())), preferred_element_type=F32)
        o_ref[:, hh * DSA_HEAD_DIM:(hh + 1) * DSA_HEAD_DIM] = jnp.transpose(y_t).astype(o_ref.dtype)


def _dsa(z_big, z_small, kv_norm_g, w_uv_bf16, bias_tiles, batch, seq):
    tq, hb, lat = DSA_TQ, DSA_HB, DSA_LATENT
    nq = seq // tq
    k_sel = min(DSA_TOPK, seq // 4)
    kern = functools.partial(_dsa_kernel, seq=seq, k_sel=k_sel)
    return pl.pallas_call(
        kern,
        grid=(batch, nq, DSA_HEADS // hb),
        in_specs=[pl.BlockSpec((tq, hb * lat), lambda b, i, g: (b * nq + i, BIG_QLAT // (hb * lat) + g)),
                  pl.BlockSpec((tq, IDX_HEADS * IDX_DIM), lambda b, i, g: (b * nq + i, 0)),
                  pl.BlockSpec((seq, lat), lambda b, i, g: (b, SM_KV // lat)),
                  pl.BlockSpec((seq, LANE), lambda b, i, g: (b, SM_MISC // LANE)),
                  pl.BlockSpec((tq, LANE), lambda b, i, g: (b * nq + i, SM_MISC // LANE)),
                  pl.BlockSpec((1, lat), lambda b, i, g: (0, 0)),
                  pl.BlockSpec((hb, lat, DSA_HEAD_DIM), lambda b, i, g: (g, 0, 0)),
                  pl.BlockSpec((hb, 2, LANE, LANE), lambda b, i, g: (g, 0, 0, 0))],
        out_specs=pl.BlockSpec((tq, hb * DSA_HEAD_DIM), lambda b, i, g: (b * nq + i, g)),
        out_shape=jax.ShapeDtypeStruct((batch * seq, DSA_HEADS * DSA_HEAD_DIM), BF16),
        scratch_shapes=[pltpu.VMEM((seq, lat), BF16),
                        pltpu.VMEM((nq, lat + DSA_PAD, tq), BF16),
                        pltpu.VMEM((seq, IDX_DIM), BF16),
                        pltpu.VMEM((IDX_HEADS, tq, IDX_DIM), BF16),
                        pltpu.VMEM((nq, tq, tq), F32),
                        pltpu.VMEM((hb, tq, lat), BF16),
                        pltpu.VMEM((hb, tq, tq), F32),
                        pltpu.VMEM((hb, tq, tq), F32),
                        pltpu.VMEM((hb, lat + DSA_PAD, tq), F32)],
        compiler_params=_cparams(("arbitrary", "arbitrary", "arbitrary")),
        name="dsa",
    )(z_big, z_small, z_small, z_small, z_small, kv_norm_g.reshape(1, -1), w_uv_bf16, bias_tiles)


def _merge_mm_kernel(gg_ref, gd_ref, yg_ref, yd_ref, w_ref, x_ref, mod_ref, o_ref, wb_ref):
    @pl.when(pl.program_id(0) == 0)
    def _():
        wb_ref[...] = w_ref[0].astype(BF16)

    merged = (jax.nn.sigmoid(gg_ref[...].astype(F32)) * yg_ref[...].astype(F32)
              + jax.nn.sigmoid(gd_ref[...].astype(F32)) * yd_ref[...].astype(F32))
    y = jnp.dot(merged.astype(BF16), wb_ref[...], preferred_element_type=F32)
    o_ref[...] = x_ref[...] + mod_ref[0, MOD_GT1:MOD_GT1 + 1, :] * y


def _merge_outproj(z_big, y_gla, y_dsa, w_out, layer, x, mod, seq, tm):
    m, d = x.shape
    tpb = seq // tm
    return pl.pallas_call(
        _merge_mm_kernel,
        grid=(m // tm,),
        in_specs=[pl.BlockSpec((tm, d), lambda i: (i, BIG_GGLA // d)),
                  pl.BlockSpec((tm, d), lambda i: (i, BIG_GDSA // d)),
                  pl.BlockSpec((tm, d), lambda i: (i, 0)),
                  pl.BlockSpec((tm, d), lambda i: (i, 0)),
                  pl.BlockSpec((1, d, d), lambda i: (layer, 0, 0), pipeline_mode=pl.Buffered(1)),
                  pl.BlockSpec((tm, d), lambda i: (i, 0)),
                  pl.BlockSpec((1, N_MOD, d), lambda i: (i // tpb, 0, 0))],
        out_specs=pl.BlockSpec((tm, d), lambda i: (i, 0)),
        out_shape=jax.ShapeDtypeStruct((m, d), F32),
        scratch_shapes=[pltpu.VMEM((d, d), BF16)],
        compiler_params=_cparams(("arbitrary",)),
        name="merge_outproj",
    )(z_big, z_big, y_gla, y_dsa, w_out, x, mod)


def _ffn_kernel(x_ref, mod_ref, g_ref, w1_ref, w2_ref, fg_ref, o_ref, h_ref, *, final):
    j = pl.program_id(1)

    @pl.when(j == 0)
    def _():
        h = _rms_mod(x_ref[...], g_ref[0], mod_ref[0, MOD_SC2:MOD_SC2 + 1, :],
                     mod_ref[0, MOD_SH2:MOD_SH2 + 1, :])
        h_ref[...] = h.astype(BF16)
        o_ref[...] = jnp.zeros_like(o_ref)

    u = jnp.dot(h_ref[...], w1_ref[0].astype(BF16), preferred_element_type=F32)
    u = jnp.square(jnp.maximum(u, 0.0)).astype(BF16)
    tf = u.shape[1]
    for n0 in range(0, o_ref.shape[1], tf):
        o_ref[:, n0:n0 + tf] += jnp.dot(u, w2_ref[0, :, n0:n0 + tf].astype(BF16), preferred_element_type=F32)

    @pl.when(j == pl.num_programs(1) - 1)
    def _():
        out = x_ref[...] + mod_ref[0, MOD_GT2:MOD_GT2 + 1, :] * o_ref[...]
        if final:
            out = out * lax.rsqrt(jnp.mean(out * out, axis=-1, keepdims=True) + EPS) * fg_ref[...]
        o_ref[...] = out


def _ffn(x, mod, norm2_g, w1, w2, layer, final_g, seq, tm, tf, final):
    m, d = x.shape
    f = w1.shape[2]
    tpb = seq // tm
    return pl.pallas_call(
        functools.partial(_ffn_kernel, final=final),
        grid=(m // tm, f // tf),
        in_specs=[pl.BlockSpec((tm, d), lambda i, j: (i, 0), pipeline_mode=pl.Buffered(1)),
                  pl.BlockSpec((1, N_MOD, d), lambda i, j: (i // tpb, 0, 0)),
                  pl.BlockSpec((1, 1, d), lambda i, j: (layer, 0, 0)),
                  pl.BlockSpec((1, d, tf), lambda i, j: (layer, 0, j)),
                  pl.BlockSpec((1, tf, d), lambda i, j: (layer, j, 0)),
                  pl.BlockSpec((1, d), lambda i, j: (0, 0))],
        out_specs=pl.BlockSpec((tm, d), lambda i, j: (i, 0)),
        out_shape=jax.ShapeDtypeStruct((m, d), F32),
        scratch_shapes=[pltpu.VMEM((tm, d), BF16)],
        compiler_params=_cparams(("arbitrary", "arbitrary")),
        name="ffn",
    )(x, mod, norm2_g.reshape(norm2_g.shape[0], 1, d), w1, w2, final_g)


def _split_w_in(w):
    parts, off = [], 0
    for s in IN_SIZES:
        parts.append(w[:, off:off + s])
        off += s
    q_g, k_g, v_g, a_low, r_g, q_lat, kv_lat, iq, ik, iw, g_gla, g_dsa = parts
    pad = jnp.zeros((w.shape[0], SM_WP - SM_MISC - IDX_DIM - GLA_GATE_RANK - IDX_HEADS), w.dtype)
    return jnp.concatenate([q_lat, q_g, k_g, v_g, r_g, g_gla, g_dsa,
                            iq, kv_lat, ik, a_low, iw, pad], axis=1).astype(BF16)


def kernel(x, c, w_mod, b_mod, norm1_g, w_in, w_gate_up, b_gate, gla_norm_g, kv_norm_g,
           w_uv, w_out, norm2_g, w_ff1, w_ff2, rel_bias, final_g):
    batch, seq, d = x.shape
    depth = w_mod.shape[0]
    m = batch * seq
    tm = min(1024, seq)

    mods = _modulation(c, w_mod, b_mod).reshape(depth, 8, N_MOD, d)
    bias_tiles = _bias_tiles(rel_bias)
    xf = x.reshape(m, d)
    fg = final_g.reshape(1, d)

    for l in range(depth):
        mod = mods[l]
        z_big, z_small = _norm_matmul(xf, mod, norm1_g[l].reshape(1, d), _split_w_in(w_in[l]), seq, tm, INPROJ_TN)
        y_gla = _gla(z_big, z_small, w_gate_up[l], b_gate[l], gla_norm_g[l], batch, seq)
        y_dsa = _dsa(z_big, z_small, kv_norm_g[l], w_uv[l].astype(BF16), bias_tiles, batch, seq)
        xf = _merge_outproj(z_big, y_gla, y_dsa, w_out, l, xf, mod, seq, min(256, seq))
        xf = _ffn(xf, mod, norm2_g, w_ff1, w_ff2, l, fg, seq, tm, 512, final=(l == depth - 1))
    return xf.reshape(batch, seq, d)
```

```python
import functools
import math

import numpy as np
import jax
import jax.numpy as jnp
from jax import lax
from jax.experimental import pallas as pl
from jax.experimental.pallas import tpu as pltpu

F32 = jnp.float32
BF16 = jnp.bfloat16

D_MODEL = 2048
GLA_HEADS = 4
GLA_HEAD_K = 256
GLA_HEAD_V = 512
GLA_GATE_RANK = 16
GLA_GATE_NORMALIZER = 16.0
DSA_HEADS = 16
DSA_HEAD_DIM = 128
DSA_LATENT = 256
DSA_TOPK = 256
IDX_HEADS = 16
IDX_DIM = 64
REL_BUCKETS = 32
REL_MAX_DIST = 128
D_FF = 4 * D_MODEL
N_MOD = 6
EPS = 1e-6

GLA_QK_W = GLA_HEADS * GLA_HEAD_K
GLA_V_W = GLA_HEADS * GLA_HEAD_V
DSA_Q_W = DSA_HEADS * DSA_LATENT
IN_SIZES = (GLA_QK_W, GLA_QK_W, GLA_V_W, GLA_GATE_RANK, GLA_V_W,
            DSA_Q_W, DSA_LATENT, IDX_HEADS * IDX_DIM, IDX_DIM, IDX_HEADS,
            D_MODEL, D_MODEL)

BIG_QLAT = 0
BIG_QG = BIG_QLAT + DSA_Q_W
BIG_KG = BIG_QG + GLA_QK_W
BIG_VG = BIG_KG + GLA_QK_W
BIG_RG = BIG_VG + GLA_V_W
BIG_GGLA = BIG_RG + GLA_V_W
BIG_GDSA = BIG_GGLA + D_MODEL
BIG_W = BIG_GDSA + D_MODEL
SM_IQ = 0
SM_KV = SM_IQ + IDX_HEADS * IDX_DIM
SM_MISC = SM_KV + DSA_LATENT
MISC_IK = 0
MISC_ALOW = MISC_IK + IDX_DIM
MISC_IW = MISC_ALOW + GLA_GATE_RANK
LANE = 128
SM_W = SM_MISC + LANE
INPROJ_TN = 512
SM_WP = -(-SM_W // INPROJ_TN) * INPROJ_TN

MOD_SH1, MOD_SC1, MOD_GT1, MOD_SH2, MOD_SC2, MOD_GT2 = range(6)

GLA_CHUNK = 256
GLA_SUB = 8
LOG2E = 1.4426950408889634
GLA_BLOCK = 256
DSA_TQ = 256
DSA_HB = 8
DSA_PAD = 16
INT_MIN = -2 ** 31
NEG_INF_KEY = 0x807FFFFF - 2 ** 32
NEG_BIG = -1e30

VMEM_LIMIT = 56 * 1024 * 1024


def _cparams(sem):
    return pltpu.CompilerParams(dimension_semantics=sem, vmem_limit_bytes=VMEM_LIMIT)


def _rms_mod(x, g, sc, sh):
    ms = jnp.mean(x * x, axis=-1, keepdims=True)
    return (x * lax.rsqrt(ms + EPS) * g) * (1.0 + sc) + sh


def _mod_kernel(ct_ref, w_ref, b_ref, o_ref, *, batch):
    ct = ct_ref[...]
    ct = ct * jax.nn.sigmoid(ct)
    d, tn = w_ref.shape[1], w_ref.shape[2]
    o_ref[...] = jnp.zeros_like(o_ref)
    for b in range(batch):
        cb = jnp.broadcast_to(ct[:, b:b + 1], (d, LANE))
        for t in range(tn // LANE):
            seg = jnp.sum(w_ref[0, :, t * LANE:(t + 1) * LANE] * cb, axis=0, keepdims=True)
            o_ref[0, b:b + 1, t * LANE:(t + 1) * LANE] = seg + b_ref[0, :, t * LANE:(t + 1) * LANE]


def _modulation(c, w_mod, b_mod):
    depth, d, n = w_mod.shape
    batch = c.shape[0]
    tn = 1024
    ct = jnp.zeros((d, LANE), F32).at[:, :batch].set(c.T)
    return pl.pallas_call(
        functools.partial(_mod_kernel, batch=batch),
        grid=(depth, n // tn),
        in_specs=[pl.BlockSpec((d, LANE), lambda l, j: (0, 0)),
                  pl.BlockSpec((1, d, tn), lambda l, j: (l, 0, j)),
                  pl.BlockSpec((1, 1, tn), lambda l, j: (l, 0, j))],
        out_specs=pl.BlockSpec((1, 8, tn), lambda l, j: (l, 0, j)),
        out_shape=jax.ShapeDtypeStruct((depth, 8, n), F32),
        compiler_params=_cparams(("arbitrary", "arbitrary")),
        name="adaln_mod",
    )(ct, w_mod, b_mod.reshape(depth, 1, n))


def _norm_mm_kernel(x_ref, mod_ref, g_ref, w_ref, big_ref, small_ref, h_ref, *, n_big):
    j = pl.program_id(1)

    @pl.when(j == 0)
    def _():
        h = _rms_mod(x_ref[...], g_ref[...], mod_ref[0, MOD_SC1:MOD_SC1 + 1, :],
                     mod_ref[0, MOD_SH1:MOD_SH1 + 1, :])
        h_ref[...] = h.astype(BF16)

    y = jnp.dot(h_ref[...], w_ref[...], preferred_element_type=F32)

    @pl.when(j < n_big)
    def _():
        big_ref[...] = y.astype(big_ref.dtype)

    @pl.when(j >= n_big)
    def _():
        small_ref[...] = y


def _norm_matmul(x, mod, g, w, seq, tm, tn):
    m, d = x.shape
    n_big, n_small = BIG_W // tn, SM_WP // tn
    tpb = seq // tm
    return pl.pallas_call(
        functools.partial(_norm_mm_kernel, n_big=n_big),
        grid=(m // tm, n_big + n_small),
        in_specs=[pl.BlockSpec((tm, d), lambda i, j: (i, 0)),
                  pl.BlockSpec((1, N_MOD, d), lambda i, j: (i // tpb, 0, 0)),
                  pl.BlockSpec((1, d), lambda i, j: (0, 0)),
                  pl.BlockSpec((d, tn), lambda i, j: (0, j))],
        out_specs=[pl.BlockSpec((tm, tn), lambda i, j: (i, jnp.minimum(j, n_big - 1))),
                   pl.BlockSpec((tm, tn), lambda i, j: (i, jnp.maximum(j - n_big, 0)))],
        out_shape=[jax.ShapeDtypeStruct((m, BIG_W), BF16), jax.ShapeDtypeStruct((m, SM_WP), F32)],
        scratch_shapes=[pltpu.VMEM((tm, d), BF16)],
        compiler_params=_cparams(("arbitrary", "arbitrary")),
        name="norm_inproj",
    )(x, mod, g, w)


def _gla_kernel(q_ref, k_ref, v_ref, r_ref, misc_ref, wa_ref, ba_ref, gn_ref, o_ref,
                s_ref, a_ref, *, block, chunk):
    dk, dv, sub = GLA_HEAD_K, GLA_HEAD_V, GLA_SUB
    nsub = chunk // sub

    @pl.when(pl.program_id(2) == 0)
    def _():
        s_ref[...] = jnp.zeros_like(s_ref)

    a_ref[...] = jnp.zeros_like(a_ref)

    a_low = misc_ref[:, MISC_ALOW:MISC_ALOW + GLA_GATE_RANK].astype(BF16)
    xg = jnp.dot(a_low, wa_ref[...].astype(BF16), preferred_element_type=F32) + ba_ref[...]
    g_all = (jnp.minimum(xg, 0.0) - jnp.log1p(jnp.exp(-jnp.abs(xg)))) * (LOG2E / GLA_GATE_NORMALIZER)

    r_i = lax.broadcasted_iota(jnp.int32, (chunk, chunk), 0)
    c_i = lax.broadcasted_iota(jnp.int32, (chunk, chunk), 1)
    tril = (r_i >= c_i).astype(BF16)
    row_in_chunk = lax.broadcasted_iota(jnp.int32, (chunk, dk), 0)
    lane3 = lax.broadcasted_iota(jnp.int32, (nsub, sub, sub), 2)
    row3 = lax.broadcasted_iota(jnp.int32, (nsub, sub, sub), 1)
    gn = gn_ref[...]

    for c in range(block // chunk):
        lo = c * chunk
        qf = q_ref[lo:lo + chunk, :].astype(F32) * (dk ** -0.5)
        kf = k_ref[lo:lo + chunk, :].astype(F32)
        vb = v_ref[lo:lo + chunk, :]
        g = g_all[lo:lo + chunk, :]

        g_hi = g.astype(BF16)
        r1 = g - g_hi.astype(F32)
        g_mid = r1.astype(BF16)
        g_lo = (r1 - g_mid.astype(F32)).astype(BF16)
        b = (jnp.dot(tril, g_hi, preferred_element_type=F32)
             + jnp.dot(tril, g_mid, preferred_element_type=F32)
             + jnp.dot(tril, g_lo, preferred_element_type=F32))

        state = s_ref[...]
        o = jnp.dot((qf * jnp.exp2(b)).astype(BF16), state.astype(BF16), preferred_element_type=F32)

        m = chunk // 2
        while m >= sub:
            nseg = chunk // (2 * m)
            pieces = []
            for s in range(nseg):
                ref_row = b[s * 2 * m + m - 1:s * 2 * m + m, :]
                pieces.append(jnp.broadcast_to(ref_row, (2 * m, dk)))
            ref_full = pieces[0] if nseg == 1 else jnp.concatenate(pieces, axis=0)
            fac = jnp.exp2(-jnp.abs(b - ref_full))
            is_q = (row_in_chunk & m) != 0
            xs = jnp.where(is_q, qf, kf) * fac
            for s in range(nseg):
                base = s * 2 * m
                blk = lax.dot_general(xs[base + m:base + 2 * m, :].astype(BF16), xs[base:base + m, :].astype(BF16),
                                      (((1,), (1,)), ((), ())), preferred_element_type=F32)
                a_ref[base + m:base + 2 * m, base:base + m] = blk
            m //= 2

        q4 = qf.reshape(nsub, sub, dk)
        k4 = kf.reshape(nsub, sub, dk)
        b4 = b.reshape(nsub, sub, dk)
        ad = jnp.zeros((nsub, sub, sub), F32)
        for j in range(sub):
            e = jnp.exp2(b4 - b4[:, j:j + 1, :])
            col = jnp.sum(q4 * e * k4[:, j:j + 1, :], axis=-1, keepdims=True)
            ad = jnp.where(lane3 == j, col, ad)
        ad = jnp.where(row3 >= lane3, ad, 0.0)
        for i in range(nsub):
            a_ref[i * sub:(i + 1) * sub, i * sub:(i + 1) * sub] = ad[i]

        o = o + jnp.dot(a_ref[...].astype(BF16), vb, preferred_element_type=F32)

        b_last = b[chunk - 1:chunk, :]
        kd = (kf * jnp.exp2(b_last - b)).astype(BF16)
        upd = lax.dot_general(kd, vb, (((0,), (0,)), ((), ())), preferred_element_type=F32)
        decay_col = jnp.transpose(jnp.broadcast_to(jnp.exp2(b_last), (LANE, dk)))[:, 0:1]
        s_ref[...] = decay_col * state + upd

        o = o * lax.rsqrt(jnp.mean(o * o, axis=-1, keepdims=True) + EPS) * gn
        r = r_ref[lo:lo + chunk, :].astype(F32)
        o_ref[lo:lo + chunk, :] = (o * (r * jax.nn.sigmoid(r))).astype(o_ref.dtype)


def _gla(z_big, z_small, w_gate_up, b_gate, gla_norm_g, batch, seq):
    blk = min(GLA_BLOCK, seq)
    nb = seq // blk
    dk, dv = GLA_HEAD_K, GLA_HEAD_V
    row = lambda b, h, t: b * nb + t
    kern = functools.partial(_gla_kernel, block=blk, chunk=GLA_CHUNK)
    return pl.pallas_call(
        kern,
        grid=(batch, GLA_HEADS, nb),
        in_specs=[pl.BlockSpec((blk, dk), lambda b, h, t: (row(b, h, t), BIG_QG // dk + h)),
                  pl.BlockSpec((blk, dk), lambda b, h, t: (row(b, h, t), BIG_KG // dk + h)),
                  pl.BlockSpec((blk, dv), lambda b, h, t: (row(b, h, t), BIG_VG // dv + h)),
                  pl.BlockSpec((blk, dv), lambda b, h, t: (row(b, h, t), BIG_RG // dv + h)),
                  pl.BlockSpec((blk, LANE), lambda b, h, t: (row(b, h, t), SM_MISC // LANE)),
                  pl.BlockSpec((GLA_GATE_RANK, dk), lambda b, h, t: (0, h)),
                  pl.BlockSpec((1, dk), lambda b, h, t: (0, h)),
                  pl.BlockSpec((1, dv), lambda b, h, t: (0, 0))],
        out_specs=pl.BlockSpec((blk, dv), lambda b, h, t: (row(b, h, t), h)),
        out_shape=jax.ShapeDtypeStruct((batch * seq, GLA_V_W), BF16),
        scratch_shapes=[pltpu.VMEM((dk, dv), F32), pltpu.VMEM((GLA_CHUNK, GLA_CHUNK), F32)],
        compiler_params=_cparams(("arbitrary", "arbitrary", "arbitrary")),
        name="gla",
    )(z_big, z_big, z_big, z_big, z_small, w_gate_up, b_gate.reshape(1, -1), gla_norm_g.reshape(1, -1))


def _rel_bucket_np(d):
    max_exact = REL_BUCKETS // 2
    d = np.maximum(d, 0)
    df = np.maximum(d, 1).astype(np.float32)
    large = max_exact + (np.log(df / np.float32(max_exact)) / np.float32(math.log(REL_MAX_DIST / max_exact))
                         * np.float32(REL_BUCKETS - max_exact)).astype(np.int32)
    large = np.minimum(large, REL_BUCKETS - 1)
    return np.where(d < max_exact, d, large).astype(np.int32)


def _bias_kernel(rb_ref, bucket_ref, o_ref):
    h = pl.program_id(0)
    far = rb_ref[REL_BUCKETS - 1, h]
    for u in range(2):
        bk = bucket_ref[u]
        acc = jnp.zeros(bk.shape, F32)
        for bb in range(REL_BUCKETS):
            acc = jnp.where(bk == bb, rb_ref[bb, h], acc)
        o_ref[0, u] = acc - far


def _bias_tiles(rel_bias):
    j = np.arange(LANE)[:, None]
    i = np.arange(LANE)[None, :]
    bucket = np.stack([_rel_bucket_np(i - j), _rel_bucket_np(LANE + i - j)]).astype(np.int32)
    assert int(_rel_bucket_np(np.array([LANE]))[0]) == REL_BUCKETS - 1
    return pl.pallas_call(
        _bias_kernel,
        grid=(DSA_HEADS,),
        in_specs=[pl.BlockSpec(memory_space=pltpu.SMEM),
                  pl.BlockSpec((2, LANE, LANE), lambda h: (0, 0, 0))],
        out_specs=pl.BlockSpec((1, 2, LANE, LANE), lambda h: (h, 0, 0, 0)),
        out_shape=jax.ShapeDtypeStruct((DSA_HEADS, 2, LANE, LANE), F32),
        compiler_params=_cparams(("arbitrary",)),
        name="t5_bias_tiles",
    )(rel_bias, jnp.asarray(bucket))


def _key_to_float(key):
    return pltpu.bitcast(key ^ ((key >> 31) & 0x7FFFFFFF), F32)


def _dsa_kernel(q_ref, iq_ref, kv_ref, misc_all_ref, misc_q_ref, kvg_ref, wuv_ref, bias_ref, o_ref,
                kvn_ref, kvt_ref, ikb_ref, iqh_ref, sc_ref, qs_ref, lga_ref, lgb_ref, acc_ref, *, seq, k_sel):
    tq, hb, lat = DSA_TQ, DSA_HB, DSA_LATENT
    qi = pl.program_id(1)
    hg = pl.program_id(2)
    key_i = lax.broadcasted_iota(jnp.int32, (tq, tq), 0)
    qry_i = lax.broadcasted_iota(jnp.int32, (tq, tq), 1)
    causal = key_i <= qry_i
    nt = (((1,), (1,)), ((), ()))

    @pl.when((qi == 0) & (hg == 0))
    def _():
        kv = kv_ref[...]
        ms = jnp.mean(kv * kv, axis=-1, keepdims=True)
        kvn = kv * lax.rsqrt(ms + EPS) * kvg_ref[...]
        kvn_ref[...] = kvn.astype(BF16)
        ones_rows = (lax.broadcasted_iota(jnp.int32, (DSA_PAD, tq), 0) == 0).astype(BF16)
        for t in range(seq // tq):
            kvt_ref[t, 0:lat, :] = jnp.transpose(kvn[t * tq:(t + 1) * tq, :]).astype(BF16)
            kvt_ref[t, lat:lat + DSA_PAD, :] = ones_rows
        ikb_ref[...] = misc_all_ref[:, MISC_IK:MISC_IK + IDX_DIM].astype(BF16)

    @pl.when(hg == 0)
    def _():
        for h in range(IDX_HEADS):
            iqh_ref[h] = (iq_ref[:, h * IDX_DIM:(h + 1) * IDX_DIM] * (IDX_DIM ** -0.5)).astype(BF16)
        iw_t = jnp.transpose(misc_q_ref[...]) * (IDX_HEADS ** -0.5)

        def score_tile(kt):
            off = pl.multiple_of(kt * tq, tq)
            ik_t = ikb_ref[pl.ds(off, tq), :]
            sc = jnp.zeros((tq, tq), F32)
            for h in range(IDX_HEADS):
                y = lax.dot_general(ik_t, iqh_ref[h], nt, preferred_element_type=F32)
                sc = sc + jnp.maximum(y, 0.0) * iw_t[MISC_IW + h:MISC_IW + h + 1, :]
            return sc

        def far_body(kt, carry):
            sc_ref[kt] = score_tile(kt)
            return carry

        lax.fori_loop(0, qi, far_body, 0)
        sc_ref[qi] = jnp.where(causal, score_tile(qi), -jnp.inf)

        def bit_body(it, key):
            cand_key = key + (jnp.int32(1) << (31 - it))
            cand = _key_to_float(cand_key)

            def cnt_body(kt, cnt):
                hit = jnp.where(sc_ref[kt] >= cand, 1, 0)
                return cnt + jnp.sum(hit.reshape(tq // 8, 8, tq), axis=0)

            cnt = lax.fori_loop(0, qi + 1, cnt_body, jnp.zeros((8, tq), jnp.int32))
            tot = jnp.sum(cnt, axis=0, keepdims=True)
            return jnp.where((tot >= k_sel) | (cand_key <= NEG_INF_KEY), cand_key, key)

        key = lax.fori_loop(0, 32, bit_body, jnp.full((1, tq), INT_MIN, jnp.int32))
        thr = _key_to_float(key)

        def mask_body(kt, carry):
            sc_ref[kt] = jnp.where(sc_ref[kt] >= thr, 0.0, NEG_BIG)
            return carry

        lax.fori_loop(0, qi, mask_body, 0)
        sc_ref[qi] = jnp.where(causal & (sc_ref[qi] >= thr), 0.0, NEG_BIG)

    zero_blk = jnp.zeros((LANE, LANE), F32)

    for hh in range(hb):
        qs_ref[hh] = q_ref[:, hh * lat:(hh + 1) * lat] * (lat ** -0.5)
    acc_ref[...] = jnp.zeros_like(acc_ref)

    def logits(kt, buf):
        off = pl.multiple_of(kt * tq, tq)
        kv_t = kvn_ref[pl.ds(off, tq), :]
        for hh in range(hb):
            buf[hh] = lax.dot_general(kv_t, qs_ref[hh], nt, preferred_element_type=F32)

    def tile(kt, carry, buf, near=None):
        kv_tt = kvt_ref[kt]
        mask = sc_ref[kt]
        out = []
        for hh in range(hb):
            m_run = carry[hh]
            s = buf[hh] + mask
            if near == "prev":
                s = s + jnp.concatenate([jnp.concatenate([zero_blk, zero_blk], axis=1),
                                         jnp.concatenate([bias_ref[hh, 1], zero_blk], axis=1)], axis=0)
            elif near == "diag":
                s = s + jnp.concatenate([jnp.concatenate([bias_ref[hh, 0], bias_ref[hh, 1]], axis=1),
                                         jnp.concatenate([zero_blk, bias_ref[hh, 0]], axis=1)], axis=0)
            m_new = jnp.maximum(m_run, jnp.max(s, axis=0, keepdims=True))
            alpha = jnp.exp(m_run - m_new)
            p = jnp.exp(s - m_new)
            acc_ref[hh] = alpha * acc_ref[hh] + jnp.dot(kv_tt, p.astype(BF16), preferred_element_type=F32)
            out.append(m_new)
        return tuple(out)

    carry = tuple(jnp.full((1, tq), NEG_BIG, F32) for _ in range(hb))
    n_far = jnp.maximum(qi - 1, 0)

    def diag_and_prev(cr):
        logits(qi - 1, lgb_ref)
        cr = tile(qi, cr, lga_ref, near="diag")
        logits(jnp.maximum(qi - 2, 0), lga_ref)
        return tile(qi - 1, cr, lgb_ref, near="prev")

    def far_pair(j, cr):
        kt_a = qi - 2 - 2 * j
        kt_b = kt_a - 1
        logits(kt_b, lgb_ref)
        cr = tile(kt_a, cr, lga_ref)
        logits(jnp.maximum(kt_b - 1, 0), lga_ref)
        return tile(kt_b, cr, lgb_ref)

    logits(qi, lga_ref)
    carry = lax.cond(qi > 0, diag_and_prev, lambda cr: tile(qi, cr, lga_ref, near="diag"), carry)
    carry = lax.fori_loop(0, n_far // 2, far_pair, carry)
    carry = lax.cond(n_far % 2 == 1, lambda cr: tile(jnp.int32(0), cr, lga_ref), lambda cr: cr, carry)

    for hh in range(hb):
        o_lat_t = (acc_ref[hh, 0:lat, :] / acc_ref[hh, lat:lat + 1, :]).astype(BF16)
        y_t = lax.dot_general(wuv_ref[hh], o_lat_t, (((0,), (0,)), ((), ())), preferred_element_type=F32)
        o_ref[:, hh * DSA_HEAD_DIM:(hh + 1) * DSA_HEAD_DIM] = jnp.transpose(y_t).astype(o_ref.dtype)


def _dsa(z_big, z_small, kv_norm_g, w_uv_bf16, bias_tiles, batch, seq):
    tq, hb, lat = DSA_TQ, DSA_HB, DSA_LATENT
    nq = seq // tq
    k_sel = min(DSA_TOPK, seq // 4)
    kern = functools.partial(_dsa_kernel, seq=seq, k_sel=k_sel)
    return pl.pallas_call(
        kern,
        grid=(batch, nq, DSA_HEADS // hb),
        in_specs=[pl.BlockSpec((tq, hb * lat), lambda b, i, g: (b * nq + i, BIG_QLAT // (hb * lat) + g)),
                  pl.BlockSpec((tq, IDX_HEADS * IDX_DIM), lambda b, i, g: (b * nq + i, 0)),
                  pl.BlockSpec((seq, lat), lambda b, i, g: (b, SM_KV // lat)),
                  pl.BlockSpec((seq, LANE), lambda b, i, g: (b, SM_MISC // LANE)),
                  pl.BlockSpec((tq, LANE), lambda b, i, g: (b * nq + i, SM_MISC // LANE)),
                  pl.BlockSpec((1, lat), lambda b, i, g: (0, 0)),
                  pl.BlockSpec((hb, lat, DSA_HEAD_DIM), lambda b, i, g: (g, 0, 0)),
                  pl.BlockSpec((hb, 2, LANE, LANE), lambda b, i, g: (g, 0, 0, 0))],
        out_specs=pl.BlockSpec((tq, hb * DSA_HEAD_DIM), lambda b, i, g: (b * nq + i, g)),
        out_shape=jax.ShapeDtypeStruct((batch * seq, DSA_HEADS * DSA_HEAD_DIM), BF16),
        scratch_shapes=[pltpu.VMEM((seq, lat), BF16),
                        pltpu.VMEM((nq, lat + DSA_PAD, tq), BF16),
                        pltpu.VMEM((seq, IDX_DIM), BF16),
                        pltpu.VMEM((IDX_HEADS, tq, IDX_DIM), BF16),
                        pltpu.VMEM((nq, tq, tq), F32),
                        pltpu.VMEM((hb, tq, lat), BF16),
                        pltpu.VMEM((hb, tq, tq), F32),
                        pltpu.VMEM((hb, tq, tq), F32),
                        pltpu.VMEM((hb, lat + DSA_PAD, tq), F32)],
        compiler_params=_cparams(("arbitrary", "arbitrary", "arbitrary")),
        name="dsa",
    )(z_big, z_small, z_small, z_small, z_small, kv_norm_g.reshape(1, -1), w_uv_bf16, bias_tiles)


def _merge_mm_kernel(gg_ref, gd_ref, yg_ref, yd_ref, w_ref, x_ref, mod_ref, o_ref, wb_ref):
    @pl.when(pl.program_id(0) == 0)
    def _():
        wb_ref[...] = w_ref[0].astype(BF16)

    merged = (jax.nn.sigmoid(gg_ref[...].astype(F32)) * yg_ref[...].astype(F32)
              + jax.nn.sigmoid(gd_ref[...].astype(F32)) * yd_ref[...].astype(F32))
    y = jnp.dot(merged.astype(BF16), wb_ref[...], preferred_element_type=F32)
    o_ref[...] = x_ref[...] + mod_ref[0, MOD_GT1:MOD_GT1 + 1, :] * y


def _merge_outproj(z_big, y_gla, y_dsa, w_out, layer, x, mod, seq, tm):
    m, d = x.shape
    tpb = seq // tm
    return pl.pallas_call(
        _merge_mm_kernel,
        grid=(m // tm,),
        in_specs=[pl.BlockSpec((tm, d), lambda i: (i, BIG_GGLA // d)),
                  pl.BlockSpec((tm, d), lambda i: (i, BIG_GDSA // d)),
                  pl.BlockSpec((tm, d), lambda i: (i, 0)),
                  pl.BlockSpec((tm, d), lambda i: (i, 0)),
                  pl.BlockSpec((1, d, d), lambda i: (layer, 0, 0), pipeline_mode=pl.Buffered(1)),
                  pl.BlockSpec((tm, d), lambda i: (i, 0)),
                  pl.BlockSpec((1, N_MOD, d), lambda i: (i // tpb, 0, 0))],
        out_specs=pl.BlockSpec((tm, d), lambda i: (i, 0)),
        out_shape=jax.ShapeDtypeStruct((m, d), F32),
        scratch_shapes=[pltpu.VMEM((d, d), BF16)],
        compiler_params=_cparams(("arbitrary",)),
        name="merge_outproj",
    )(z_big, z_big, y_gla, y_dsa, w_out, x, mod)


def _ffn_kernel(x_ref, mod_ref, g_ref, w1_ref, w2_ref, fg_ref, o_ref, h_ref, *, final):
    j = pl.program_id(1)

    @pl.when(j == 0)
    def _():
        h = _rms_mod(x_ref[...], g_ref[0], mod_ref[0, MOD_SC2:MOD_SC2 + 1, :],
                     mod_ref[0, MOD_SH2:MOD_SH2 + 1, :])
        h_ref[...] = h.astype(BF16)
        o_ref[...] = jnp.zeros_like(o_ref)

    u = jnp.dot(h_ref[...], w1_ref[0].astype(BF16), preferred_element_type=F32)
    u = jnp.square(jnp.maximum(u, 0.0)).astype(BF16)
    tf = u.shape[1]
    for n0 in range(0, o_ref.shape[1], tf):
        o_ref[:, n0:n0 + tf] += jnp.dot(u, w2_ref[0, :, n0:n0 + tf].astype(BF16), preferred_element_type=F32)

    @pl.when(j == pl.num_programs(1) - 1)
    def _():
        out = x_ref[...] + mod_ref[0, MOD_GT2:MOD_GT2 + 1, :] * o_ref[...]
        if final:
            out = out * lax.rsqrt(jnp.mean(out * out, axis=-1, keepdims=True) + EPS) * fg_ref[...]
        o_ref[...] = out


def _ffn(x, mod, norm2_g, w1, w2, layer, final_g, seq, tm, tf, final):
    m, d = x.shape
    f = w1.shape[2]
    tpb = seq // tm
    return pl.pallas_call(
        functools.partial(_ffn_kernel, final=final),
        grid=(m // tm, f // tf),
        in_specs=[pl.BlockSpec((tm, d), lambda i, j: (i, 0), pipeline_mode=pl.Buffered(1)),
                  pl.BlockSpec((1, N_MOD, d), lambda i, j: (i // tpb, 0, 0)),
                  pl.BlockSpec((1, 1, d), lambda i, j: (layer, 0, 0)),
                  pl.BlockSpec((1, d, tf), lambda i, j: (layer, 0, j)),
                  pl.BlockSpec((1, tf, d), lambda i, j: (layer, j, 0)),
                  pl.BlockSpec((1, d), lambda i, j: (0, 0))],
        out_specs=pl.BlockSpec((tm, d), lambda i, j: (i, 0)),
        out_shape=jax.ShapeDtypeStruct((m, d), F32),
        scratch_shapes=[pltpu.VMEM((tm, d), BF16)],
        compiler_params=_cparams(("arbitrary", "arbitrary")),
        name="ffn",
    )(x, mod, norm2_g.reshape(norm2_g.shape[0], 1, d), w1, w2, final_g)


def _split_w_in(w):
    w = w.astype(BF16)
    parts, off = [], 0
    for s in IN_SIZES:
        parts.append(w[:, off:off + s])
        off += s
    q_g, k_g, v_g, a_low, r_g, q_lat, kv_lat, iq, ik, iw, g_gla, g_dsa = parts
    pad = jnp.zeros((w.shape[0], SM_WP - SM_MISC - IDX_DIM - GLA_GATE_RANK - IDX_HEADS), w.dtype)
    return jnp.concatenate([q_lat, q_g, k_g, v_g, r_g, g_gla, g_dsa,
                            iq, kv_lat, ik, a_low, iw, pad], axis=1).astype(BF16)


def kernel(x, c, w_mod, b_mod, norm1_g, w_in, w_gate_up, b_gate, gla_norm_g, kv_norm_g,
           w_uv, w_out, norm2_g, w_ff1, w_ff2, rel_bias, final_g):
    batch, seq, d = x.shape
    depth = w_mod.shape[0]
    m = batch * seq
    tm = min(1024, seq)

    mods = _modulation(c, w_mod, b_mod).reshape(depth, 8, N_MOD, d)
    bias_tiles = _bias_tiles(rel_bias)
    xf = x.reshape(m, d)
    fg = final_g.reshape(1, d)

    for l in range(depth):
        mod = mods[l]
        z_big, z_small = _norm_matmul(xf, mod, norm1_g[l].reshape(1, d), _split_w_in(w_in[l]), seq, tm, INPROJ_TN)
        y_gla = _gla(z_big, z_small, w_gate_up[l], b_gate[l], gla_norm_g[l], batch, seq)
        y_dsa = _dsa(z_big, z_small, kv_norm_g[l], w_uv[l].astype(BF16), bias_tiles, batch, seq)
        xf = _merge_outproj(z_big, y_gla, y_dsa, w_out, l, xf, mod, seq, min(256, seq))
        xf = _ffn(xf, mod, norm2_g, w_ff1, w_ff2, l, fg, seq, tm, 512, final=(l == depth - 1))
    return xf.reshape(batch, seq, d)
```

```python
import functools
import math

import numpy as np
import jax
import jax.numpy as jnp
from jax import lax
from jax.experimental import pallas as pl
from jax.experimental.pallas import tpu as pltpu

F32 = jnp.float32
BF16 = jnp.bfloat16

D_MODEL = 2048
GLA_HEADS = 4
GLA_HEAD_K = 256
GLA_HEAD_V = 512
GLA_GATE_RANK = 16
GLA_GATE_NORMALIZER = 16.0
DSA_HEADS = 16
DSA_HEAD_DIM = 128
DSA_LATENT = 256
DSA_TOPK = 256
IDX_HEADS = 16
IDX_DIM = 64
REL_BUCKETS = 32
REL_MAX_DIST = 128
D_FF = 4 * D_MODEL
N_MOD = 6
EPS = 1e-6

GLA_QK_W = GLA_HEADS * GLA_HEAD_K
GLA_V_W = GLA_HEADS * GLA_HEAD_V
DSA_Q_W = DSA_HEADS * DSA_LATENT
IN_SIZES = (GLA_QK_W, GLA_QK_W, GLA_V_W, GLA_GATE_RANK, GLA_V_W,
            DSA_Q_W, DSA_LATENT, IDX_HEADS * IDX_DIM, IDX_DIM, IDX_HEADS,
            D_MODEL, D_MODEL)

BIG_QLAT = 0
BIG_QG = BIG_QLAT + DSA_Q_W
BIG_KG = BIG_QG + GLA_QK_W
BIG_VG = BIG_KG + GLA_QK_W
BIG_RG = BIG_VG + GLA_V_W
BIG_GGLA = BIG_RG + GLA_V_W
BIG_GDSA = BIG_GGLA + D_MODEL
BIG_W = BIG_GDSA + D_MODEL
SM_IQ = 0
SM_KV = SM_IQ + IDX_HEADS * IDX_DIM
SM_MISC = SM_KV + DSA_LATENT
MISC_IK = 0
MISC_ALOW = MISC_IK + IDX_DIM
MISC_IW = MISC_ALOW + GLA_GATE_RANK
LANE = 128
SM_W = SM_MISC + LANE
INPROJ_TN = 512
INPROJ_ROWS = 256
SM_WP = -(-SM_W // INPROJ_TN) * INPROJ_TN

MOD_SH1, MOD_SC1, MOD_GT1, MOD_SH2, MOD_SC2, MOD_GT2 = range(6)

GLA_CHUNK = 256
GLA_SUB = 8
LOG2E = 1.4426950408889634
GLA_BLOCK = 256
DSA_TQ = 256
DSA_HB = 8
DSA_PAD = 16
INT_MIN = -2 ** 31
NEG_INF_KEY = 0x807FFFFF - 2 ** 32
NEG_BIG = -1e30

VMEM_LIMIT = 56 * 1024 * 1024


def _cparams(sem):
    return pltpu.CompilerParams(dimension_semantics=sem, vmem_limit_bytes=VMEM_LIMIT)


def _rms_mod(x, g, sc, sh):
    ms = jnp.mean(x * x, axis=-1, keepdims=True)
    return (x * lax.rsqrt(ms + EPS) * g) * (1.0 + sc) + sh


def _mod_kernel(ct_ref, w_ref, b_ref, o_ref, *, batch):
    ct = ct_ref[...]
    ct = ct * jax.nn.sigmoid(ct)
    d, tn = w_ref.shape[1], w_ref.shape[2]
    o_ref[...] = jnp.zeros_like(o_ref)
    for b in range(batch):
        cb = jnp.broadcast_to(ct[:, b:b + 1], (d, LANE))
        for t in range(tn // LANE):
            seg = jnp.sum(w_ref[0, :, t * LANE:(t + 1) * LANE] * cb, axis=0, keepdims=True)
            o_ref[0, b:b + 1, t * LANE:(t + 1) * LANE] = seg + b_ref[0, :, t * LANE:(t + 1) * LANE]


def _modulation(c, w_mod, b_mod):
    depth, d, n = w_mod.shape
    batch = c.shape[0]
    tn = 1024
    ct = jnp.zeros((d, LANE), F32).at[:, :batch].set(c.T)
    return pl.pallas_call(
        functools.partial(_mod_kernel, batch=batch),
        grid=(depth, n // tn),
        in_specs=[pl.BlockSpec((d, LANE), lambda l, j: (0, 0)),
                  pl.BlockSpec((1, d, tn), lambda l, j: (l, 0, j)),
                  pl.BlockSpec((1, 1, tn), lambda l, j: (l, 0, j))],
        out_specs=pl.BlockSpec((1, 8, tn), lambda l, j: (l, 0, j)),
        out_shape=jax.ShapeDtypeStruct((depth, 8, n), F32),
        compiler_params=_cparams(("arbitrary", "arbitrary")),
        name="adaln_mod",
    )(ct, w_mod, b_mod.reshape(depth, 1, n))


def _norm_mm_kernel(x_ref, mod_ref, g_ref, w_ref, big_ref, small_ref, h_ref, *, n_big):
    j = pl.program_id(1)

    @pl.when(j == 0)
    def _():
        rows = min(512, x_ref.shape[0])
        for r0 in range(0, x_ref.shape[0], rows):
            h = _rms_mod(x_ref[r0:r0 + rows, :], g_ref[...], mod_ref[0, MOD_SC1:MOD_SC1 + 1, :],
                         mod_ref[0, MOD_SH1:MOD_SH1 + 1, :])
            h_ref[r0:r0 + rows, :] = h.astype(BF16)

    rows = min(INPROJ_ROWS, h_ref.shape[0])

    @pl.when(j < n_big)
    def _():
        for r0 in range(0, h_ref.shape[0], rows):
            y = jnp.dot(h_ref[r0:r0 + rows, :], w_ref[0], preferred_element_type=F32)
            big_ref[r0:r0 + rows, :] = y.astype(big_ref.dtype)

    @pl.when(j >= n_big)
    def _():
        for r0 in range(0, h_ref.shape[0], rows):
            small_ref[r0:r0 + rows, :] = jnp.dot(h_ref[r0:r0 + rows, :], w_ref[0], preferred_element_type=F32)


def _norm_matmul(x, mod, g, w, layer, seq, tm, tn):
    m, d = x.shape
    n_big, n_small = BIG_W // tn, SM_WP // tn
    tpb = seq // tm
    return pl.pallas_call(
        functools.partial(_norm_mm_kernel, n_big=n_big),
        grid=(m // tm, n_big + n_small),
        in_specs=[pl.BlockSpec((tm, d), lambda i, j: (i, 0), pipeline_mode=pl.Buffered(1)),
                  pl.BlockSpec((1, N_MOD, d), lambda i, j: (i // tpb, 0, 0)),
                  pl.BlockSpec((1, d), lambda i, j: (0, 0)),
                  pl.BlockSpec((1, d, tn), lambda i, j: (layer, 0, j))],
        out_specs=[pl.BlockSpec((tm, tn), lambda i, j: (i, jnp.minimum(j, n_big - 1))),
                   pl.BlockSpec((tm, tn), lambda i, j: (i, jnp.maximum(j - n_big, 0)))],
        out_shape=[jax.ShapeDtypeStruct((m, BIG_W), BF16), jax.ShapeDtypeStruct((m, SM_WP), F32)],
        scratch_shapes=[pltpu.VMEM((tm, d), BF16)],
        compiler_params=_cparams(("arbitrary", "arbitrary")),
        name="norm_inproj",
    )(x, mod, g, w)


def _gla_kernel(q_ref, k_ref, v_ref, r_ref, misc_ref, wa_ref, ba_ref, gn_ref, o_ref,
                s_ref, a_ref, *, block, chunk):
    dk, dv, sub = GLA_HEAD_K, GLA_HEAD_V, GLA_SUB
    nsub = chunk // sub

    @pl.when(pl.program_id(2) == 0)
    def _():
        s_ref[...] = jnp.zeros_like(s_ref)

    a_ref[...] = jnp.zeros_like(a_ref)

    a_low = misc_ref[:, MISC_ALOW:MISC_ALOW + GLA_GATE_RANK].astype(BF16)
    xg = jnp.dot(a_low, wa_ref[...].astype(BF16), preferred_element_type=F32) + ba_ref[...]
    g_all = (jnp.minimum(xg, 0.0) - jnp.log1p(jnp.exp(-jnp.abs(xg)))) * (LOG2E / GLA_GATE_NORMALIZER)

    r_i = lax.broadcasted_iota(jnp.int32, (chunk, chunk), 0)
    c_i = lax.broadcasted_iota(jnp.int32, (chunk, chunk), 1)
    tril = (r_i >= c_i).astype(BF16)
    row_in_chunk = lax.broadcasted_iota(jnp.int32, (chunk, dk), 0)
    lane3 = lax.broadcasted_iota(jnp.int32, (nsub, sub, sub), 2)
    row3 = lax.broadcasted_iota(jnp.int32, (nsub, sub, sub), 1)
    gn = gn_ref[...]

    for c in range(block // chunk):
        lo = c * chunk
        qf = q_ref[lo:lo + chunk, :].astype(F32) * (dk ** -0.5)
        kf = k_ref[lo:lo + chunk, :].astype(F32)
        vb = v_ref[lo:lo + chunk, :]
        g = g_all[lo:lo + chunk, :]

        g_hi = g.astype(BF16)
        r1 = g - g_hi.astype(F32)
        g_mid = r1.astype(BF16)
        g_lo = (r1 - g_mid.astype(F32)).astype(BF16)
        b = (jnp.dot(tril, g_hi, preferred_element_type=F32)
             + jnp.dot(tril, g_mid, preferred_element_type=F32)
             + jnp.dot(tril, g_lo, preferred_element_type=F32))

        state = s_ref[...]
        o = jnp.dot((qf * jnp.exp2(b)).astype(BF16), state.astype(BF16), preferred_element_type=F32)

        m = chunk // 2
        while m >= sub:
            nseg = chunk // (2 * m)
            pieces = []
            for s in range(nseg):
                ref_row = b[s * 2 * m + m - 1:s * 2 * m + m, :]
                pieces.append(jnp.broadcast_to(ref_row, (2 * m, dk)))
            ref_full = pieces[0] if nseg == 1 else jnp.concatenate(pieces, axis=0)
            fac = jnp.exp2(-jnp.abs(b - ref_full))
            is_q = (row_in_chunk & m) != 0
            xs = jnp.where(is_q, qf, kf) * fac
            for s in range(nseg):
                base = s * 2 * m
                blk = lax.dot_general(xs[base + m:base + 2 * m, :].astype(BF16), xs[base:base + m, :].astype(BF16),
                                      (((1,), (1,)), ((), ())), preferred_element_type=F32)
                a_ref[base + m:base + 2 * m, base:base + m] = blk
            m //= 2

        q4 = qf.reshape(nsub, sub, dk)
        k4 = kf.reshape(nsub, sub, dk)
        b4 = b.reshape(nsub, sub, dk)
        ad = jnp.zeros((nsub, sub, sub), F32)
        for j in range(sub):
            e = jnp.exp2(b4 - b4[:, j:j + 1, :])
            col = jnp.sum(q4 * e * k4[:, j:j + 1, :], axis=-1, keepdims=True)
            ad = jnp.where(lane3 == j, col, ad)
        ad = jnp.where(row3 >= lane3, ad, 0.0)
        for i in range(nsub):
            a_ref[i * sub:(i + 1) * sub, i * sub:(i + 1) * sub] = ad[i]

        o = o + jnp.dot(a_ref[...].astype(BF16), vb, preferred_element_type=F32)

        b_last = b[chunk - 1:chunk, :]
        kd = (kf * jnp.exp2(b_last - b)).astype(BF16)
        upd = lax.dot_general(kd, vb, (((0,), (0,)), ((), ())), preferred_element_type=F32)
        decay_col = jnp.transpose(jnp.broadcast_to(jnp.exp2(b_last), (LANE, dk)))[:, 0:1]
        s_ref[...] = decay_col * state + upd

        o = o * lax.rsqrt(jnp.mean(o * o, axis=-1, keepdims=True) + EPS) * gn
        r = r_ref[lo:lo + chunk, :].astype(F32)
        o_ref[lo:lo + chunk, :] = (o * (r * jax.nn.sigmoid(r))).astype(o_ref.dtype)


def _gla(z_big, z_small, w_gate_up, b_gate, gla_norm_g, batch, seq):
    blk = min(GLA_BLOCK, seq)
    nb = seq // blk
    dk, dv = GLA_HEAD_K, GLA_HEAD_V
    row = lambda b, h, t: b * nb + t
    kern = functools.partial(_gla_kernel, block=blk, chunk=GLA_CHUNK)
    return pl.pallas_call(
        kern,
        grid=(batch, GLA_HEADS, nb),
        in_specs=[pl.BlockSpec((blk, dk), lambda b, h, t: (row(b, h, t), BIG_QG // dk + h)),
                  pl.BlockSpec((blk, dk), lambda b, h, t: (row(b, h, t), BIG_KG // dk + h)),
                  pl.BlockSpec((blk, dv), lambda b, h, t: (row(b, h, t), BIG_VG // dv + h)),
                  pl.BlockSpec((blk, dv), lambda b, h, t: (row(b, h, t), BIG_RG // dv + h)),
                  pl.BlockSpec((blk, LANE), lambda b, h, t: (row(b, h, t), SM_MISC // LANE)),
                  pl.BlockSpec((GLA_GATE_RANK, dk), lambda b, h, t: (0, h)),
                  pl.BlockSpec((1, dk), lambda b, h, t: (0, h)),
                  pl.BlockSpec((1, dv), lambda b, h, t: (0, 0))],
        out_specs=pl.BlockSpec((blk, dv), lambda b, h, t: (row(b, h, t), h)),
        out_shape=jax.ShapeDtypeStruct((batch * seq, GLA_V_W), BF16),
        scratch_shapes=[pltpu.VMEM((dk, dv), F32), pltpu.VMEM((GLA_CHUNK, GLA_CHUNK), F32)],
        compiler_params=_cparams(("arbitrary", "arbitrary", "arbitrary")),
        name="gla",
    )(z_big, z_big, z_big, z_big, z_small, w_gate_up, b_gate.reshape(1, -1), gla_norm_g.reshape(1, -1))


def _rel_bucket_np(d):
    max_exact = REL_BUCKETS // 2
    d = np.maximum(d, 0)
    df = np.maximum(d, 1).astype(np.float32)
    large = max_exact + (np.log(df / np.float32(max_exact)) / np.float32(math.log(REL_MAX_DIST / max_exact))
                         * np.float32(REL_BUCKETS - max_exact)).astype(np.int32)
    large = np.minimum(large, REL_BUCKETS - 1)
    return np.where(d < max_exact, d, large).astype(np.int32)


def _bias_kernel(rb_ref, bucket_ref, o_ref):
    h = pl.program_id(0)
    far = rb_ref[REL_BUCKETS - 1, h]
    for u in range(2):
        bk = bucket_ref[u]
        acc = jnp.zeros(bk.shape, F32)
        for bb in range(REL_BUCKETS):
            acc = jnp.where(bk == bb, rb_ref[bb, h], acc)
        o_ref[0, u] = acc - far


def _bias_tiles(rel_bias):
    j = np.arange(LANE)[:, None]
    i = np.arange(LANE)[None, :]
    bucket = np.stack([_rel_bucket_np(i - j), _rel_bucket_np(LANE + i - j)]).astype(np.int32)
    assert int(_rel_bucket_np(np.array([LANE]))[0]) == REL_BUCKETS - 1
    return pl.pallas_call(
        _bias_kernel,
        grid=(DSA_HEADS,),
        in_specs=[pl.BlockSpec(memory_space=pltpu.SMEM),
                  pl.BlockSpec((2, LANE, LANE), lambda h: (0, 0, 0))],
        out_specs=pl.BlockSpec((1, 2, LANE, LANE), lambda h: (h, 0, 0, 0)),
        out_shape=jax.ShapeDtypeStruct((DSA_HEADS, 2, LANE, LANE), F32),
        compiler_params=_cparams(("arbitrary",)),
        name="t5_bias_tiles",
    )(rel_bias, jnp.asarray(bucket))


def _key_to_float(key):
    return pltpu.bitcast(key ^ ((key >> 31) & 0x7FFFFFFF), F32)


def _dsa_kernel(q_ref, iq_ref, kv_ref, misc_all_ref, misc_q_ref, kvg_ref, wuv_ref, bias_ref, o_ref,
                kvn_ref, kvt_ref, ikb_ref, iqh_ref, sc_ref, qs_ref, lga_ref, lgb_ref, acc_ref, *, seq, k_sel):
    tq, hb, lat = DSA_TQ, DSA_HB, DSA_LATENT
    qi = pl.program_id(1)
    hg = pl.program_id(2)
    key_i = lax.broadcasted_iota(jnp.int32, (tq, tq), 0)
    qry_i = lax.broadcasted_iota(jnp.int32, (tq, tq), 1)
    causal = key_i <= qry_i
    nt = (((1,), (1,)), ((), ()))

    @pl.when((qi == 0) & (hg == 0))
    def _():
        kv = kv_ref[...]
        ms = jnp.mean(kv * kv, axis=-1, keepdims=True)
        kvn = kv * lax.rsqrt(ms + EPS) * kvg_ref[...]
        kvn_ref[...] = kvn.astype(BF16)
        ones_rows = (lax.broadcasted_iota(jnp.int32, (DSA_PAD, tq), 0) == 0).astype(BF16)
        for t in range(seq // tq):
            kvt_ref[t, 0:lat, :] = jnp.transpose(kvn[t * tq:(t + 1) * tq, :]).astype(BF16)
            kvt_ref[t, lat:lat + DSA_PAD, :] = ones_rows
        ikb_ref[...] = misc_all_ref[:, MISC_IK:MISC_IK + IDX_DIM].astype(BF16)

    @pl.when(hg == 0)
    def _():
        for h in range(IDX_HEADS):
            iqh_ref[h] = (iq_ref[:, h * IDX_DIM:(h + 1) * IDX_DIM] * (IDX_DIM ** -0.5)).astype(BF16)
        iw_t = jnp.transpose(misc_q_ref[...]) * (IDX_HEADS ** -0.5)

        def score_tile(kt):
            off = pl.multiple_of(kt * tq, tq)
            ik_t = ikb_ref[pl.ds(off, tq), :]
            sc = jnp.zeros((tq, tq), F32)
            for h in range(IDX_HEADS):
                y = lax.dot_general(ik_t, iqh_ref[h], nt, preferred_element_type=F32)
                sc = sc + jnp.maximum(y, 0.0) * iw_t[MISC_IW + h:MISC_IW + h + 1, :]
            return sc

        def far_body(kt, carry):
            sc_ref[kt] = score_tile(kt)
            return carry

        lax.fori_loop(0, qi, far_body, 0)
        sc_ref[qi] = jnp.where(causal, score_tile(qi), -jnp.inf)

        def bit_body(it, key):
            cand_key = key + (jnp.int32(1) << (31 - it))
            cand = _key_to_float(cand_key)

            def cnt_body(kt, cnt):
                hit = jnp.where(sc_ref[kt] >= cand, 1, 0)
                return cnt + jnp.sum(hit.reshape(tq // 8, 8, tq), axis=0)

            cnt = lax.fori_loop(0, qi + 1, cnt_body, jnp.zeros((8, tq), jnp.int32))
            tot = jnp.sum(cnt, axis=0, keepdims=True)
            return jnp.where((tot >= k_sel) | (cand_key <= NEG_INF_KEY), cand_key, key)

        key = lax.fori_loop(0, 32, bit_body, jnp.full((1, tq), INT_MIN, jnp.int32))
        thr = _key_to_float(key)

        def mask_body(kt, carry):
            sc_ref[kt] = jnp.where(sc_ref[kt] >= thr, 0.0, NEG_BIG)
            return carry

        lax.fori_loop(0, qi, mask_body, 0)
        sc_ref[qi] = jnp.where(causal & (sc_ref[qi] >= thr), 0.0, NEG_BIG)

    zero_blk = jnp.zeros((LANE, LANE), F32)

    for hh in range(hb):
        qs_ref[hh] = q_ref[:, hh * lat:(hh + 1) * lat] * (lat ** -0.5)
    acc_ref[...] = jnp.zeros_like(acc_ref)

    def logits(kt, buf):
        off = pl.multiple_of(kt * tq, tq)
        kv_t = kvn_ref[pl.ds(off, tq), :]
        for hh in range(hb):
            buf[hh] = lax.dot_general(kv_t, qs_ref[hh], nt, preferred_element_type=F32)

    def tile(kt, carry, buf, near=None):
        kv_tt = kvt_ref[kt]
        mask = sc_ref[kt]
        out = []
        for hh in range(hb):
            m_run = carry[hh]
            s = buf[hh] + mask
            if near == "prev":
                s = s + jnp.concatenate([jnp.concatenate([zero_blk, zero_blk], axis=1),
                                         jnp.concatenate([bias_ref[hh, 1], zero_blk], axis=1)], axis=0)
            elif near == "diag":
                s = s + jnp.concatenate([jnp.concatenate([bias_ref[hh, 0], bias_ref[hh, 1]], axis=1),
                                         jnp.concatenate([zero_blk, bias_ref[hh, 0]], axis=1)], axis=0)
            m_new = jnp.maximum(m_run, jnp.max(s, axis=0, keepdims=True))
            alpha = jnp.exp(m_run - m_new)
            p = jnp.exp(s - m_new)
            acc_ref[hh] = alpha * acc_ref[hh] + jnp.dot(kv_tt, p.astype(BF16), preferred_element_type=F32)
            out.append(m_new)
        return tuple(out)

    carry = tuple(jnp.full((1, tq), NEG_BIG, F32) for _ in range(hb))
    n_far = jnp.maximum(qi - 1, 0)

    def diag_and_prev(cr):
        logits(qi - 1, lgb_ref)
        cr = tile(qi, cr, lga_ref, near="diag")
        logits(jnp.maximum(qi - 2, 0), lga_ref)
        return tile(qi - 1, cr, lgb_ref, near="prev")

    def far_pair(j, cr):
        kt_a = qi - 2 - 2 * j
        kt_b = kt_a - 1
        logits(kt_b, lgb_ref)
        cr = tile(kt_a, cr, lga_ref)
        logits(jnp.maximum(kt_b - 1, 0), lga_ref)
        return tile(kt_b, cr, lgb_ref)

    logits(qi, lga_ref)
    carry = lax.cond(qi > 0, diag_and_prev, lambda cr: tile(qi, cr, lga_ref, near="diag"), carry)
    carry = lax.fori_loop(0, n_far // 2, far_pair, carry)
    carry = lax.cond(n_far % 2 == 1, lambda cr: tile(jnp.int32(0), cr, lga_ref), lambda cr: cr, carry)

    for hh in range(hb):
        o_lat_t = (acc_ref[hh, 0:lat, :] / acc_ref[hh, lat:lat + 1, :]).astype(BF16)
        y_t = lax.dot_general(wuv_ref[hh], o_lat_t, (((0,), (0,)), ((), ())), preferred_element_type=F32)
        o_ref[:, hh * DSA_HEAD_DIM:(hh + 1) * DSA_HEAD_DIM] = jnp.transpose(y_t).astype(o_ref.dtype)


def _dsa(z_big, z_small, kv_norm_g, w_uv_bf16, bias_tiles, batch, seq):
    tq, hb, lat = DSA_TQ, DSA_HB, DSA_LATENT
    nq = seq // tq
    k_sel = min(DSA_TOPK, seq // 4)
    kern = functools.partial(_dsa_kernel, seq=seq, k_sel=k_sel)
    return pl.pallas_call(
        kern,
        grid=(batch, nq, DSA_HEADS // hb),
        in_specs=[pl.BlockSpec((tq, hb * lat), lambda b, i, g: (b * nq + i, BIG_QLAT // (hb * lat) + g)),
                  pl.BlockSpec((tq, IDX_HEADS * IDX_DIM), lambda b, i, g: (b * nq + i, 0)),
                  pl.BlockSpec((seq, lat), lambda b, i, g: (b, SM_KV // lat)),
                  pl.BlockSpec((seq, LANE), lambda b, i, g: (b, SM_MISC // LANE)),
                  pl.BlockSpec((tq, LANE), lambda b, i, g: (b * nq + i, SM_MISC // LANE)),
                  pl.BlockSpec((1, lat), lambda b, i, g: (0, 0)),
                  pl.BlockSpec((hb, lat, DSA_HEAD_DIM), lambda b, i, g: (g, 0, 0)),
                  pl.BlockSpec((hb, 2, LANE, LANE), lambda b, i, g: (g, 0, 0, 0))],
        out_specs=pl.BlockSpec((tq, hb * DSA_HEAD_DIM), lambda b, i, g: (b * nq + i, g)),
        out_shape=jax.ShapeDtypeStruct((batch * seq, DSA_HEADS * DSA_HEAD_DIM), BF16),
        scratch_shapes=[pltpu.VMEM((seq, lat), BF16),
                        pltpu.VMEM((nq, lat + DSA_PAD, tq), BF16),
                        pltpu.VMEM((seq, IDX_DIM), BF16),
                        pltpu.VMEM((IDX_HEADS, tq, IDX_DIM), BF16),
                        pltpu.VMEM((nq, tq, tq), F32),
                        pltpu.VMEM((hb, tq, lat), BF16),
                        pltpu.VMEM((hb, tq, tq), F32),
                        pltpu.VMEM((hb, tq, tq), F32),
                        pltpu.VMEM((hb, lat + DSA_PAD, tq), F32)],
        compiler_params=_cparams(("arbitrary", "arbitrary", "arbitrary")),
        name="dsa",
    )(z_big, z_small, z_small, z_small, z_small, kv_norm_g.reshape(1, -1), w_uv_bf16, bias_tiles)


def _merge_mm_kernel(gg_ref, gd_ref, yg_ref, yd_ref, w_ref, x_ref, mod_ref, o_ref, wb_ref):
    @pl.when(pl.program_id(0) == 0)
    def _():
        wb_ref[...] = w_ref[0].astype(BF16)

    merged = (jax.nn.sigmoid(gg_ref[...].astype(F32)) * yg_ref[...].astype(F32)
              + jax.nn.sigmoid(gd_ref[...].astype(F32)) * yd_ref[...].astype(F32))
    y = jnp.dot(merged.astype(BF16), wb_ref[...], preferred_element_type=F32)
    o_ref[...] = x_ref[...] + mod_ref[0, MOD_GT1:MOD_GT1 + 1, :] * y


def _merge_outproj(z_big, y_gla, y_dsa, w_out, layer, x, mod, seq, tm):
    m, d = x.shape
    tpb = seq // tm
    return pl.pallas_call(
        _merge_mm_kernel,
        grid=(m // tm,),
        in_specs=[pl.BlockSpec((tm, d), lambda i: (i, BIG_GGLA // d)),
                  pl.BlockSpec((tm, d), lambda i: (i, BIG_GDSA // d)),
                  pl.BlockSpec((tm, d), lambda i: (i, 0)),
                  pl.BlockSpec((tm, d), lambda i: (i, 0)),
                  pl.BlockSpec((1, d, d), lambda i: (layer, 0, 0), pipeline_mode=pl.Buffered(1)),
                  pl.BlockSpec((tm, d), lambda i: (i, 0)),
                  pl.BlockSpec((1, N_MOD, d), lambda i: (i // tpb, 0, 0))],
        out_specs=pl.BlockSpec((tm, d), lambda i: (i, 0)),
        out_shape=jax.ShapeDtypeStruct((m, d), F32),
        scratch_shapes=[pltpu.VMEM((d, d), BF16)],
        compiler_params=_cparams(("arbitrary",)),
        name="merge_outproj",
    )(z_big, z_big, y_gla, y_dsa, w_out, x, mod)


def _ffn_kernel(x_ref, mod_ref, g_ref, w1_ref, w2_ref, fg_ref, o_ref, h_ref, *, final):
    j = pl.program_id(1)

    @pl.when(j == 0)
    def _():
        h = _rms_mod(x_ref[...], g_ref[0], mod_ref[0, MOD_SC2:MOD_SC2 + 1, :],
                     mod_ref[0, MOD_SH2:MOD_SH2 + 1, :])
        h_ref[...] = h.astype(BF16)
        o_ref[...] = jnp.zeros_like(o_ref)

    u = jnp.dot(h_ref[...], w1_ref[0].astype(BF16), preferred_element_type=F32)
    u = jnp.square(jnp.maximum(u, 0.0)).astype(BF16)
    tf = u.shape[1]
    for n0 in range(0, o_ref.shape[1], tf):
        o_ref[:, n0:n0 + tf] += jnp.dot(u, w2_ref[0, :, n0:n0 + tf].astype(BF16), preferred_element_type=F32)

    @pl.when(j == pl.num_programs(1) - 1)
    def _():
        out = x_ref[...] + mod_ref[0, MOD_GT2:MOD_GT2 + 1, :] * o_ref[...]
        if final:
            out = out * lax.rsqrt(jnp.mean(out * out, axis=-1, keepdims=True) + EPS) * fg_ref[...]
        o_ref[...] = out


def _ffn(x, mod, norm2_g, w1, w2, layer, final_g, seq, tm, tf, final):
    m, d = x.shape
    f = w1.shape[2]
    tpb = seq // tm
    return pl.pallas_call(
        functools.partial(_ffn_kernel, final=final),
        grid=(m // tm, f // tf),
        in_specs=[pl.BlockSpec((tm, d), lambda i, j: (i, 0), pipeline_mode=pl.Buffered(1)),
                  pl.BlockSpec((1, N_MOD, d), lambda i, j: (i // tpb, 0, 0)),
                  pl.BlockSpec((1, 1, d), lambda i, j: (layer, 0, 0)),
                  pl.BlockSpec((1, d, tf), lambda i, j: (layer, 0, j)),
                  pl.BlockSpec((1, tf, d), lambda i, j: (layer, j, 0)),
                  pl.BlockSpec((1, d), lambda i, j: (0, 0))],
        out_specs=pl.BlockSpec((tm, d), lambda i, j: (i, 0)),
        out_shape=jax.ShapeDtypeStruct((m, d), F32),
        scratch_shapes=[pltpu.VMEM((tm, d), BF16)],
        compiler_params=_cparams(("arbitrary", "arbitrary")),
        name="ffn",
    )(x, mod, norm2_g.reshape(norm2_g.shape[0], 1, d), w1, w2, final_g)


def _relayout_tables():
    src = dict(zip(("q_g", "k_g", "v_g", "a_low", "r_g", "q_lat", "kv_lat", "iq", "ik", "iw", "g_gla", "g_dsa"),
                   np.concatenate([[0], np.cumsum(IN_SIZES)[:-1]])))
    order = (("q_lat", DSA_Q_W), ("q_g", GLA_QK_W), ("k_g", GLA_QK_W), ("v_g", GLA_V_W), ("r_g", GLA_V_W),
             ("g_gla", D_MODEL), ("g_dsa", D_MODEL), ("iq", IDX_HEADS * IDX_DIM), ("kv_lat", INPROJ_TN))
    starts = [src[name] + c for name, width in order for c in range(0, width, INPROJ_TN)]
    assert len(starts) * INPROJ_TN == BIG_W + SM_WP
    tiles = np.array([s // LANE for s in starts], np.int32)
    deltas = np.array([s % LANE for s in starts], np.int32)
    assert src["ik"] % LANE + IDX_DIM == src["iw"] % LANE == MISC_IW and src["ik"] // LANE == src["iw"] // LANE
    assert src["a_low"] % LANE == 0
    return tiles, deltas, int(src["ik"] // LANE), int(src["ik"] % LANE), int(src["a_low"] // LANE)


def _relayout_kernel(tile_ref, delta_ref, w0_ref, w1_ref, w2_ref, w3_ref, w4_ref, ik_ref, al_ref, o_ref, *, ik_lane):
    del tile_ref
    j = pl.program_id(1)
    delta = delta_ref[j]
    d = o_ref.shape[1]
    lane = lax.broadcasted_iota(jnp.int32, (d, LANE), 1)
    win = (w0_ref[0], w1_ref[0], w2_ref[0], w3_ref[0], w4_ref[0])
    for t in range(INPROJ_TN // LANE):
        y = jnp.where(lane >= delta, win[t], win[t + 1])
        o_ref[0, :, t * LANE:(t + 1) * LANE] = pltpu.roll(y, (LANE - delta) % LANE, 1).astype(o_ref.dtype)

    @pl.when(j == pl.num_programs(1) - 1)
    def _():
        ikw = ik_ref[0]
        misc = jnp.where(lane < MISC_ALOW, pltpu.roll(ikw, LANE - ik_lane, 1),
                         jnp.where(lane < MISC_IW, pltpu.roll(al_ref[0], MISC_ALOW, 1),
                                   jnp.where(lane < MISC_IW + IDX_HEADS, ikw, 0.0)))
        o_ref[0, :, DSA_LATENT:DSA_LATENT + LANE] = misc.astype(o_ref.dtype)
        o_ref[0, :, DSA_LATENT + LANE:] = jnp.zeros((d, INPROJ_TN - DSA_LATENT - LANE), o_ref.dtype)


def _relayout_w_in(w_in):
    depth, d, _ = w_in.shape
    tiles, deltas, ik_tile, ik_lane, al_tile = _relayout_tables()
    win_specs = [pl.BlockSpec((1, d, LANE), lambda l, j, tt, dt, t=t: (l, 0, tt[j] + t)) for t in range(5)]
    grid_spec = pltpu.PrefetchScalarGridSpec(
        num_scalar_prefetch=2,
        grid=(depth, len(tiles)),
        in_specs=win_specs + [pl.BlockSpec((1, d, LANE), lambda l, j, tt, dt: (l, 0, ik_tile)),
                              pl.BlockSpec((1, d, LANE), lambda l, j, tt, dt: (l, 0, al_tile))],
        out_specs=pl.BlockSpec((1, d, INPROJ_TN), lambda l, j, tt, dt: (l, 0, j)))
    return pl.pallas_call(
        functools.partial(_relayout_kernel, ik_lane=ik_lane),
        grid_spec=grid_spec,
        out_shape=jax.ShapeDtypeStruct((depth, d, BIG_W + SM_WP), BF16),
        compiler_params=_cparams(("arbitrary", "arbitrary")),
        name="w_in_relayout",
    )(jnp.asarray(tiles), jnp.asarray(deltas), *([w_in] * 7))


def kernel(x, c, w_mod, b_mod, norm1_g, w_in, w_gate_up, b_gate, gla_norm_g, kv_norm_g,
           w_uv, w_out, norm2_g, w_ff1, w_ff2, rel_bias, final_g):
    batch, seq, d = x.shape
    depth = w_mod.shape[0]
    m = batch * seq
    tm = min(1024, seq)

    mods = _modulation(c, w_mod, b_mod).reshape(depth, 8, N_MOD, d)
    bias_tiles = _bias_tiles(rel_bias)
    w_in_all = _relayout_w_in(w_in)
    xf = x.reshape(m, d)
    fg = final_g.reshape(1, d)

    for l in range(depth):
        mod = mods[l]
        z_big, z_small = _norm_matmul(xf, mod, norm1_g[l].reshape(1, d), w_in_all, l, seq, min(2048, seq), INPROJ_TN)
        y_gla = _gla(z_big, z_small, w_gate_up[l], b_gate[l], gla_norm_g[l], batch, seq)
        y_dsa = _dsa(z_big, z_small, kv_norm_g[l], w_uv[l].astype(BF16), bias_tiles, batch, seq)
        xf = _merge_outproj(z_big, y_gla, y_dsa, w_out, l, xf, mod, seq, min(256, seq))
        xf = _ffn(xf, mod, norm2_g, w_ff1, w_ff2, l, fg, seq, tm, 512, final=(l == depth - 1))
    return xf.reshape(batch, seq, d)
```

```python
import functools
import math

import numpy as np
import jax
import jax.numpy as jnp
from jax import lax
from jax.experimental import pallas as pl
from jax.experimental.pallas import tpu as pltpu

F32 = jnp.float32
BF16 = jnp.bfloat16

D_MODEL = 2048
GLA_HEADS = 4
GLA_HEAD_K = 256
GLA_HEAD_V = 512
GLA_GATE_RANK = 16
GLA_GATE_NORMALIZER = 16.0
DSA_HEADS = 16
DSA_HEAD_DIM = 128
DSA_LATENT = 256
DSA_TOPK = 256
IDX_HEADS = 16
IDX_DIM = 64
REL_BUCKETS = 32
REL_MAX_DIST = 128
D_FF = 4 * D_MODEL
N_MOD = 6
EPS = 1e-6

GLA_QK_W = GLA_HEADS * GLA_HEAD_K
GLA_V_W = GLA_HEADS * GLA_HEAD_V
DSA_Q_W = DSA_HEADS * DSA_LATENT
IN_SIZES = (GLA_QK_W, GLA_QK_W, GLA_V_W, GLA_GATE_RANK, GLA_V_W,
            DSA_Q_W, DSA_LATENT, IDX_HEADS * IDX_DIM, IDX_DIM, IDX_HEADS,
            D_MODEL, D_MODEL)

BIG_QLAT = 0
BIG_QG = BIG_QLAT + DSA_Q_W
BIG_KG = BIG_QG + GLA_QK_W
BIG_VG = BIG_KG + GLA_QK_W
BIG_RG = BIG_VG + GLA_V_W
BIG_GGLA = BIG_RG + GLA_V_W
BIG_GDSA = BIG_GGLA + D_MODEL
BIG_W = BIG_GDSA + D_MODEL
SM_IQ = 0
SM_KV = SM_IQ + IDX_HEADS * IDX_DIM
SM_MISC = SM_KV + DSA_LATENT
MISC_IK = 0
MISC_ALOW = MISC_IK + IDX_DIM
MISC_IW = MISC_ALOW + GLA_GATE_RANK
LANE = 128
SM_W = SM_MISC + LANE
INPROJ_TN = 512
INPROJ_ROWS = 256
SM_WP = -(-SM_W // INPROJ_TN) * INPROJ_TN

MOD_SH1, MOD_SC1, MOD_GT1, MOD_SH2, MOD_SC2, MOD_GT2 = range(6)

GLA_CHUNK = 256
GLA_SUB = 8
LOG2E = 1.4426950408889634
GLA_BLOCK = 256
DSA_TQ = 256
DSA_HB = 8
DSA_PAD = 16
INT_MIN = -2 ** 31
NEG_INF_KEY = 0x807FFFFF - 2 ** 32
NEG_BIG = -1e30

VMEM_LIMIT = 56 * 1024 * 1024


def _cparams(sem):
    return pltpu.CompilerParams(dimension_semantics=sem, vmem_limit_bytes=VMEM_LIMIT)


def _rms_mod(x, g, sc, sh):
    ms = jnp.mean(x * x, axis=-1, keepdims=True)
    return (x * lax.rsqrt(ms + EPS) * g) * (1.0 + sc) + sh


def _mod_kernel(ct_ref, w_ref, b_ref, o_ref, *, batch):
    ct = ct_ref[...]
    ct = ct * jax.nn.sigmoid(ct)
    d, tn = w_ref.shape[1], w_ref.shape[2]
    o_ref[...] = jnp.zeros_like(o_ref)
    for b in range(batch):
        cb = jnp.broadcast_to(ct[:, b:b + 1], (d, LANE))
        for t in range(tn // LANE):
            seg = jnp.sum(w_ref[0, :, t * LANE:(t + 1) * LANE] * cb, axis=0, keepdims=True)
            o_ref[0, b:b + 1, t * LANE:(t + 1) * LANE] = seg + b_ref[0, :, t * LANE:(t + 1) * LANE]


def _modulation(c, w_mod, b_mod):
    depth, d, n = w_mod.shape
    batch = c.shape[0]
    tn = 1024
    ct = jnp.zeros((d, LANE), F32).at[:, :batch].set(c.T)
    return pl.pallas_call(
        functools.partial(_mod_kernel, batch=batch),
        grid=(depth, n // tn),
        in_specs=[pl.BlockSpec((d, LANE), lambda l, j: (0, 0)),
                  pl.BlockSpec((1, d, tn), lambda l, j: (l, 0, j)),
                  pl.BlockSpec((1, 1, tn), lambda l, j: (l, 0, j))],
        out_specs=pl.BlockSpec((1, 8, tn), lambda l, j: (l, 0, j)),
        out_shape=jax.ShapeDtypeStruct((depth, 8, n), F32),
        compiler_params=_cparams(("arbitrary", "arbitrary")),
        name="adaln_mod",
    )(ct, w_mod, b_mod.reshape(depth, 1, n))


def _norm_mm_kernel(x_ref, mod_ref, g_ref, w_ref, big_ref, small_ref, h_ref, *, n_big):
    j = pl.program_id(1)

    @pl.when(j == 0)
    def _():
        rows = min(512, x_ref.shape[0])
        for r0 in range(0, x_ref.shape[0], rows):
            h = _rms_mod(x_ref[r0:r0 + rows, :], g_ref[...], mod_ref[0, MOD_SC1:MOD_SC1 + 1, :],
                         mod_ref[0, MOD_SH1:MOD_SH1 + 1, :])
            h_ref[r0:r0 + rows, :] = h.astype(BF16)

    rows = min(INPROJ_ROWS, h_ref.shape[0])

    @pl.when(j < n_big)
    def _():
        for r0 in range(0, h_ref.shape[0], rows):
            y = jnp.dot(h_ref[r0:r0 + rows, :], w_ref[0], preferred_element_type=F32)
            big_ref[r0:r0 + rows, :] = y.astype(big_ref.dtype)

    @pl.when(j >= n_big)
    def _():
        for r0 in range(0, h_ref.shape[0], rows):
            small_ref[r0:r0 + rows, :] = jnp.dot(h_ref[r0:r0 + rows, :], w_ref[0], preferred_element_type=F32)


def _norm_matmul(x, mod, g, w, layer, seq, tm, tn):
    m, d = x.shape
    n_big, n_small = BIG_W // tn, SM_WP // tn
    tpb = seq // tm
    return pl.pallas_call(
        functools.partial(_norm_mm_kernel, n_big=n_big),
        grid=(m // tm, n_big + n_small),
        in_specs=[pl.BlockSpec((tm, d), lambda i, j: (i, 0), pipeline_mode=pl.Buffered(1)),
                  pl.BlockSpec((1, N_MOD, d), lambda i, j: (i // tpb, 0, 0)),
                  pl.BlockSpec((1, d), lambda i, j: (0, 0)),
                  pl.BlockSpec((1, d, tn), lambda i, j: (layer, 0, j))],
        out_specs=[pl.BlockSpec((tm, tn), lambda i, j: (i, jnp.minimum(j, n_big - 1))),
                   pl.BlockSpec((tm, tn), lambda i, j: (i, jnp.maximum(j - n_big, 0)))],
        out_shape=[jax.ShapeDtypeStruct((m, BIG_W), BF16), jax.ShapeDtypeStruct((m, SM_WP), F32)],
        scratch_shapes=[pltpu.VMEM((tm, d), BF16)],
        compiler_params=_cparams(("arbitrary", "arbitrary")),
        name="norm_inproj",
    )(x, mod, g, w)


def _gla_kernel(q_ref, k_ref, v_ref, r_ref, misc_ref, wa_ref, ba_ref, gn_ref, o_ref,
                s_ref, a_ref, *, block, chunk):
    dk, dv, sub = GLA_HEAD_K, GLA_HEAD_V, GLA_SUB
    nsub = chunk // sub

    @pl.when(pl.program_id(2) == 0)
    def _():
        s_ref[...] = jnp.zeros_like(s_ref)

    a_ref[...] = jnp.zeros_like(a_ref)

    a_low = misc_ref[:, MISC_ALOW:MISC_ALOW + GLA_GATE_RANK].astype(BF16)
    xg = jnp.dot(a_low, wa_ref[...].astype(BF16), preferred_element_type=F32) + ba_ref[...]
    g_all = (jnp.minimum(xg, 0.0) - jnp.log1p(jnp.exp(-jnp.abs(xg)))) * (LOG2E / GLA_GATE_NORMALIZER)

    r_i = lax.broadcasted_iota(jnp.int32, (chunk, chunk), 0)
    c_i = lax.broadcasted_iota(jnp.int32, (chunk, chunk), 1)
    tril = (r_i >= c_i).astype(BF16)
    row_in_chunk = lax.broadcasted_iota(jnp.int32, (chunk, dk), 0)
    lane3 = lax.broadcasted_iota(jnp.int32, (nsub, sub, sub), 2)
    row3 = lax.broadcasted_iota(jnp.int32, (nsub, sub, sub), 1)
    gn = gn_ref[...]

    for c in range(block // chunk):
        lo = c * chunk
        qf = q_ref[lo:lo + chunk, :].astype(F32) * (dk ** -0.5)
        kf = k_ref[lo:lo + chunk, :].astype(F32)
        vb = v_ref[lo:lo + chunk, :]
        g = g_all[lo:lo + chunk, :]

        g_hi = g.astype(BF16)
        r1 = g - g_hi.astype(F32)
        g_mid = r1.astype(BF16)
        g_lo = (r1 - g_mid.astype(F32)).astype(BF16)
        b = (jnp.dot(tril, g_hi, preferred_element_type=F32)
             + jnp.dot(tril, g_mid, preferred_element_type=F32)
             + jnp.dot(tril, g_lo, preferred_element_type=F32))

        state = s_ref[...]
        o = jnp.dot((qf * jnp.exp2(b)).astype(BF16), state.astype(BF16), preferred_element_type=F32)

        m = chunk // 2
        while m >= sub:
            nseg = chunk // (2 * m)
            pieces = []
            for s in range(nseg):
                ref_row = b[s * 2 * m + m - 1:s * 2 * m + m, :]
                pieces.append(jnp.broadcast_to(ref_row, (2 * m, dk)))
            ref_full = pieces[0] if nseg == 1 else jnp.concatenate(pieces, axis=0)
            fac = jnp.exp2(-jnp.abs(b - ref_full))
            is_q = (row_in_chunk & m) != 0
            xs = jnp.where(is_q, qf, kf) * fac
            for s in range(nseg):
                base = s * 2 * m
                blk = lax.dot_general(xs[base + m:base + 2 * m, :].astype(BF16), xs[base:base + m, :].astype(BF16),
                                      (((1,), (1,)), ((), ())), preferred_element_type=F32)
                a_ref[base + m:base + 2 * m, base:base + m] = blk
            m //= 2

        q4 = qf.reshape(nsub, sub, dk)
        k4 = kf.reshape(nsub, sub, dk)
        b4 = b.reshape(nsub, sub, dk)
        ad = jnp.zeros((nsub, sub, sub), F32)
        for j in range(sub):
            e = jnp.exp2(b4 - b4[:, j:j + 1, :])
            col = jnp.sum(q4 * e * k4[:, j:j + 1, :], axis=-1, keepdims=True)
            ad = jnp.where(lane3 == j, col, ad)
        ad = jnp.where(row3 >= lane3, ad, 0.0)
        for i in range(nsub):
            a_ref[i * sub:(i + 1) * sub, i * sub:(i + 1) * sub] = ad[i]

        o = o + jnp.dot(a_ref[...].astype(BF16), vb, preferred_element_type=F32)

        b_last = b[chunk - 1:chunk, :]
        kd = (kf * jnp.exp2(b_last - b)).astype(BF16)
        upd = lax.dot_general(kd, vb, (((0,), (0,)), ((), ())), preferred_element_type=F32)
        decay_col = jnp.transpose(jnp.broadcast_to(jnp.exp2(b_last), (LANE, dk)))[:, 0:1]
        s_ref[...] = decay_col * state + upd

        o = o * lax.rsqrt(jnp.mean(o * o, axis=-1, keepdims=True) + EPS) * gn
        r = r_ref[lo:lo + chunk, :].astype(F32)
        o_ref[lo:lo + chunk, :] = (o * (r * jax.nn.sigmoid(r))).astype(o_ref.dtype)


def _gla(z_big, z_small, w_gate_up, b_gate, gla_norm_g, batch, seq):
    blk = min(GLA_BLOCK, seq)
    nb = seq // blk
    dk, dv = GLA_HEAD_K, GLA_HEAD_V
    row = lambda b, h, t: b * nb + t
    kern = functools.partial(_gla_kernel, block=blk, chunk=GLA_CHUNK)
    return pl.pallas_call(
        kern,
        grid=(batch, GLA_HEADS, nb),
        in_specs=[pl.BlockSpec((blk, dk), lambda b, h, t: (row(b, h, t), BIG_QG // dk + h)),
                  pl.BlockSpec((blk, dk), lambda b, h, t: (row(b, h, t), BIG_KG // dk + h)),
                  pl.BlockSpec((blk, dv), lambda b, h, t: (row(b, h, t), BIG_VG // dv + h)),
                  pl.BlockSpec((blk, dv), lambda b, h, t: (row(b, h, t), BIG_RG // dv + h)),
                  pl.BlockSpec((blk, LANE), lambda b, h, t: (row(b, h, t), SM_MISC // LANE)),
                  pl.BlockSpec((GLA_GATE_RANK, dk), lambda b, h, t: (0, h)),
                  pl.BlockSpec((1, dk), lambda b, h, t: (0, h)),
                  pl.BlockSpec((1, dv), lambda b, h, t: (0, 0))],
        out_specs=pl.BlockSpec((blk, dv), lambda b, h, t: (row(b, h, t), h)),
        out_shape=jax.ShapeDtypeStruct((batch * seq, GLA_V_W), BF16),
        scratch_shapes=[pltpu.VMEM((dk, dv), F32), pltpu.VMEM((GLA_CHUNK, GLA_CHUNK), F32)],
        compiler_params=_cparams(("arbitrary", "arbitrary", "arbitrary")),
        name="gla",
    )(z_big, z_big, z_big, z_big, z_small, w_gate_up, b_gate.reshape(1, -1), gla_norm_g.reshape(1, -1))


def _rel_bucket_np(d):
    max_exact = REL_BUCKETS // 2
    d = np.maximum(d, 0)
    df = np.maximum(d, 1).astype(np.float32)
    large = max_exact + (np.log(df / np.float32(max_exact)) / np.float32(math.log(REL_MAX_DIST / max_exact))
                         * np.float32(REL_BUCKETS - max_exact)).astype(np.int32)
    large = np.minimum(large, REL_BUCKETS - 1)
    return np.where(d < max_exact, d, large).astype(np.int32)


def _bias_kernel(rb_ref, bucket_ref, o_ref):
    h = pl.program_id(0)
    far = rb_ref[REL_BUCKETS - 1, h]
    for u in range(2):
        bk = bucket_ref[u]
        acc = jnp.zeros(bk.shape, F32)
        for bb in range(REL_BUCKETS):
            acc = jnp.where(bk == bb, rb_ref[bb, h], acc)
        o_ref[0, u] = acc - far


def _bias_tiles(rel_bias):
    j = np.arange(LANE)[:, None]
    i = np.arange(LANE)[None, :]
    bucket = np.stack([_rel_bucket_np(i - j), _rel_bucket_np(LANE + i - j)]).astype(np.int32)
    assert int(_rel_bucket_np(np.array([LANE]))[0]) == REL_BUCKETS - 1
    return pl.pallas_call(
        _bias_kernel,
        grid=(DSA_HEADS,),
        in_specs=[pl.BlockSpec(memory_space=pltpu.SMEM),
                  pl.BlockSpec((2, LANE, LANE), lambda h: (0, 0, 0))],
        out_specs=pl.BlockSpec((1, 2, LANE, LANE), lambda h: (h, 0, 0, 0)),
        out_shape=jax.ShapeDtypeStruct((DSA_HEADS, 2, LANE, LANE), F32),
        compiler_params=_cparams(("arbitrary",)),
        name="t5_bias_tiles",
    )(rel_bias, jnp.asarray(bucket))


def _key_to_float(key):
    return pltpu.bitcast(key ^ ((key >> 31) & 0x7FFFFFFF), F32)


def _dsa_kernel(q_ref, iq_ref, kv_ref, misc_all_ref, misc_q_ref, kvg_ref, wuv_ref, bias_ref, o_ref,
                kvn_ref, kvt_ref, ikb_ref, iqh_ref, sc_ref, qs_ref, lga_ref, lgb_ref, acc_ref, *, seq, k_sel):
    tq, hb, lat = DSA_TQ, DSA_HB, DSA_LATENT
    qi = pl.program_id(1)
    hg = pl.program_id(2)
    key_i = lax.broadcasted_iota(jnp.int32, (tq, tq), 0)
    qry_i = lax.broadcasted_iota(jnp.int32, (tq, tq), 1)
    causal = key_i <= qry_i
    nt = (((1,), (1,)), ((), ()))

    @pl.when((qi == 0) & (hg == 0))
    def _():
        kv = kv_ref[...]
        ms = jnp.mean(kv * kv, axis=-1, keepdims=True)
        kvn = kv * lax.rsqrt(ms + EPS) * kvg_ref[...]
        kvn_ref[...] = kvn.astype(BF16)
        ones_rows = (lax.broadcasted_iota(jnp.int32, (DSA_PAD, tq), 0) == 0).astype(BF16)
        for t in range(seq // tq):
            kvt_ref[t, 0:lat, :] = jnp.transpose(kvn[t * tq:(t + 1) * tq, :]).astype(BF16)
            kvt_ref[t, lat:lat + DSA_PAD, :] = ones_rows
        ikb_ref[...] = misc_all_ref[:, MISC_IK:MISC_IK + IDX_DIM].astype(BF16)

    @pl.when(hg == 0)
    def _():
        for h in range(IDX_HEADS):
            iqh_ref[h] = (iq_ref[:, h * IDX_DIM:(h + 1) * IDX_DIM] * (IDX_DIM ** -0.5)).astype(BF16)
        iw_t = jnp.transpose(misc_q_ref[...]) * (IDX_HEADS ** -0.5)

        def score_tile(kt):
            off = pl.multiple_of(kt * tq, tq)
            ik_t = ikb_ref[pl.ds(off, tq), :]
            sc = jnp.zeros((tq, tq), F32)
            for h in range(IDX_HEADS):
                y = lax.dot_general(ik_t, iqh_ref[h], nt, preferred_element_type=F32)
                sc = sc + jnp.maximum(y, 0.0) * iw_t[MISC_IW + h:MISC_IW + h + 1, :]
            return sc

        def far_body(kt, carry):
            sc_ref[kt] = score_tile(kt)
            return carry

        lax.fori_loop(0, qi, far_body, 0)
        sc_ref[qi] = jnp.where(causal, score_tile(qi), -jnp.inf)

        def bit_body(it, key):
            cand_key = key + (jnp.int32(1) << (31 - it))
            cand = _key_to_float(cand_key)

            def cnt_body(kt, cnt):
                hit = jnp.where(sc_ref[kt] >= cand, 1, 0)
                return cnt + jnp.sum(hit.reshape(tq // 8, 8, tq), axis=0)

            cnt = lax.fori_loop(0, qi + 1, cnt_body, jnp.zeros((8, tq), jnp.int32))
            tot = jnp.sum(cnt, axis=0, keepdims=True)
            return jnp.where((tot >= k_sel) | (cand_key <= NEG_INF_KEY), cand_key, key)

        key = lax.fori_loop(0, 32, bit_body, jnp.full((1, tq), INT_MIN, jnp.int32))
        thr = _key_to_float(key)

        def mask_body(kt, carry):
            sc_ref[kt] = jnp.where(sc_ref[kt] >= thr, 0.0, NEG_BIG)
            return carry

        lax.fori_loop(0, qi, mask_body, 0)
        sc_ref[qi] = jnp.where(causal & (sc_ref[qi] >= thr), 0.0, NEG_BIG)

    zero_blk = jnp.zeros((LANE, LANE), F32)

    for hh in range(hb):
        qs_ref[hh] = q_ref[:, hh * lat:(hh + 1) * lat] * (lat ** -0.5)
    acc_ref[...] = jnp.zeros_like(acc_ref)

    def logits(kt, buf):
        off = pl.multiple_of(kt * tq, tq)
        kv_t = kvn_ref[pl.ds(off, tq), :]
        for hh in range(hb):
            buf[hh] = lax.dot_general(kv_t, qs_ref[hh], nt, preferred_element_type=F32)

    def tile(kt, carry, buf, near=None):
        kv_tt = kvt_ref[kt]
        mask = sc_ref[kt]
        out = []
        for hh in range(hb):
            m_run = carry[hh]
            s = buf[hh] + mask
            if near == "prev":
                s = s + jnp.concatenate([jnp.concatenate([zero_blk, zero_blk], axis=1),
                                         jnp.concatenate([bias_ref[hh, 1], zero_blk], axis=1)], axis=0)
            elif near == "diag":
                s = s + jnp.concatenate([jnp.concatenate([bias_ref[hh, 0], bias_ref[hh, 1]], axis=1),
                                         jnp.concatenate([zero_blk, bias_ref[hh, 0]], axis=1)], axis=0)
            m_new = jnp.maximum(m_run, jnp.max(s, axis=0, keepdims=True))
            alpha = jnp.exp(m_run - m_new)
            p = jnp.exp(s - m_new)
            acc_ref[hh] = alpha * acc_ref[hh] + jnp.dot(kv_tt, p.astype(BF16), preferred_element_type=F32)
            out.append(m_new)
        return tuple(out)

    carry = tuple(jnp.full((1, tq), NEG_BIG, F32) for _ in range(hb))
    n_far = jnp.maximum(qi - 1, 0)

    def diag_and_prev(cr):
        logits(qi - 1, lgb_ref)
        cr = tile(qi, cr, lga_ref, near="diag")
        logits(jnp.maximum(qi - 2, 0), lga_ref)
        return tile(qi - 1, cr, lgb_ref, near="prev")

    def far_pair(j, cr):
        kt_a = qi - 2 - 2 * j
        kt_b = kt_a - 1
        logits(kt_b, lgb_ref)
        cr = tile(kt_a, cr, lga_ref)
        logits(jnp.maximum(kt_b - 1, 0), lga_ref)
        return tile(kt_b, cr, lgb_ref)

    logits(qi, lga_ref)
    carry = lax.cond(qi > 0, diag_and_prev, lambda cr: tile(qi, cr, lga_ref, near="diag"), carry)
    carry = lax.fori_loop(0, n_far // 2, far_pair, carry)
    carry = lax.cond(n_far % 2 == 1, lambda cr: tile(jnp.int32(0), cr, lga_ref), lambda cr: cr, carry)

    for hh in range(hb):
        o_lat_t = (acc_ref[hh, 0:lat, :] / acc_ref[hh, lat:lat + 1, :]).astype(BF16)
        y_t = lax.dot_general(wuv_ref[hh], o_lat_t, (((0,), (0,)), ((), ())), preferred_element_type=F32)
        o_ref[:, hh * DSA_HEAD_DIM:(hh + 1) * DSA_HEAD_DIM] = jnp.transpose(y_t).astype(o_ref.dtype)


def _dsa(z_big, z_small, kv_norm_g, w_uv_bf16, bias_tiles, batch, seq):
    tq, hb, lat = DSA_TQ, DSA_HB, DSA_LATENT
    nq = seq // tq
    k_sel = min(DSA_TOPK, seq // 4)
    kern = functools.partial(_dsa_kernel, seq=seq, k_sel=k_sel)
    return pl.pallas_call(
        kern,
        grid=(batch, nq, DSA_HEADS // hb),
        in_specs=[pl.BlockSpec((tq, hb * lat), lambda b, i, g: (b * nq + i, BIG_QLAT // (hb * lat) + g)),
                  pl.BlockSpec((tq, IDX_HEADS * IDX_DIM), lambda b, i, g: (b * nq + i, 0)),
                  pl.BlockSpec((seq, lat), lambda b, i, g: (b, SM_KV // lat)),
                  pl.BlockSpec((seq, LANE), lambda b, i, g: (b, SM_MISC // LANE)),
                  pl.BlockSpec((tq, LANE), lambda b, i, g: (b * nq + i, SM_MISC // LANE)),
                  pl.BlockSpec((1, lat), lambda b, i, g: (0, 0)),
                  pl.BlockSpec((hb, lat, DSA_HEAD_DIM), lambda b, i, g: (g, 0, 0)),
                  pl.BlockSpec((hb, 2, LANE, LANE), lambda b, i, g: (g, 0, 0, 0))],
        out_specs=pl.BlockSpec((tq, hb * DSA_HEAD_DIM), lambda b, i, g: (b * nq + i, g)),
        out_shape=jax.ShapeDtypeStruct((batch * seq, DSA_HEADS * DSA_HEAD_DIM), BF16),
        scratch_shapes=[pltpu.VMEM((seq, lat), BF16),
                        pltpu.VMEM((nq, lat + DSA_PAD, tq), BF16),
                        pltpu.VMEM((seq, IDX_DIM), BF16),
                        pltpu.VMEM((IDX_HEADS, tq, IDX_DIM), BF16),
                        pltpu.VMEM((nq, tq, tq), F32),
                        pltpu.VMEM((hb, tq, lat), BF16),
                        pltpu.VMEM((hb, tq, tq), F32),
                        pltpu.VMEM((hb, tq, tq), F32),
                        pltpu.VMEM((hb, lat + DSA_PAD, tq), F32)],
        compiler_params=_cparams(("arbitrary", "arbitrary", "arbitrary")),
        name="dsa",
    )(z_big, z_small, z_small, z_small, z_small, kv_norm_g.reshape(1, -1), w_uv_bf16, bias_tiles)


def _merge_mm_kernel(gg_ref, gd_ref, yg_ref, yd_ref, w_ref, x_ref, mod_ref, o_ref, wb_ref):
    @pl.when(pl.program_id(0) == 0)
    def _():
        wb_ref[...] = w_ref[0].astype(BF16)

    merged = (jax.nn.sigmoid(gg_ref[...].astype(F32)) * yg_ref[...].astype(F32)
              + jax.nn.sigmoid(gd_ref[...].astype(F32)) * yd_ref[...].astype(F32))
    y = jnp.dot(merged.astype(BF16), wb_ref[...], preferred_element_type=F32)
    o_ref[...] = x_ref[...] + mod_ref[0, MOD_GT1:MOD_GT1 + 1, :] * y


def _merge_outproj(z_big, y_gla, y_dsa, w_out, layer, x, mod, seq, tm):
    m, d = x.shape
    tpb = seq // tm
    return pl.pallas_call(
        _merge_mm_kernel,
        grid=(m // tm,),
        in_specs=[pl.BlockSpec((tm, d), lambda i: (i, BIG_GGLA // d)),
                  pl.BlockSpec((tm, d), lambda i: (i, BIG_GDSA // d)),
                  pl.BlockSpec((tm, d), lambda i: (i, 0)),
                  pl.BlockSpec((tm, d), lambda i: (i, 0)),
                  pl.BlockSpec((1, d, d), lambda i: (layer, 0, 0), pipeline_mode=pl.Buffered(1)),
                  pl.BlockSpec((tm, d), lambda i: (i, 0)),
                  pl.BlockSpec((1, N_MOD, d), lambda i: (i // tpb, 0, 0))],
        out_specs=pl.BlockSpec((tm, d), lambda i: (i, 0)),
        out_shape=jax.ShapeDtypeStruct((m, d), F32),
        scratch_shapes=[pltpu.VMEM((d, d), BF16)],
        compiler_params=_cparams(("arbitrary",)),
        name="merge_outproj",
    )(z_big, z_big, y_gla, y_dsa, w_out, x, mod)


def _ffn_kernel(x_ref, mod_ref, g_ref, w1_ref, w2_ref, fg_ref, o_ref, h_ref, *, final):
    j = pl.program_id(1)

    @pl.when(j == 0)
    def _():
        h = _rms_mod(x_ref[...], g_ref[0], mod_ref[0, MOD_SC2:MOD_SC2 + 1, :],
                     mod_ref[0, MOD_SH2:MOD_SH2 + 1, :])
        h_ref[...] = h.astype(BF16)
        o_ref[...] = jnp.zeros_like(o_ref)

    u = jnp.dot(h_ref[...], w1_ref[0].astype(BF16), preferred_element_type=F32)
    u = jnp.square(jnp.maximum(u, 0.0)).astype(BF16)
    tf = u.shape[1]
    for n0 in range(0, o_ref.shape[1], tf):
        o_ref[:, n0:n0 + tf] += jnp.dot(u, w2_ref[0, :, n0:n0 + tf].astype(BF16), preferred_element_type=F32)

    @pl.when(j == pl.num_programs(1) - 1)
    def _():
        out = x_ref[...] + mod_ref[0, MOD_GT2:MOD_GT2 + 1, :] * o_ref[...]
        if final:
            out = out * lax.rsqrt(jnp.mean(out * out, axis=-1, keepdims=True) + EPS) * fg_ref[...]
        o_ref[...] = out


def _ffn(x, mod, norm2_g, w1, w2, layer, final_g, seq, tm, tf, final):
    m, d = x.shape
    f = w1.shape[2]
    tpb = seq // tm
    return pl.pallas_call(
        functools.partial(_ffn_kernel, final=final),
        grid=(m // tm, f // tf),
        in_specs=[pl.BlockSpec((tm, d), lambda i, j: (i, 0), pipeline_mode=pl.Buffered(1)),
                  pl.BlockSpec((1, N_MOD, d), lambda i, j: (i // tpb, 0, 0)),
                  pl.BlockSpec((1, 1, d), lambda i, j: (layer, 0, 0)),
                  pl.BlockSpec((1, d, tf), lambda i, j: (layer, 0, j)),
                  pl.BlockSpec((1, tf, d), lambda i, j: (layer, j, 0)),
                  pl.BlockSpec((1, d), lambda i, j: (0, 0))],
        out_specs=pl.BlockSpec((tm, d), lambda i, j: (i, 0)),
        out_shape=jax.ShapeDtypeStruct((m, d), F32),
        scratch_shapes=[pltpu.VMEM((tm, d), BF16)],
        compiler_params=_cparams(("arbitrary", "arbitrary")),
        name="ffn",
    )(x, mod, norm2_g.reshape(norm2_g.shape[0], 1, d), w1, w2, final_g)


def _relayout_tables():
    src = dict(zip(("q_g", "k_g", "v_g", "a_low", "r_g", "q_lat", "kv_lat", "iq", "ik", "iw", "g_gla", "g_dsa"),
                   np.concatenate([[0], np.cumsum(IN_SIZES)[:-1]])))
    order = (("q_lat", DSA_Q_W), ("q_g", GLA_QK_W), ("k_g", GLA_QK_W), ("v_g", GLA_V_W), ("r_g", GLA_V_W),
             ("g_gla", D_MODEL), ("g_dsa", D_MODEL), ("iq", IDX_HEADS * IDX_DIM), ("kv_lat", INPROJ_TN))
    starts = np.array([src[name] + c for name, width in order for c in range(0, width, INPROJ_TN)], np.int32)
    assert len(starts) * INPROJ_TN == BIG_W + SM_WP and not (starts % 8).any()
    assert src["ik"] + IDX_DIM == src["iw"] and src["ik"] % 8 == 0 and src["a_low"] % 8 == 0
    return starts, int(src["ik"]), int(src["a_low"])


def _relayout_kernel(start_ref, w_ref, ikw_ref, al_ref, o_ref):
    del start_ref
    j = pl.program_id(1)
    d = w_ref.shape[2]
    x = w_ref[0]
    ikw = ikw_ref[0]
    tail = jnp.zeros((INPROJ_TN - DSA_LATENT - MISC_IW - IDX_HEADS, d), F32)
    misc = jnp.concatenate([ikw[:IDX_DIM], al_ref[0], ikw[IDX_DIM:], tail], axis=0)
    upper = jnp.where(j == pl.num_programs(1) - 1, misc, x[DSA_LATENT:])
    o_ref[0, :, :DSA_LATENT] = jnp.transpose(x[:DSA_LATENT]).astype(o_ref.dtype)
    o_ref[0, :, DSA_LATENT:] = jnp.transpose(upper).astype(o_ref.dtype)


def _relayout_w_in(w_in):
    depth, d, _ = w_in.shape
    starts, ik_row, al_row = _relayout_tables()
    w_t = jnp.swapaxes(w_in, 1, 2)
    grid_spec = pltpu.PrefetchScalarGridSpec(
        num_scalar_prefetch=1,
        grid=(depth, len(starts)),
        in_specs=[pl.BlockSpec((pl.Element(1), pl.Element(INPROJ_TN), pl.Element(d)),
                               lambda l, j, st: (l, st[j] * 8, 0)),
                  pl.BlockSpec((pl.Element(1), pl.Element(IDX_DIM + IDX_HEADS), pl.Element(d)),
                               lambda l, j, st: (l, ik_row, 0)),
                  pl.BlockSpec((pl.Element(1), pl.Element(GLA_GATE_RANK), pl.Element(d)),
                               lambda l, j, st: (l, al_row, 0))],
        out_specs=pl.BlockSpec((1, d, INPROJ_TN), lambda l, j, st: (l, 0, j)))
    return pl.pallas_call(
        _relayout_kernel,
        grid_spec=grid_spec,
        out_shape=jax.ShapeDtypeStruct((depth, d, BIG_W + SM_WP), BF16),
        compiler_params=_cparams(("arbitrary", "arbitrary")),
        name="w_in_relayout",
    )(jnp.asarray(starts // 8), w_t, w_t, w_t)


def kernel(x, c, w_mod, b_mod, norm1_g, w_in, w_gate_up, b_gate, gla_norm_g, kv_norm_g,
           w_uv, w_out, norm2_g, w_ff1, w_ff2, rel_bias, final_g):
    batch, seq, d = x.shape
    depth = w_mod.shape[0]
    m = batch * seq
    tm = min(1024, seq)

    mods = _modulation(c, w_mod, b_mod).reshape(depth, 8, N_MOD, d)
    bias_tiles = _bias_tiles(rel_bias)
    w_in_all = _relayout_w_in(w_in)
    xf = x.reshape(m, d)
    fg = final_g.reshape(1, d)

    for l in range(depth):
        mod = mods[l]
        z_big, z_small = _norm_matmul(xf, mod, norm1_g[l].reshape(1, d), w_in_all, l, seq, min(2048, seq), INPROJ_TN)
        y_gla = _gla(z_big, z_small, w_gate_up[l], b_gate[l], gla_norm_g[l], batch, seq)
        y_dsa = _dsa(z_big, z_small, kv_norm_g[l], w_uv[l].astype(BF16), bias_tiles, batch, seq)
        xf = _merge_outproj(z_big, y_gla, y_dsa, w_out, l, xf, mod, seq, min(256, seq))
        xf = _ffn(xf, mod, norm2_g, w_ff1, w_ff2, l, fg, seq, tm, 512, final=(l == depth - 1))
    return xf.reshape(batch, seq, d)
```

```python
import functools
import math

import numpy as np
import jax
import jax.numpy as jnp
from jax import lax
from jax.experimental import pallas as pl
from jax.experimental.pallas import tpu as pltpu

F32 = jnp.float32
BF16 = jnp.bfloat16

D_MODEL = 2048
GLA_HEADS = 4
GLA_HEAD_K = 256
GLA_HEAD_V = 512
GLA_GATE_RANK = 16
GLA_GATE_NORMALIZER = 16.0
DSA_HEADS = 16
DSA_HEAD_DIM = 128
DSA_LATENT = 256
DSA_TOPK = 256
IDX_HEADS = 16
IDX_DIM = 64
REL_BUCKETS = 32
REL_MAX_DIST = 128
D_FF = 4 * D_MODEL
N_MOD = 6
EPS = 1e-6

GLA_QK_W = GLA_HEADS * GLA_HEAD_K
GLA_V_W = GLA_HEADS * GLA_HEAD_V
DSA_Q_W = DSA_HEADS * DSA_LATENT
IN_SIZES = (GLA_QK_W, GLA_QK_W, GLA_V_W, GLA_GATE_RANK, GLA_V_W,
            DSA_Q_W, DSA_LATENT, IDX_HEADS * IDX_DIM, IDX_DIM, IDX_HEADS,
            D_MODEL, D_MODEL)

BIG_QLAT = 0
BIG_QG = BIG_QLAT + DSA_Q_W
BIG_KG = BIG_QG + GLA_QK_W
BIG_VG = BIG_KG + GLA_QK_W
BIG_RG = BIG_VG + GLA_V_W
BIG_GGLA = BIG_RG + GLA_V_W
BIG_GDSA = BIG_GGLA + D_MODEL
BIG_W = BIG_GDSA + D_MODEL
SM_IQ = 0
SM_KV = SM_IQ + IDX_HEADS * IDX_DIM
SM_MISC = SM_KV + DSA_LATENT
MISC_IK = 0
MISC_ALOW = MISC_IK + IDX_DIM
MISC_IW = MISC_ALOW + GLA_GATE_RANK
LANE = 128
SM_W = SM_MISC + LANE
INPROJ_TN = 512
INPROJ_ROWS = 256
SM_WP = -(-SM_W // INPROJ_TN) * INPROJ_TN

MOD_SH1, MOD_SC1, MOD_GT1, MOD_SH2, MOD_SC2, MOD_GT2 = range(6)

GLA_CHUNK = 256
GLA_SUB = 8
LOG2E = 1.4426950408889634
GLA_BLOCK = 256
GLA_HB = 4
DSA_TQ = 256
DSA_HB = 8
DSA_PAD = 16
INT_MIN = -2 ** 31
NEG_INF_KEY = 0x807FFFFF - 2 ** 32
NEG_BIG = -1e30

VMEM_LIMIT = 56 * 1024 * 1024


def _cparams(sem):
    return pltpu.CompilerParams(dimension_semantics=sem, vmem_limit_bytes=VMEM_LIMIT)


def _rms_mod(x, g, sc, sh):
    ms = jnp.mean(x * x, axis=-1, keepdims=True)
    return (x * lax.rsqrt(ms + EPS) * g) * (1.0 + sc) + sh


def _mod_kernel(ct_ref, w_ref, b_ref, o_ref, *, batch):
    ct = ct_ref[...]
    ct = ct * jax.nn.sigmoid(ct)
    d, tn = w_ref.shape[1], w_ref.shape[2]
    o_ref[...] = jnp.zeros_like(o_ref)
    for b in range(batch):
        cb = jnp.broadcast_to(ct[:, b:b + 1], (d, LANE))
        for t in range(tn // LANE):
            seg = jnp.sum(w_ref[0, :, t * LANE:(t + 1) * LANE] * cb, axis=0, keepdims=True)
            o_ref[0, b:b + 1, t * LANE:(t + 1) * LANE] = seg + b_ref[0, :, t * LANE:(t + 1) * LANE]


def _modulation(c, w_mod, b_mod):
    depth, d, n = w_mod.shape
    batch = c.shape[0]
    tn = 1024
    ct = jnp.zeros((d, LANE), F32).at[:, :batch].set(c.T)
    return pl.pallas_call(
        functools.partial(_mod_kernel, batch=batch),
        grid=(depth, n // tn),
        in_specs=[pl.BlockSpec((d, LANE), lambda l, j: (0, 0)),
                  pl.BlockSpec((1, d, tn), lambda l, j: (l, 0, j)),
                  pl.BlockSpec((1, 1, tn), lambda l, j: (l, 0, j))],
        out_specs=pl.BlockSpec((1, 8, tn), lambda l, j: (l, 0, j)),
        out_shape=jax.ShapeDtypeStruct((depth, 8, n), F32),
        compiler_params=_cparams(("arbitrary", "arbitrary")),
        name="adaln_mod",
    )(ct, w_mod, b_mod.reshape(depth, 1, n))


def _norm_mm_kernel(x_ref, mod_ref, g_ref, w_ref, big_ref, small_ref, h_ref, *, n_big):
    j = pl.program_id(1)

    @pl.when(j == 0)
    def _():
        rows = min(512, x_ref.shape[0])
        for r0 in range(0, x_ref.shape[0], rows):
            h = _rms_mod(x_ref[r0:r0 + rows, :], g_ref[...], mod_ref[0, MOD_SC1:MOD_SC1 + 1, :],
                         mod_ref[0, MOD_SH1:MOD_SH1 + 1, :])
            h_ref[r0:r0 + rows, :] = h.astype(BF16)

    rows = min(INPROJ_ROWS, h_ref.shape[0])

    @pl.when(j < n_big)
    def _():
        for r0 in range(0, h_ref.shape[0], rows):
            y = jnp.dot(h_ref[r0:r0 + rows, :], w_ref[0], preferred_element_type=F32)
            big_ref[r0:r0 + rows, :] = y.astype(big_ref.dtype)

    @pl.when(j >= n_big)
    def _():
        for r0 in range(0, h_ref.shape[0], rows):
            small_ref[r0:r0 + rows, :] = jnp.dot(h_ref[r0:r0 + rows, :], w_ref[0], preferred_element_type=F32)


def _norm_matmul(x, mod, g, w, layer, seq, tm, tn):
    m, d = x.shape
    n_big, n_small = BIG_W // tn, SM_WP // tn
    tpb = seq // tm
    return pl.pallas_call(
        functools.partial(_norm_mm_kernel, n_big=n_big),
        grid=(m // tm, n_big + n_small),
        in_specs=[pl.BlockSpec((tm, d), lambda i, j: (i, 0), pipeline_mode=pl.Buffered(1)),
                  pl.BlockSpec((1, N_MOD, d), lambda i, j: (i // tpb, 0, 0)),
                  pl.BlockSpec((1, d), lambda i, j: (0, 0)),
                  pl.BlockSpec((1, d, tn), lambda i, j: (layer, 0, j))],
        out_specs=[pl.BlockSpec((tm, tn), lambda i, j: (i, jnp.minimum(j, n_big - 1))),
                   pl.BlockSpec((tm, tn), lambda i, j: (i, jnp.maximum(j - n_big, 0)))],
        out_shape=[jax.ShapeDtypeStruct((m, BIG_W), BF16), jax.ShapeDtypeStruct((m, SM_WP), F32)],
        scratch_shapes=[pltpu.VMEM((tm, d), BF16)],
        compiler_params=_cparams(("arbitrary", "arbitrary")),
        name="norm_inproj",
    )(x, mod, g, w)


def _gla_kernel(q_ref, k_ref, v_ref, r_ref, misc_ref, wa_ref, ba_ref, gn_ref, o_ref,
                s_all_ref, a_all_ref, *, block, chunk):
    dk, dv, sub = GLA_HEAD_K, GLA_HEAD_V, GLA_SUB
    nsub = chunk // sub

    @pl.when(pl.program_id(2) == 0)
    def _():
        s_all_ref[...] = jnp.zeros_like(s_all_ref)

    a_all_ref[...] = jnp.zeros_like(a_all_ref)

    a_low = misc_ref[:, MISC_ALOW:MISC_ALOW + GLA_GATE_RANK].astype(BF16)
    xg = jnp.dot(a_low, wa_ref[...].astype(BF16), preferred_element_type=F32) + ba_ref[...]
    g_all = (jnp.minimum(xg, 0.0) - jnp.log1p(jnp.exp(-jnp.abs(xg)))) * (LOG2E / GLA_GATE_NORMALIZER)

    r_i = lax.broadcasted_iota(jnp.int32, (chunk, chunk), 0)
    c_i = lax.broadcasted_iota(jnp.int32, (chunk, chunk), 1)
    tril = (r_i >= c_i).astype(BF16)
    row_in_chunk = lax.broadcasted_iota(jnp.int32, (chunk, dk), 0)
    lane3 = lax.broadcasted_iota(jnp.int32, (nsub, sub, sub), 2)
    row3 = lax.broadcasted_iota(jnp.int32, (nsub, sub, sub), 1)
    gn = gn_ref[...]

    for c, hh in [(c, hh) for c in range(block // chunk) for hh in range(GLA_HB)]:
        lo = c * chunk
        s_ref, a_ref = s_all_ref.at[hh], a_all_ref.at[hh]
        qf = q_ref[lo:lo + chunk, hh * dk:(hh + 1) * dk].astype(F32) * (dk ** -0.5)
        kf = k_ref[lo:lo + chunk, hh * dk:(hh + 1) * dk].astype(F32)
        vb = v_ref[lo:lo + chunk, hh * dv:(hh + 1) * dv]
        g = g_all[lo:lo + chunk, hh * dk:(hh + 1) * dk]

        g_hi = g.astype(BF16)
        r1 = g - g_hi.astype(F32)
        g_mid = r1.astype(BF16)
        g_lo = (r1 - g_mid.astype(F32)).astype(BF16)
        b = (jnp.dot(tril, g_hi, preferred_element_type=F32)
             + jnp.dot(tril, g_mid, preferred_element_type=F32)
             + jnp.dot(tril, g_lo, preferred_element_type=F32))

        state = s_ref[...]
        o = jnp.dot((qf * jnp.exp2(b)).astype(BF16), state.astype(BF16), preferred_element_type=F32)

        m = chunk // 2
        while m >= sub:
            nseg = chunk // (2 * m)
            pieces = []
            for s in range(nseg):
                ref_row = b[s * 2 * m + m - 1:s * 2 * m + m, :]
                pieces.append(jnp.broadcast_to(ref_row, (2 * m, dk)))
            ref_full = pieces[0] if nseg == 1 else jnp.concatenate(pieces, axis=0)
            fac = jnp.exp2(-jnp.abs(b - ref_full))
            is_q = (row_in_chunk & m) != 0
            xs = jnp.where(is_q, qf, kf) * fac
            for s in range(nseg):
                base = s * 2 * m
                blk = lax.dot_general(xs[base + m:base + 2 * m, :].astype(BF16), xs[base:base + m, :].astype(BF16),
                                      (((1,), (1,)), ((), ())), preferred_element_type=F32)
                a_ref[base + m:base + 2 * m, base:base + m] = blk
            m //= 2

        q4 = qf.reshape(nsub, sub, dk)
        k4 = kf.reshape(nsub, sub, dk)
        b4 = b.reshape(nsub, sub, dk)
        ad = jnp.zeros((nsub, sub, sub), F32)
        for j in range(sub):
            e = jnp.exp2(b4 - b4[:, j:j + 1, :])
            col = jnp.sum(q4 * e * k4[:, j:j + 1, :], axis=-1, keepdims=True)
            ad = jnp.where(lane3 == j, col, ad)
        ad = jnp.where(row3 >= lane3, ad, 0.0)
        for i in range(nsub):
            a_ref[i * sub:(i + 1) * sub, i * sub:(i + 1) * sub] = ad[i]

        o = o + jnp.dot(a_ref[...].astype(BF16), vb, preferred_element_type=F32)

        b_last = b[chunk - 1:chunk, :]
        kd = (kf * jnp.exp2(b_last - b)).astype(BF16)
        upd = lax.dot_general(kd, vb, (((0,), (0,)), ((), ())), preferred_element_type=F32)
        decay_col = jnp.transpose(jnp.broadcast_to(jnp.exp2(b_last), (LANE, dk)))[:, 0:1]
        s_ref[...] = decay_col * state + upd

        o = o * lax.rsqrt(jnp.mean(o * o, axis=-1, keepdims=True) + EPS) * gn
        r = r_ref[lo:lo + chunk, hh * dv:(hh + 1) * dv].astype(F32)
        o_ref[lo:lo + chunk, hh * dv:(hh + 1) * dv] = (o * (r * jax.nn.sigmoid(r))).astype(o_ref.dtype)


def _gla(z_big, z_small, w_gate_up, b_gate, gla_norm_g, batch, seq):
    blk = min(GLA_BLOCK, seq)
    nb = seq // blk
    dk, dv = GLA_HB * GLA_HEAD_K, GLA_HB * GLA_HEAD_V
    row = lambda b, h, t: b * nb + t
    kern = functools.partial(_gla_kernel, block=blk, chunk=GLA_CHUNK)
    return pl.pallas_call(
        kern,
        grid=(batch, GLA_HEADS // GLA_HB, nb),
        in_specs=[pl.BlockSpec((blk, dk), lambda b, h, t: (row(b, h, t), BIG_QG // dk + h)),
                  pl.BlockSpec((blk, dk), lambda b, h, t: (row(b, h, t), BIG_KG // dk + h)),
                  pl.BlockSpec((blk, dv), lambda b, h, t: (row(b, h, t), BIG_VG // dv + h)),
                  pl.BlockSpec((blk, dv), lambda b, h, t: (row(b, h, t), BIG_RG // dv + h)),
                  pl.BlockSpec((blk, LANE), lambda b, h, t: (row(b, h, t), SM_MISC // LANE)),
                  pl.BlockSpec((GLA_GATE_RANK, dk), lambda b, h, t: (0, h)),
                  pl.BlockSpec((1, dk), lambda b, h, t: (0, h)),
                  pl.BlockSpec((1, GLA_HEAD_V), lambda b, h, t: (0, 0))],
        out_specs=pl.BlockSpec((blk, dv), lambda b, h, t: (row(b, h, t), h)),
        out_shape=jax.ShapeDtypeStruct((batch * seq, GLA_V_W), BF16),
        scratch_shapes=[pltpu.VMEM((GLA_HB, GLA_HEAD_K, GLA_HEAD_V), F32),
                        pltpu.VMEM((GLA_HB, GLA_CHUNK, GLA_CHUNK), F32)],
        compiler_params=_cparams(("arbitrary", "arbitrary", "arbitrary")),
        name="gla",
    )(z_big, z_big, z_big, z_big, z_small, w_gate_up, b_gate.reshape(1, -1), gla_norm_g.reshape(1, -1))


def _rel_bucket_np(d):
    max_exact = REL_BUCKETS // 2
    d = np.maximum(d, 0)
    df = np.maximum(d, 1).astype(np.float32)
    large = max_exact + (np.log(df / np.float32(max_exact)) / np.float32(math.log(REL_MAX_DIST / max_exact))
                         * np.float32(REL_BUCKETS - max_exact)).astype(np.int32)
    large = np.minimum(large, REL_BUCKETS - 1)
    return np.where(d < max_exact, d, large).astype(np.int32)


def _bias_kernel(rb_ref, bucket_ref, o_ref):
    h = pl.program_id(0)
    far = rb_ref[REL_BUCKETS - 1, h]
    for u in range(2):
        bk = bucket_ref[u]
        acc = jnp.zeros(bk.shape, F32)
        for bb in range(REL_BUCKETS):
            acc = jnp.where(bk == bb, rb_ref[bb, h], acc)
        o_ref[0, u] = acc - far


def _bias_tiles(rel_bias):
    j = np.arange(LANE)[:, None]
    i = np.arange(LANE)[None, :]
    bucket = np.stack([_rel_bucket_np(i - j), _rel_bucket_np(LANE + i - j)]).astype(np.int32)
    assert int(_rel_bucket_np(np.array([LANE]))[0]) == REL_BUCKETS - 1
    return pl.pallas_call(
        _bias_kernel,
        grid=(DSA_HEADS,),
        in_specs=[pl.BlockSpec(memory_space=pltpu.SMEM),
                  pl.BlockSpec((2, LANE, LANE), lambda h: (0, 0, 0))],
        out_specs=pl.BlockSpec((1, 2, LANE, LANE), lambda h: (h, 0, 0, 0)),
        out_shape=jax.ShapeDtypeStruct((DSA_HEADS, 2, LANE, LANE), F32),
        compiler_params=_cparams(("arbitrary",)),
        name="t5_bias_tiles",
    )(rel_bias, jnp.asarray(bucket))


def _key_to_float(key):
    return pltpu.bitcast(key ^ ((key >> 31) & 0x7FFFFFFF), F32)


def _dsa_kernel(q_ref, iq_ref, kv_ref, misc_all_ref, misc_q_ref, kvg_ref, wuv_ref, bias_ref, o_ref,
                kvn_ref, kvt_ref, ikb_ref, iqh_ref, sc_ref, qs_ref, lga_ref, lgb_ref, acc_ref, *, seq, k_sel):
    tq, hb, lat = DSA_TQ, DSA_HB, DSA_LATENT
    qi = pl.program_id(1)
    hg = pl.program_id(2)
    key_i = lax.broadcasted_iota(jnp.int32, (tq, tq), 0)
    qry_i = lax.broadcasted_iota(jnp.int32, (tq, tq), 1)
    causal = key_i <= qry_i
    nt = (((1,), (1,)), ((), ()))

    @pl.when((qi == 0) & (hg == 0))
    def _():
        kv = kv_ref[...]
        ms = jnp.mean(kv * kv, axis=-1, keepdims=True)
        kvn = kv * lax.rsqrt(ms + EPS) * kvg_ref[...]
        kvn_ref[...] = kvn.astype(BF16)
        ones_rows = (lax.broadcasted_iota(jnp.int32, (DSA_PAD, tq), 0) == 0).astype(BF16)
        for t in range(seq // tq):
            kvt_ref[t, 0:lat, :] = jnp.transpose(kvn[t * tq:(t + 1) * tq, :]).astype(BF16)
            kvt_ref[t, lat:lat + DSA_PAD, :] = ones_rows
        ikb_ref[...] = misc_all_ref[:, MISC_IK:MISC_IK + IDX_DIM].astype(BF16)

    @pl.when(hg == 0)
    def _():
        for h in range(IDX_HEADS):
            iqh_ref[h] = (iq_ref[:, h * IDX_DIM:(h + 1) * IDX_DIM] * (IDX_DIM ** -0.5)).astype(BF16)
        iw_t = jnp.transpose(misc_q_ref[...]) * (IDX_HEADS ** -0.5)

        def score_tile(kt):
            off = pl.multiple_of(kt * tq, tq)
            ik_t = ikb_ref[pl.ds(off, tq), :]
            sc = jnp.zeros((tq, tq), F32)
            for h in range(IDX_HEADS):
                y = lax.dot_general(ik_t, iqh_ref[h], nt, preferred_element_type=F32)
                sc = sc + jnp.maximum(y, 0.0) * iw_t[MISC_IW + h:MISC_IW + h + 1, :]
            return sc

        def far_body(kt, carry):
            sc_ref[kt] = score_tile(kt)
            return carry

        lax.fori_loop(0, qi, far_body, 0)
        sc_ref[qi] = jnp.where(causal, score_tile(qi), -jnp.inf)

        def bit_body(it, key):
            cand_key = key + (jnp.int32(1) << (31 - it))
            cand = _key_to_float(cand_key)

            def cnt_body(kt, cnt):
                hit = jnp.where(sc_ref[kt] >= cand, 1, 0)
                return cnt + jnp.sum(hit.reshape(tq // 8, 8, tq), axis=0)

            cnt = lax.fori_loop(0, qi + 1, cnt_body, jnp.zeros((8, tq), jnp.int32))
            tot = jnp.sum(cnt, axis=0, keepdims=True)
            return jnp.where((tot >= k_sel) | (cand_key <= NEG_INF_KEY), cand_key, key)

        key = lax.fori_loop(0, 32, bit_body, jnp.full((1, tq), INT_MIN, jnp.int32))
        thr = _key_to_float(key)

        def gt_body(kt, cnt):
            hit = jnp.where(sc_ref[kt] > thr, 1, 0)
            return cnt + jnp.sum(hit.reshape(tq // 8, 8, tq), axis=0)

        n_gt = jnp.sum(lax.fori_loop(0, qi + 1, gt_body, jnp.zeros((8, tq), jnp.int32)), axis=0, keepdims=True)
        need = (k_sel - n_gt).astype(F32)
        tri = (qry_i <= key_i).astype(BF16)

        def select(kt, ties_before, extra=None):
            sc = sc_ref[kt]
            tie = sc == thr
            rank = ties_before + jnp.dot(tri, jnp.where(tie, 1.0, 0.0).astype(BF16), preferred_element_type=F32)
            keep = (sc > thr) | (tie & (rank <= need))
            if extra is not None:
                keep = keep & extra
            sc_ref[kt] = jnp.where(keep, 0.0, NEG_BIG)
            return rank[tq - 1:tq, :]

        ties = lax.fori_loop(0, qi, select, jnp.zeros((1, tq), F32))
        select(qi, ties, extra=causal)

    zero_blk = jnp.zeros((LANE, LANE), F32)

    for hh in range(hb):
        qs_ref[hh] = q_ref[:, hh * lat:(hh + 1) * lat] * (lat ** -0.5)
    acc_ref[...] = jnp.zeros_like(acc_ref)

    def logits(kt, buf):
        off = pl.multiple_of(kt * tq, tq)
        kv_t = kvn_ref[pl.ds(off, tq), :]
        for hh in range(hb):
            buf[hh] = lax.dot_general(kv_t, qs_ref[hh], nt, preferred_element_type=F32)

    def tile(kt, carry, buf, near=None):
        kv_tt = kvt_ref[kt]
        mask = sc_ref[kt]
        out = []
        for hh in range(hb):
            m_run = carry[hh]
            s = buf[hh] + mask
            if near == "prev":
                s = s + jnp.concatenate([jnp.concatenate([zero_blk, zero_blk], axis=1),
                                         jnp.concatenate([bias_ref[hh, 1], zero_blk], axis=1)], axis=0)
            elif near == "diag":
                s = s + jnp.concatenate([jnp.concatenate([bias_ref[hh, 0], bias_ref[hh, 1]], axis=1),
                                         jnp.concatenate([zero_blk, bias_ref[hh, 0]], axis=1)], axis=0)
            m_new = jnp.maximum(m_run, jnp.max(s, axis=0, keepdims=True))
            alpha = jnp.exp(m_run - m_new)
            p = jnp.exp(s - m_new)
            acc_ref[hh] = alpha * acc_ref[hh] + jnp.dot(kv_tt, p.astype(BF16), preferred_element_type=F32)
            out.append(m_new)
        return tuple(out)

    carry = tuple(jnp.full((1, tq), NEG_BIG, F32) for _ in range(hb))
    n_far = jnp.maximum(qi - 1, 0)

    def diag_and_prev(cr):
        logits(qi - 1, lgb_ref)
        cr = tile(qi, cr, lga_ref, near="diag")
        logits(jnp.maximum(qi - 2, 0), lga_ref)
        return tile(qi - 1, cr, lgb_ref, near="prev")

    def far_pair(j, cr):
        kt_a = qi - 2 - 2 * j
        kt_b = kt_a - 1
        logits(kt_b, lgb_ref)
        cr = tile(kt_a, cr, lga_ref)
        logits(jnp.maximum(kt_b - 1, 0), lga_ref)
        return tile(kt_b, cr, lgb_ref)

    logits(qi, lga_ref)
    carry = lax.cond(qi > 0, diag_and_prev, lambda cr: tile(qi, cr, lga_ref, near="diag"), carry)
    carry = lax.fori_loop(0, n_far // 2, far_pair, carry)
    carry = lax.cond(n_far % 2 == 1, lambda cr: tile(jnp.int32(0), cr, lga_ref), lambda cr: cr, carry)

    for hh in range(hb):
        o_lat_t = (acc_ref[hh, 0:lat, :] / acc_ref[hh, lat:lat + 1, :]).astype(BF16)
        y_t = lax.dot_general(wuv_ref[hh], o_lat_t, (((0,), (0,)), ((), ())), preferred_element_type=F32)
        o_ref[:, hh * DSA_HEAD_DIM:(hh + 1) * DSA_HEAD_DIM] = jnp.transpose(y_t).astype(o_ref.dtype)


def _dsa(z_big, z_small, kv_norm_g, w_uv_bf16, bias_tiles, batch, seq):
    tq, hb, lat = DSA_TQ, DSA_HB, DSA_LATENT
    nq = seq // tq
    k_sel = min(DSA_TOPK, seq // 4)
    kern = functools.partial(_dsa_kernel, seq=seq, k_sel=k_sel)
    return pl.pallas_call(
        kern,
        grid=(batch, nq, DSA_HEADS // hb),
        in_specs=[pl.BlockSpec((tq, hb * lat), lambda b, i, g: (b * nq + i, BIG_QLAT // (hb * lat) + g)),
                  pl.BlockSpec((tq, IDX_HEADS * IDX_DIM), lambda b, i, g: (b * nq + i, 0)),
                  pl.BlockSpec((seq, lat), lambda b, i, g: (b, SM_KV // lat)),
                  pl.BlockSpec((seq, LANE), lambda b, i, g: (b, SM_MISC // LANE)),
                  pl.BlockSpec((tq, LANE), lambda b, i, g: (b * nq + i, SM_MISC // LANE)),
                  pl.BlockSpec((1, lat), lambda b, i, g: (0, 0)),
                  pl.BlockSpec((hb, lat, DSA_HEAD_DIM), lambda b, i, g: (g, 0, 0)),
                  pl.BlockSpec((hb, 2, LANE, LANE), lambda b, i, g: (g, 0, 0, 0))],
        out_specs=pl.BlockSpec((tq, hb * DSA_HEAD_DIM), lambda b, i, g: (b * nq + i, g)),
        out_shape=jax.ShapeDtypeStruct((batch * seq, DSA_HEADS * DSA_HEAD_DIM), BF16),
        scratch_shapes=[pltpu.VMEM((seq, lat), BF16),
                        pltpu.VMEM((nq, lat + DSA_PAD, tq), BF16),
                        pltpu.VMEM((seq, IDX_DIM), BF16),
                        pltpu.VMEM((IDX_HEADS, tq, IDX_DIM), BF16),
                        pltpu.VMEM((nq, tq, tq), F32),
                        pltpu.VMEM((hb, tq, lat), BF16),
                        pltpu.VMEM((hb, tq, tq), F32),
                        pltpu.VMEM((hb, tq, tq), F32),
                        pltpu.VMEM((hb, lat + DSA_PAD, tq), F32)],
        compiler_params=_cparams(("arbitrary", "arbitrary", "arbitrary")),
        name="dsa",
    )(z_big, z_small, z_small, z_small, z_small, kv_norm_g.reshape(1, -1), w_uv_bf16, bias_tiles)


def _merge_mm_kernel(gg_ref, gd_ref, yg_ref, yd_ref, w_ref, x_ref, mod_ref, o_ref, wb_ref):
    @pl.when(pl.program_id(0) == 0)
    def _():
        wb_ref[...] = w_ref[0].astype(BF16)

    merged = (jax.nn.sigmoid(gg_ref[...].astype(F32)) * yg_ref[...].astype(F32)
              + jax.nn.sigmoid(gd_ref[...].astype(F32)) * yd_ref[...].astype(F32))
    y = jnp.dot(merged.astype(BF16), wb_ref[...], preferred_element_type=F32)
    o_ref[...] = x_ref[...] + mod_ref[0, MOD_GT1:MOD_GT1 + 1, :] * y


def _merge_outproj(z_big, y_gla, y_dsa, w_out, layer, x, mod, seq, tm):
    m, d = x.shape
    tpb = seq // tm
    return pl.pallas_call(
        _merge_mm_kernel,
        grid=(m // tm,),
        in_specs=[pl.BlockSpec((tm, d), lambda i: (i, BIG_GGLA // d)),
                  pl.BlockSpec((tm, d), lambda i: (i, BIG_GDSA // d)),
                  pl.BlockSpec((tm, d), lambda i: (i, 0)),
                  pl.BlockSpec((tm, d), lambda i: (i, 0)),
                  pl.BlockSpec((1, d, d), lambda i: (layer, 0, 0), pipeline_mode=pl.Buffered(1)),
                  pl.BlockSpec((tm, d), lambda i: (i, 0)),
                  pl.BlockSpec((1, N_MOD, d), lambda i: (i // tpb, 0, 0))],
        out_specs=pl.BlockSpec((tm, d), lambda i: (i, 0)),
        out_shape=jax.ShapeDtypeStruct((m, d), F32),
        scratch_shapes=[pltpu.VMEM((d, d), BF16)],
        compiler_params=_cparams(("arbitrary",)),
        name="merge_outproj",
    )(z_big, z_big, y_gla, y_dsa, w_out, x, mod)


def _ffn_kernel(x_ref, mod_ref, g_ref, w1_ref, w2_ref, fg_ref, o_ref, h_ref, *, final):
    j = pl.program_id(1)

    @pl.when(j == 0)
    def _():
        h = _rms_mod(x_ref[...], g_ref[0], mod_ref[0, MOD_SC2:MOD_SC2 + 1, :],
                     mod_ref[0, MOD_SH2:MOD_SH2 + 1, :])
        h_ref[...] = h.astype(BF16)
        o_ref[...] = jnp.zeros_like(o_ref)

    u = jnp.dot(h_ref[...], w1_ref[0].astype(BF16), preferred_element_type=F32)
    u = jnp.square(jnp.maximum(u, 0.0)).astype(BF16)
    tf = u.shape[1]
    for n0 in range(0, o_ref.shape[1], tf):
        o_ref[:, n0:n0 + tf] += jnp.dot(u, w2_ref[0, :, n0:n0 + tf].astype(BF16), preferred_element_type=F32)

    @pl.when(j == pl.num_programs(1) - 1)
    def _():
        out = x_ref[...] + mod_ref[0, MOD_GT2:MOD_GT2 + 1, :] * o_ref[...]
        if final:
            out = out * lax.rsqrt(jnp.mean(out * out, axis=-1, keepdims=True) + EPS) * fg_ref[...]
        o_ref[...] = out


def _ffn(x, mod, norm2_g, w1, w2, layer, final_g, seq, tm, tf, final):
    m, d = x.shape
    f = w1.shape[2]
    tpb = seq // tm
    return pl.pallas_call(
        functools.partial(_ffn_kernel, final=final),
        grid=(m // tm, f // tf),
        in_specs=[pl.BlockSpec((tm, d), lambda i, j: (i, 0), pipeline_mode=pl.Buffered(1)),
                  pl.BlockSpec((1, N_MOD, d), lambda i, j: (i // tpb, 0, 0)),
                  pl.BlockSpec((1, 1, d), lambda i, j: (layer, 0, 0)),
                  pl.BlockSpec((1, d, tf), lambda i, j: (layer, 0, j)),
                  pl.BlockSpec((1, tf, d), lambda i, j: (layer, j, 0)),
                  pl.BlockSpec((1, d), lambda i, j: (0, 0))],
        out_specs=pl.BlockSpec((tm, d), lambda i, j: (i, 0)),
        out_shape=jax.ShapeDtypeStruct((m, d), F32),
        scratch_shapes=[pltpu.VMEM((tm, d), BF16)],
        compiler_params=_cparams(("arbitrary", "arbitrary")),
        name="ffn",
    )(x, mod, norm2_g.reshape(norm2_g.shape[0], 1, d), w1, w2, final_g)


def _relayout_tables():
    src = dict(zip(("q_g", "k_g", "v_g", "a_low", "r_g", "q_lat", "kv_lat", "iq", "ik", "iw", "g_gla", "g_dsa"),
                   np.concatenate([[0], np.cumsum(IN_SIZES)[:-1]])))
    order = (("q_lat", DSA_Q_W), ("q_g", GLA_QK_W), ("k_g", GLA_QK_W), ("v_g", GLA_V_W), ("r_g", GLA_V_W),
             ("g_gla", D_MODEL), ("g_dsa", D_MODEL), ("iq", IDX_HEADS * IDX_DIM), ("kv_lat", INPROJ_TN))
    starts = np.array([src[name] + c for name, width in order for c in range(0, width, INPROJ_TN)], np.int32)
    assert len(starts) * INPROJ_TN == BIG_W + SM_WP and not (starts % 8).any()
    assert src["ik"] + IDX_DIM == src["iw"] and src["ik"] % 8 == 0 and src["a_low"] % 8 == 0
    return starts, int(src["ik"]), int(src["a_low"])


def _relayout_kernel(start_ref, w_ref, ikw_ref, al_ref, o_ref):
    del start_ref
    j = pl.program_id(1)
    d = w_ref.shape[2]
    x = w_ref[0]
    ikw = ikw_ref[0]
    tail = jnp.zeros((INPROJ_TN - DSA_LATENT - MISC_IW - IDX_HEADS, d), F32)
    misc = jnp.concatenate([ikw[:IDX_DIM], al_ref[0], ikw[IDX_DIM:], tail], axis=0)
    upper = jnp.where(j == pl.num_programs(1) - 1, misc, x[DSA_LATENT:])
    o_ref[0, :, :DSA_LATENT] = jnp.transpose(x[:DSA_LATENT]).astype(o_ref.dtype)
    o_ref[0, :, DSA_LATENT:] = jnp.transpose(upper).astype(o_ref.dtype)


def _relayout_w_in(w_in):
    depth, d, _ = w_in.shape
    starts, ik_row, al_row = _relayout_tables()
    w_t = jnp.swapaxes(w_in, 1, 2)
    grid_spec = pltpu.PrefetchScalarGridSpec(
        num_scalar_prefetch=1,
        grid=(depth, len(starts)),
        in_specs=[pl.BlockSpec((pl.Element(1), pl.Element(INPROJ_TN), pl.Element(d)),
                               lambda l, j, st: (l, st[j] * 8, 0)),
                  pl.BlockSpec((pl.Element(1), pl.Element(IDX_DIM + IDX_HEADS), pl.Element(d)),
                               lambda l, j, st: (l, ik_row, 0)),
                  pl.BlockSpec((pl.Element(1), pl.Element(GLA_GATE_RANK), pl.Element(d)),
                               lambda l, j, st: (l, al_row, 0))],
        out_specs=pl.BlockSpec((1, d, INPROJ_TN), lambda l, j, st: (l, 0, j)))
    return pl.pallas_call(
        _relayout_kernel,
        grid_spec=grid_spec,
        out_shape=jax.ShapeDtypeStruct((depth, d, BIG_W + SM_WP), BF16),
        compiler_params=_cparams(("arbitrary", "arbitrary")),
        name="w_in_relayout",
    )(jnp.asarray(starts // 8), w_t, w_t, w_t)


def kernel(x, c, w_mod, b_mod, norm1_g, w_in, w_gate_up, b_gate, gla_norm_g, kv_norm_g,
           w_uv, w_out, norm2_g, w_ff1, w_ff2, rel_bias, final_g):
    batch, seq, d = x.shape
    depth = w_mod.shape[0]
    m = batch * seq
    tm = min(1024, seq)

    mods = _modulation(c, w_mod, b_mod).reshape(depth, 8, N_MOD, d)
    bias_tiles = _bias_tiles(rel_bias)
    w_in_all = _relayout_w_in(w_in)
    xf = x.reshape(m, d)
    fg = final_g.reshape(1, d)

    for l in range(depth):
        mod = mods[l]
        z_big, z_small = _norm_matmul(xf, mod, norm1_g[l].reshape(1, d), w_in_all, l, seq, min(2048, seq), INPROJ_TN)
        y_gla = _gla(z_big, z_small, w_gate_up[l], b_gate[l], gla_norm_g[l], batch, seq)
        y_dsa = _dsa(z_big, z_small, kv_norm_g[l], w_uv[l].astype(BF16), bias_tiles, batch, seq)
        xf = _merge_outproj(z_big, y_gla, y_dsa, w_out, l, xf, mod, seq, min(256, seq))
        xf = _ffn(xf, mod, norm2_g, w_ff1, w_ff2, l, fg, seq, tm, 512, final=(l == depth - 1))
    return xf.reshape(batch, seq, d)
```

```python
import functools
import math

import numpy as np
import jax
import jax.numpy as jnp
from jax import lax
from jax.experimental import pallas as pl
from jax.experimental.pallas import tpu as pltpu

F32 = jnp.float32
BF16 = jnp.bfloat16

D_MODEL = 2048
GLA_HEADS = 4
GLA_HEAD_K = 256
GLA_HEAD_V = 512
GLA_GATE_RANK = 16
GLA_GATE_NORMALIZER = 16.0
DSA_HEADS = 16
DSA_HEAD_DIM = 128
DSA_LATENT = 256
DSA_TOPK = 256
IDX_HEADS = 16
IDX_DIM = 64
REL_BUCKETS = 32
REL_MAX_DIST = 128
D_FF = 4 * D_MODEL
N_MOD = 6
EPS = 1e-6

GLA_QK_W = GLA_HEADS * GLA_HEAD_K
GLA_V_W = GLA_HEADS * GLA_HEAD_V
DSA_Q_W = DSA_HEADS * DSA_LATENT
IN_SIZES = (GLA_QK_W, GLA_QK_W, GLA_V_W, GLA_GATE_RANK, GLA_V_W,
            DSA_Q_W, DSA_LATENT, IDX_HEADS * IDX_DIM, IDX_DIM, IDX_HEADS,
            D_MODEL, D_MODEL)

BIG_QLAT = 0
BIG_QG = BIG_QLAT + DSA_Q_W
BIG_KG = BIG_QG + GLA_QK_W
BIG_VG = BIG_KG + GLA_QK_W
BIG_RG = BIG_VG + GLA_V_W
BIG_GGLA = BIG_RG + GLA_V_W
BIG_GDSA = BIG_GGLA + D_MODEL
BIG_W = BIG_GDSA + D_MODEL
SM_IQ = 0
SM_KV = SM_IQ + IDX_HEADS * IDX_DIM
SM_MISC = SM_KV + DSA_LATENT
MISC_IK = 0
MISC_ALOW = MISC_IK + IDX_DIM
MISC_IW = MISC_ALOW + GLA_GATE_RANK
LANE = 128
SM_W = SM_MISC + LANE
INPROJ_TN = 512
INPROJ_ROWS = 256
SM_WP = -(-SM_W // INPROJ_TN) * INPROJ_TN

MOD_SH1, MOD_SC1, MOD_GT1, MOD_SH2, MOD_SC2, MOD_GT2 = range(6)

GLA_CHUNK = 256
GLA_SUB = 8
LOG2E = 1.4426950408889634
GLA_BLOCK = 256
GLA_HB = 4
DSA_TQ = 256
DSA_HB = 8
DSA_PAD = 16
INT_MIN = -2 ** 31
NEG_INF_KEY = 0x807FFFFF - 2 ** 32
NEG_BIG = -1e30

VMEM_LIMIT = 56 * 1024 * 1024


def _cparams(sem):
    return pltpu.CompilerParams(dimension_semantics=sem, vmem_limit_bytes=VMEM_LIMIT)


def _rms_mod(x, g, sc, sh):
    ms = jnp.mean(x * x, axis=-1, keepdims=True)
    return (x * lax.rsqrt(ms + EPS) * g) * (1.0 + sc) + sh


def _mod_kernel(ct_ref, w_ref, b_ref, o_ref, *, batch):
    ct = ct_ref[...]
    ct = ct * jax.nn.sigmoid(ct)
    d, tn = w_ref.shape[1], w_ref.shape[2]
    o_ref[...] = jnp.zeros_like(o_ref)
    for b in range(batch):
        cb = jnp.broadcast_to(ct[:, b:b + 1], (d, LANE))
        for t in range(tn // LANE):
            seg = jnp.sum(w_ref[0, :, t * LANE:(t + 1) * LANE] * cb, axis=0, keepdims=True)
            o_ref[0, b:b + 1, t * LANE:(t + 1) * LANE] = seg + b_ref[0, :, t * LANE:(t + 1) * LANE]


def _modulation(c, w_mod, b_mod):
    depth, d, n = w_mod.shape
    batch = c.shape[0]
    tn = 1024
    ct = jnp.zeros((d, LANE), F32).at[:, :batch].set(c.T)
    return pl.pallas_call(
        functools.partial(_mod_kernel, batch=batch),
        grid=(depth, n // tn),
        in_specs=[pl.BlockSpec((d, LANE), lambda l, j: (0, 0)),
                  pl.BlockSpec((1, d, tn), lambda l, j: (l, 0, j)),
                  pl.BlockSpec((1, 1, tn), lambda l, j: (l, 0, j))],
        out_specs=pl.BlockSpec((1, 8, tn), lambda l, j: (l, 0, j)),
        out_shape=jax.ShapeDtypeStruct((depth, 8, n), F32),
        compiler_params=_cparams(("arbitrary", "arbitrary")),
        name="adaln_mod",
    )(ct, w_mod, b_mod.reshape(depth, 1, n))


def _norm_mm_kernel(x_ref, mod_ref, g_ref, w_ref, big_ref, small_ref, h_ref, *, n_big):
    j = pl.program_id(1)

    @pl.when(j == 0)
    def _():
        rows = min(512, x_ref.shape[0])
        for r0 in range(0, x_ref.shape[0], rows):
            h = _rms_mod(x_ref[r0:r0 + rows, :], g_ref[...], mod_ref[0, MOD_SC1:MOD_SC1 + 1, :],
                         mod_ref[0, MOD_SH1:MOD_SH1 + 1, :])
            h_ref[r0:r0 + rows, :] = h.astype(BF16)

    rows = min(INPROJ_ROWS, h_ref.shape[0])

    @pl.when(j < n_big)
    def _():
        for r0 in range(0, h_ref.shape[0], rows):
            y = jnp.dot(h_ref[r0:r0 + rows, :], w_ref[0], preferred_element_type=F32)
            big_ref[r0:r0 + rows, :] = y.astype(big_ref.dtype)

    @pl.when(j >= n_big)
    def _():
        for r0 in range(0, h_ref.shape[0], rows):
            small_ref[r0:r0 + rows, :] = jnp.dot(h_ref[r0:r0 + rows, :], w_ref[0], preferred_element_type=F32)


def _norm_matmul(x, mod, g, w, layer, seq, tm, tn):
    m, d = x.shape
    n_big, n_small = BIG_W // tn, SM_WP // tn
    tpb = seq // tm
    return pl.pallas_call(
        functools.partial(_norm_mm_kernel, n_big=n_big),
        grid=(m // tm, n_big + n_small),
        in_specs=[pl.BlockSpec((tm, d), lambda i, j: (i, 0), pipeline_mode=pl.Buffered(1)),
                  pl.BlockSpec((1, N_MOD, d), lambda i, j: (i // tpb, 0, 0)),
                  pl.BlockSpec((1, d), lambda i, j: (0, 0)),
                  pl.BlockSpec((1, d, tn), lambda i, j: (layer, 0, j))],
        out_specs=[pl.BlockSpec((tm, tn), lambda i, j: (i, jnp.minimum(j, n_big - 1))),
                   pl.BlockSpec((tm, tn), lambda i, j: (i, jnp.maximum(j - n_big, 0)))],
        out_shape=[jax.ShapeDtypeStruct((m, BIG_W), BF16), jax.ShapeDtypeStruct((m, SM_WP), F32)],
        scratch_shapes=[pltpu.VMEM((tm, d), BF16)],
        compiler_params=_cparams(("arbitrary", "arbitrary")),
        name="norm_inproj",
    )(x, mod, g, w)


def _gla_kernel(q_ref, k_ref, v_ref, r_ref, misc_ref, wa_ref, ba_ref, gn_ref, o_ref,
                s_all_ref, a_all_ref, *, block, chunk):
    dk, dv, sub = GLA_HEAD_K, GLA_HEAD_V, GLA_SUB
    nsub = chunk // sub

    @pl.when(pl.program_id(2) == 0)
    def _():
        s_all_ref[...] = jnp.zeros_like(s_all_ref)

    a_all_ref[...] = jnp.zeros_like(a_all_ref)

    a_low = misc_ref[:, MISC_ALOW:MISC_ALOW + GLA_GATE_RANK].astype(BF16)
    xg = jnp.dot(a_low, wa_ref[...].astype(BF16), preferred_element_type=F32) + ba_ref[...]
    g_all = (jnp.minimum(xg, 0.0) - jnp.log1p(jnp.exp(-jnp.abs(xg)))) * (LOG2E / GLA_GATE_NORMALIZER)

    r_i = lax.broadcasted_iota(jnp.int32, (chunk, chunk), 0)
    c_i = lax.broadcasted_iota(jnp.int32, (chunk, chunk), 1)
    tril = (r_i >= c_i).astype(BF16)
    row_in_chunk = lax.broadcasted_iota(jnp.int32, (chunk, dk), 0)
    lane3 = lax.broadcasted_iota(jnp.int32, (nsub, sub, sub), 2)
    row3 = lax.broadcasted_iota(jnp.int32, (nsub, sub, sub), 1)
    gn = gn_ref[...]

    for c, hh in [(c, hh) for c in range(block // chunk) for hh in range(GLA_HB)]:
        lo = c * chunk
        s_ref, a_ref = s_all_ref.at[hh], a_all_ref.at[hh]
        qf = q_ref[lo:lo + chunk, hh * dk:(hh + 1) * dk].astype(F32) * (dk ** -0.5)
        kf = k_ref[lo:lo + chunk, hh * dk:(hh + 1) * dk].astype(F32)
        vb = v_ref[lo:lo + chunk, hh * dv:(hh + 1) * dv]
        g = g_all[lo:lo + chunk, hh * dk:(hh + 1) * dk]

        g_hi = g.astype(BF16)
        r1 = g - g_hi.astype(F32)
        g_mid = r1.astype(BF16)
        g_lo = (r1 - g_mid.astype(F32)).astype(BF16)
        b = (jnp.dot(tril, g_hi, preferred_element_type=F32)
             + jnp.dot(tril, g_mid, preferred_element_type=F32)
             + jnp.dot(tril, g_lo, preferred_element_type=F32))

        state = s_ref[...]
        o = jnp.dot((qf * jnp.exp2(b)).astype(BF16), state.astype(BF16), preferred_element_type=F32)

        m = chunk // 2
        while m >= sub:
            nseg = chunk // (2 * m)
            pieces = []
            for s in range(nseg):
                ref_row = b[s * 2 * m + m - 1:s * 2 * m + m, :]
                pieces.append(jnp.broadcast_to(ref_row, (2 * m, dk)))
            ref_full = pieces[0] if nseg == 1 else jnp.concatenate(pieces, axis=0)
            fac = jnp.exp2(-jnp.abs(b - ref_full))
            is_q = (row_in_chunk & m) != 0
            xs = jnp.where(is_q, qf, kf) * fac
            for s in range(nseg):
                base = s * 2 * m
                blk = lax.dot_general(xs[base + m:base + 2 * m, :].astype(BF16), xs[base:base + m, :].astype(BF16),
                                      (((1,), (1,)), ((), ())), preferred_element_type=F32)
                a_ref[base + m:base + 2 * m, base:base + m] = blk
            m //= 2

        q4 = qf.reshape(nsub, sub, dk)
        k4 = kf.reshape(nsub, sub, dk)
        b4 = b.reshape(nsub, sub, dk)
        ad = jnp.zeros((nsub, sub, sub), F32)
        for j in range(sub):
            e = jnp.exp2(b4 - b4[:, j:j + 1, :])
            col = jnp.sum(q4 * e * k4[:, j:j + 1, :], axis=-1, keepdims=True)
            ad = jnp.where(lane3 == j, col, ad)
        ad = jnp.where(row3 >= lane3, ad, 0.0)
        for i in range(nsub):
            a_ref[i * sub:(i + 1) * sub, i * sub:(i + 1) * sub] = ad[i]

        o = o + jnp.dot(a_ref[...].astype(BF16), vb, preferred_element_type=F32)

        b_last = b[chunk - 1:chunk, :]
        kd = (kf * jnp.exp2(b_last - b)).astype(BF16)
        upd = lax.dot_general(kd, vb, (((0,), (0,)), ((), ())), preferred_element_type=F32)
        decay_col = jnp.transpose(jnp.broadcast_to(jnp.exp2(b_last), (LANE, dk)))[:, 0:1]
        s_ref[...] = decay_col * state + upd

        o = o * lax.rsqrt(jnp.mean(o * o, axis=-1, keepdims=True) + EPS) * gn
        r = r_ref[lo:lo + chunk, hh * dv:(hh + 1) * dv].astype(F32)
        o_ref[lo:lo + chunk, hh * dv:(hh + 1) * dv] = (o * (r * jax.nn.sigmoid(r))).astype(o_ref.dtype)


def _gla(z_big, z_small, w_gate_up, b_gate, gla_norm_g, batch, seq):
    blk = min(GLA_BLOCK, seq)
    nb = seq // blk
    dk, dv = GLA_HB * GLA_HEAD_K, GLA_HB * GLA_HEAD_V
    row = lambda b, h, t: b * nb + t
    kern = functools.partial(_gla_kernel, block=blk, chunk=GLA_CHUNK)
    return pl.pallas_call(
        kern,
        grid=(batch, GLA_HEADS // GLA_HB, nb),
        in_specs=[pl.BlockSpec((blk, dk), lambda b, h, t: (row(b, h, t), BIG_QG // dk + h)),
                  pl.BlockSpec((blk, dk), lambda b, h, t: (row(b, h, t), BIG_KG // dk + h)),
                  pl.BlockSpec((blk, dv), lambda b, h, t: (row(b, h, t), BIG_VG // dv + h)),
                  pl.BlockSpec((blk, dv), lambda b, h, t: (row(b, h, t), BIG_RG // dv + h)),
                  pl.BlockSpec((blk, LANE), lambda b, h, t: (row(b, h, t), SM_MISC // LANE)),
                  pl.BlockSpec((GLA_GATE_RANK, dk), lambda b, h, t: (0, h)),
                  pl.BlockSpec((1, dk), lambda b, h, t: (0, h)),
                  pl.BlockSpec((1, GLA_HEAD_V), lambda b, h, t: (0, 0))],
        out_specs=pl.BlockSpec((blk, dv), lambda b, h, t: (row(b, h, t), h)),
        out_shape=jax.ShapeDtypeStruct((batch * seq, GLA_V_W), BF16),
        scratch_shapes=[pltpu.VMEM((GLA_HB, GLA_HEAD_K, GLA_HEAD_V), F32),
                        pltpu.VMEM((GLA_HB, GLA_CHUNK, GLA_CHUNK), F32)],
        compiler_params=_cparams(("arbitrary", "arbitrary", "arbitrary")),
        name="gla",
    )(z_big, z_big, z_big, z_big, z_small, w_gate_up, b_gate.reshape(1, -1), gla_norm_g.reshape(1, -1))


def _rel_bucket_np(d):
    max_exact = REL_BUCKETS // 2
    d = np.maximum(d, 0)
    df = np.maximum(d, 1).astype(np.float32)
    large = max_exact + (np.log(df / np.float32(max_exact)) / np.float32(math.log(REL_MAX_DIST / max_exact))
                         * np.float32(REL_BUCKETS - max_exact)).astype(np.int32)
    large = np.minimum(large, REL_BUCKETS - 1)
    return np.where(d < max_exact, d, large).astype(np.int32)


def _bias_kernel(rb_ref, bucket_ref, o_ref):
    h = pl.program_id(0)
    far = rb_ref[REL_BUCKETS - 1, h]
    for u in range(2):
        bk = bucket_ref[u]
        acc = jnp.zeros(bk.shape, F32)
        for bb in range(REL_BUCKETS):
            acc = jnp.where(bk == bb, rb_ref[bb, h], acc)
        o_ref[0, u] = acc - far


def _bias_tiles(rel_bias):
    j = np.arange(LANE)[:, None]
    i = np.arange(LANE)[None, :]
    bucket = np.stack([_rel_bucket_np(i - j), _rel_bucket_np(LANE + i - j)]).astype(np.int32)
    assert int(_rel_bucket_np(np.array([LANE]))[0]) == REL_BUCKETS - 1
    return pl.pallas_call(
        _bias_kernel,
        grid=(DSA_HEADS,),
        in_specs=[pl.BlockSpec(memory_space=pltpu.SMEM),
                  pl.BlockSpec((2, LANE, LANE), lambda h: (0, 0, 0))],
        out_specs=pl.BlockSpec((1, 2, LANE, LANE), lambda h: (h, 0, 0, 0)),
        out_shape=jax.ShapeDtypeStruct((DSA_HEADS, 2, LANE, LANE), F32),
        compiler_params=_cparams(("arbitrary",)),
        name="t5_bias_tiles",
    )(rel_bias, jnp.asarray(bucket))


def _key_to_float(key):
    return pltpu.bitcast(key ^ ((key >> 31) & 0x7FFFFFFF), F32)


def _dsa_kernel(q_ref, iq_ref, kv_ref, misc_all_ref, misc_q_ref, kvg_ref, wuv_ref, bias_ref, o_ref,
                kvn_ref, kvt_ref, ikb_ref, iqh_ref, sc_ref, hi_ref, qs_ref, lga_ref, lgb_ref, acc_ref, *, seq, k_sel):
    tq, hb, lat = DSA_TQ, DSA_HB, DSA_LATENT
    qi = pl.program_id(1)
    hg = pl.program_id(2)
    key_i = lax.broadcasted_iota(jnp.int32, (tq, tq), 0)
    qry_i = lax.broadcasted_iota(jnp.int32, (tq, tq), 1)
    causal = key_i <= qry_i
    nt = (((1,), (1,)), ((), ()))

    @pl.when((qi == 0) & (hg == 0))
    def _():
        kv = kv_ref[...]
        ms = jnp.mean(kv * kv, axis=-1, keepdims=True)
        kvn = kv * lax.rsqrt(ms + EPS) * kvg_ref[...]
        kvn_ref[...] = kvn.astype(BF16)
        ones_rows = (lax.broadcasted_iota(jnp.int32, (DSA_PAD, tq), 0) == 0).astype(BF16)
        for t in range(seq // tq):
            kvt_ref[t, 0:lat, :] = jnp.transpose(kvn[t * tq:(t + 1) * tq, :]).astype(BF16)
            kvt_ref[t, lat:lat + DSA_PAD, :] = ones_rows
        ikb_ref[...] = misc_all_ref[:, MISC_IK:MISC_IK + IDX_DIM].astype(BF16)

    @pl.when(hg == 0)
    def _():
        for h in range(IDX_HEADS):
            iqh_ref[h] = (iq_ref[:, h * IDX_DIM:(h + 1) * IDX_DIM] * (IDX_DIM ** -0.5)).astype(BF16)
        iw_t = jnp.transpose(misc_q_ref[...]) * (IDX_HEADS ** -0.5)

        def score_tile(kt):
            off = pl.multiple_of(kt * tq, tq)
            ik_t = ikb_ref[pl.ds(off, tq), :]
            sc = jnp.zeros((tq, tq), F32)
            for h in range(IDX_HEADS):
                y = lax.dot_general(ik_t, iqh_ref[h], nt, preferred_element_type=F32)
                sc = sc + jnp.maximum(y, 0.0) * iw_t[MISC_IW + h:MISC_IW + h + 1, :]
            return sc

        def far_body(kt, carry):
            sc_ref[kt] = score_tile(kt)
            return carry

        lax.fori_loop(0, qi, far_body, 0)
        sc_ref[qi] = jnp.where(causal, score_tile(qi), -jnp.inf)

        def count(pred):
            def body(kt, cnt):
                hit = jnp.where(pred(sc_ref[kt]), 1, 0)
                return cnt + jnp.sum(hit.reshape(tq // 8, 8, tq), axis=0)

            return jnp.sum(lax.fori_loop(0, qi + 1, body, jnp.zeros((8, tq), jnp.int32)), axis=0, keepdims=True)

        def bit_body(it, state):
            key, n_ge = state
            cand_key = key + (jnp.int32(1) << (31 - it))
            cand = _key_to_float(cand_key)
            tot = count(lambda sc: sc >= cand)
            take = (tot >= k_sel) | (cand_key <= NEG_INF_KEY)
            return jnp.where(take, cand_key, key), jnp.where(take, tot, n_ge)

        def clear_body(kt, carry):
            bits = pltpu.bitcast(sc_ref[kt], jnp.int32) & jnp.int32(-(1 << 16))
            hi_ref[kt] = pltpu.bitcast(bits, F32).astype(BF16)
            return carry

        lax.fori_loop(0, qi + 1, clear_body, 0)
        one_h, zero_h = jnp.ones((tq, tq), BF16), jnp.zeros((tq, tq), BF16)

        def hi_body(it, state):
            key, n_ge = state
            cand_key = key + (jnp.int32(1) << (31 - it))
            cand = pltpu.bitcast(cand_key ^ ((cand_key >> 31) & 0x7FFF0000), F32).astype(BF16)

            def body(kt, cnt):
                hit = jnp.where(hi_ref[kt] >= cand, one_h, zero_h).reshape(tq // 16, 16, tq)
                for g in range(tq // 16):
                    cnt = cnt + hit[g]
                return cnt

            cnt = lax.fori_loop(0, qi + 1, body, jnp.zeros((16, tq), BF16))
            tot = jnp.sum(cnt.astype(F32), axis=0, keepdims=True).astype(jnp.int32)
            take = (tot >= k_sel) | (cand_key <= (NEG_INF_KEY & -(1 << 16)))
            return jnp.where(take, cand_key, key), jnp.where(take, tot, n_ge)

        state = (jnp.full((1, tq), INT_MIN, jnp.int32), jnp.full((1, tq), k_sel, jnp.int32))
        state = lax.fori_loop(0, 16, hi_body, state)
        key, n_ge = lax.fori_loop(16, 32, bit_body, state)
        thr = _key_to_float(key)

        def select_all(cr):
            def body(kt, c):
                sc_ref[kt] = jnp.where(sc_ref[kt] >= thr, 0.0, NEG_BIG)
                return c

            lax.fori_loop(0, qi, body, 0)
            sc_ref[qi] = jnp.where(causal & (sc_ref[qi] >= thr), 0.0, NEG_BIG)
            return cr

        def select_ranked(cr):
            need = (k_sel - count(lambda sc: sc > thr)).astype(F32)
            tri = (qry_i <= key_i).astype(BF16)

            def body(kt, ties_before, extra=None):
                sc = sc_ref[kt]
                tie = sc == thr
                rank = ties_before + jnp.dot(tri, jnp.where(tie, 1.0, 0.0).astype(BF16),
                                             preferred_element_type=F32)
                keep = (sc > thr) | (tie & (rank <= need))
                if extra is not None:
                    keep = keep & extra
                sc_ref[kt] = jnp.where(keep, 0.0, NEG_BIG)
                return rank[tq - 1:tq, :]

            body(qi, lax.fori_loop(0, qi, body, jnp.zeros((1, tq), F32)), extra=causal)
            return cr

        lax.cond(jnp.max(n_ge) > k_sel, select_ranked, select_all, 0)

    zero_blk = jnp.zeros((LANE, LANE), F32)

    for hh in range(hb):
        qs_ref[hh] = q_ref[:, hh * lat:(hh + 1) * lat] * (lat ** -0.5)
    acc_ref[...] = jnp.zeros_like(acc_ref)

    def logits(kt, buf):
        off = pl.multiple_of(kt * tq, tq)
        kv_t = kvn_ref[pl.ds(off, tq), :]
        for hh in range(hb):
            buf[hh] = lax.dot_general(kv_t, qs_ref[hh], nt, preferred_element_type=F32)

    def tile(kt, carry, buf, near=None):
        kv_tt = kvt_ref[kt]
        mask = sc_ref[kt]
        out = []
        for hh in range(hb):
            m_run = carry[hh]
            s = buf[hh] + mask
            if near == "prev":
                s = s + jnp.concatenate([jnp.concatenate([zero_blk, zero_blk], axis=1),
                                         jnp.concatenate([bias_ref[hh, 1], zero_blk], axis=1)], axis=0)
            elif near == "diag":
                s = s + jnp.concatenate([jnp.concatenate([bias_ref[hh, 0], bias_ref[hh, 1]], axis=1),
                                         jnp.concatenate([zero_blk, bias_ref[hh, 0]], axis=1)], axis=0)
            m_new = jnp.maximum(m_run, jnp.max(s, axis=0, keepdims=True))
            alpha = jnp.exp(m_run - m_new)
            p = jnp.exp(s - m_new)
            acc_ref[hh] = alpha * acc_ref[hh] + jnp.dot(kv_tt, p.astype(BF16), preferred_element_type=F32)
            out.append(m_new)
        return tuple(out)

    carry = tuple(jnp.full((1, tq), NEG_BIG, F32) for _ in range(hb))
    n_far = jnp.maximum(qi - 1, 0)

    def diag_and_prev(cr):
        logits(qi - 1, lgb_ref)
        cr = tile(qi, cr, lga_ref, near="diag")
        logits(jnp.maximum(qi - 2, 0), lga_ref)
        return tile(qi - 1, cr, lgb_ref, near="prev")

    def far_pair(j, cr):
        kt_a = qi - 2 - 2 * j
        kt_b = kt_a - 1
        logits(kt_b, lgb_ref)
        cr = tile(kt_a, cr, lga_ref)
        logits(jnp.maximum(kt_b - 1, 0), lga_ref)
        return tile(kt_b, cr, lgb_ref)

    logits(qi, lga_ref)
    carry = lax.cond(qi > 0, diag_and_prev, lambda cr: tile(qi, cr, lga_ref, near="diag"), carry)
    carry = lax.fori_loop(0, n_far // 2, far_pair, carry)
    carry = lax.cond(n_far % 2 == 1, lambda cr: tile(jnp.int32(0), cr, lga_ref), lambda cr: cr, carry)

    for hh in range(hb):
        o_lat_t = (acc_ref[hh, 0:lat, :] / acc_ref[hh, lat:lat + 1, :]).astype(BF16)
        y_t = lax.dot_general(wuv_ref[hh], o_lat_t, (((0,), (0,)), ((), ())), preferred_element_type=F32)
        o_ref[:, hh * DSA_HEAD_DIM:(hh + 1) * DSA_HEAD_DIM] = jnp.transpose(y_t).astype(o_ref.dtype)


def _dsa(z_big, z_small, kv_norm_g, w_uv_bf16, bias_tiles, batch, seq):
    tq, hb, lat = DSA_TQ, DSA_HB, DSA_LATENT
    nq = seq // tq
    k_sel = min(DSA_TOPK, seq // 4)
    assert nq * 16 <= 256, "per-slot bf16 hit counts must stay exactly representable"
    kern = functools.partial(_dsa_kernel, seq=seq, k_sel=k_sel)
    return pl.pallas_call(
        kern,
        grid=(batch, nq, DSA_HEADS // hb),
        in_specs=[pl.BlockSpec((tq, hb * lat), lambda b, i, g: (b * nq + i, BIG_QLAT // (hb * lat) + g)),
                  pl.BlockSpec((tq, IDX_HEADS * IDX_DIM), lambda b, i, g: (b * nq + i, 0)),
                  pl.BlockSpec((seq, lat), lambda b, i, g: (b, SM_KV // lat)),
                  pl.BlockSpec((seq, LANE), lambda b, i, g: (b, SM_MISC // LANE)),
                  pl.BlockSpec((tq, LANE), lambda b, i, g: (b * nq + i, SM_MISC // LANE)),
                  pl.BlockSpec((1, lat), lambda b, i, g: (0, 0)),
                  pl.BlockSpec((hb, lat, DSA_HEAD_DIM), lambda b, i, g: (g, 0, 0)),
                  pl.BlockSpec((hb, 2, LANE, LANE), lambda b, i, g: (g, 0, 0, 0))],
        out_specs=pl.BlockSpec((tq, hb * DSA_HEAD_DIM), lambda b, i, g: (b * nq + i, g)),
        out_shape=jax.ShapeDtypeStruct((batch * seq, DSA_HEADS * DSA_HEAD_DIM), BF16),
        scratch_shapes=[pltpu.VMEM((seq, lat), BF16),
                        pltpu.VMEM((nq, lat + DSA_PAD, tq), BF16),
                        pltpu.VMEM((seq, IDX_DIM), BF16),
                        pltpu.VMEM((IDX_HEADS, tq, IDX_DIM), BF16),
                        pltpu.VMEM((nq, tq, tq), F32),
                        pltpu.VMEM((nq, tq, tq), BF16),
                        pltpu.VMEM((hb, tq, lat), BF16),
                        pltpu.VMEM((hb, tq, tq), F32),
                        pltpu.VMEM((hb, tq, tq), F32),
                        pltpu.VMEM((hb, lat + DSA_PAD, tq), F32)],
        compiler_params=_cparams(("arbitrary", "arbitrary", "arbitrary")),
        name="dsa",
    )(z_big, z_small, z_small, z_small, z_small, kv_norm_g.reshape(1, -1), w_uv_bf16, bias_tiles)


def _merge_mm_kernel(gg_ref, gd_ref, yg_ref, yd_ref, w_ref, x_ref, mod_ref, o_ref, wb_ref):
    @pl.when(pl.program_id(0) == 0)
    def _():
        wb_ref[...] = w_ref[0].astype(BF16)

    merged = (jax.nn.sigmoid(gg_ref[...].astype(F32)) * yg_ref[...].astype(F32)
              + jax.nn.sigmoid(gd_ref[...].astype(F32)) * yd_ref[...].astype(F32))
    y = jnp.dot(merged.astype(BF16), wb_ref[...], preferred_element_type=F32)
    o_ref[...] = x_ref[...] + mod_ref[0, MOD_GT1:MOD_GT1 + 1, :] * y


def _merge_outproj(z_big, y_gla, y_dsa, w_out, layer, x, mod, seq, tm):
    m, d = x.shape
    tpb = seq // tm
    return pl.pallas_call(
        _merge_mm_kernel,
        grid=(m // tm,),
        in_specs=[pl.BlockSpec((tm, d), lambda i: (i, BIG_GGLA // d)),
                  pl.BlockSpec((tm, d), lambda i: (i, BIG_GDSA // d)),
                  pl.BlockSpec((tm, d), lambda i: (i, 0)),
                  pl.BlockSpec((tm, d), lambda i: (i, 0)),
                  pl.BlockSpec((1, d, d), lambda i: (layer, 0, 0), pipeline_mode=pl.Buffered(1)),
                  pl.BlockSpec((tm, d), lambda i: (i, 0)),
                  pl.BlockSpec((1, N_MOD, d), lambda i: (i // tpb, 0, 0))],
        out_specs=pl.BlockSpec((tm, d), lambda i: (i, 0)),
        out_shape=jax.ShapeDtypeStruct((m, d), F32),
        scratch_shapes=[pltpu.VMEM((d, d), BF16)],
        compiler_params=_cparams(("arbitrary",)),
        name="merge_outproj",
    )(z_big, z_big, y_gla, y_dsa, w_out, x, mod)


def _ffn_kernel(x_ref, mod_ref, g_ref, w1_ref, w2_ref, fg_ref, o_ref, h_ref, *, final):
    j = pl.program_id(1)

    @pl.when(j == 0)
    def _():
        h = _rms_mod(x_ref[...], g_ref[0], mod_ref[0, MOD_SC2:MOD_SC2 + 1, :],
                     mod_ref[0, MOD_SH2:MOD_SH2 + 1, :])
        h_ref[...] = h.astype(BF16)
        o_ref[...] = jnp.zeros_like(o_ref)

    u = jnp.dot(h_ref[...], w1_ref[0].astype(BF16), preferred_element_type=F32)
    u = jnp.square(jnp.maximum(u, 0.0)).astype(BF16)
    tf = u.shape[1]
    for n0 in range(0, o_ref.shape[1], tf):
        o_ref[:, n0:n0 + tf] += jnp.dot(u, w2_ref[0, :, n0:n0 + tf].astype(BF16), preferred_element_type=F32)

    @pl.when(j == pl.num_programs(1) - 1)
    def _():
        out = x_ref[...] + mod_ref[0, MOD_GT2:MOD_GT2 + 1, :] * o_ref[...]
        if final:
            out = out * lax.rsqrt(jnp.mean(out * out, axis=-1, keepdims=True) + EPS) * fg_ref[...]
        o_ref[...] = out


def _ffn(x, mod, norm2_g, w1, w2, layer, final_g, seq, tm, tf, final):
    m, d = x.shape
    f = w1.shape[2]
    tpb = seq // tm
    return pl.pallas_call(
        functools.partial(_ffn_kernel, final=final),
        grid=(m // tm, f // tf),
        in_specs=[pl.BlockSpec((tm, d), lambda i, j: (i, 0), pipeline_mode=pl.Buffered(1)),
                  pl.BlockSpec((1, N_MOD, d), lambda i, j: (i // tpb, 0, 0)),
                  pl.BlockSpec((1, 1, d), lambda i, j: (layer, 0, 0)),
                  pl.BlockSpec((1, d, tf), lambda i, j: (layer, 0, j)),
                  pl.BlockSpec((1, tf, d), lambda i, j: (layer, j, 0)),
                  pl.BlockSpec((1, d), lambda i, j: (0, 0))],
        out_specs=pl.BlockSpec((tm, d), lambda i, j: (i, 0)),
        out_shape=jax.ShapeDtypeStruct((m, d), F32),
        scratch_shapes=[pltpu.VMEM((tm, d), BF16)],
        compiler_params=_cparams(("arbitrary", "arbitrary")),
        name="ffn",
    )(x, mod, norm2_g.reshape(norm2_g.shape[0], 1, d), w1, w2, final_g)


def _relayout_tables():
    src = dict(zip(("q_g", "k_g", "v_g", "a_low", "r_g", "q_lat", "kv_lat", "iq", "ik", "iw", "g_gla", "g_dsa"),
                   np.concatenate([[0], np.cumsum(IN_SIZES)[:-1]])))
    order = (("q_lat", DSA_Q_W), ("q_g", GLA_QK_W), ("k_g", GLA_QK_W), ("v_g", GLA_V_W), ("r_g", GLA_V_W),
             ("g_gla", D_MODEL), ("g_dsa", D_MODEL), ("iq", IDX_HEADS * IDX_DIM), ("kv_lat", INPROJ_TN))
    starts = np.array([src[name] + c for name, width in order for c in range(0, width, INPROJ_TN)], np.int32)
    assert len(starts) * INPROJ_TN == BIG_W + SM_WP and not (starts % 8).any()
    assert src["ik"] + IDX_DIM == src["iw"] and src["ik"] % 8 == 0 and src["a_low"] % 8 == 0
    return starts, int(src["ik"]), int(src["a_low"])


def _relayout_kernel(start_ref, w_ref, ikw_ref, al_ref, o_ref):
    del start_ref
    j = pl.program_id(1)
    d = w_ref.shape[2]
    x = w_ref[0]
    ikw = ikw_ref[0]
    tail = jnp.zeros((INPROJ_TN - DSA_LATENT - MISC_IW - IDX_HEADS, d), F32)
    misc = jnp.concatenate([ikw[:IDX_DIM], al_ref[0], ikw[IDX_DIM:], tail], axis=0)
    upper = jnp.where(j == pl.num_programs(1) - 1, misc, x[DSA_LATENT:])
    o_ref[0, :, :DSA_LATENT] = jnp.transpose(x[:DSA_LATENT]).astype(o_ref.dtype)
    o_ref[0, :, DSA_LATENT:] = jnp.transpose(upper).astype(o_ref.dtype)


def _relayout_w_in(w_in):
    depth, d, _ = w_in.shape
    starts, ik_row, al_row = _relayout_tables()
    w_t = jnp.swapaxes(w_in, 1, 2)
    grid_spec = pltpu.PrefetchScalarGridSpec(
        num_scalar_prefetch=1,
        grid=(depth, len(starts)),
        in_specs=[pl.BlockSpec((pl.Element(1), pl.Element(INPROJ_TN), pl.Element(d)),
                               lambda l, j, st: (l, st[j] * 8, 0)),
                  pl.BlockSpec((pl.Element(1), pl.Element(IDX_DIM + IDX_HEADS), pl.Element(d)),
                               lambda l, j, st: (l, ik_row, 0)),
                  pl.BlockSpec((pl.Element(1), pl.Element(GLA_GATE_RANK), pl.Element(d)),
                               lambda l, j, st: (l, al_row, 0))],
        out_specs=pl.BlockSpec((1, d, INPROJ_TN), lambda l, j, st: (l, 0, j)))
    return pl.pallas_call(
        _relayout_kernel,
        grid_spec=grid_spec,
        out_shape=jax.ShapeDtypeStruct((depth, d, BIG_W + SM_WP), BF16),
        compiler_params=_cparams(("arbitrary", "arbitrary")),
        name="w_in_relayout",
    )(jnp.asarray(starts // 8), w_t, w_t, w_t)


def kernel(x, c, w_mod, b_mod, norm1_g, w_in, w_gate_up, b_gate, gla_norm_g, kv_norm_g,
           w_uv, w_out, norm2_g, w_ff1, w_ff2, rel_bias, final_g):
    batch, seq, d = x.shape
    depth = w_mod.shape[0]
    m = batch * seq
    tm = min(1024, seq)

    mods = _modulation(c, w_mod, b_mod).reshape(depth, 8, N_MOD, d)
    bias_tiles = _bias_tiles(rel_bias)
    w_in_all = _relayout_w_in(w_in)
    xf = x.reshape(m, d)
    fg = final_g.reshape(1, d)

    for l in range(depth):
        mod = mods[l]
        z_big, z_small = _norm_matmul(xf, mod, norm1_g[l].reshape(1, d), w_in_all, l, seq, min(2048, seq), INPROJ_TN)
        y_gla = _gla(z_big, z_small, w_gate_up[l], b_gate[l], gla_norm_g[l], batch, seq)
        y_dsa = _dsa(z_big, z_small, kv_norm_g[l], w_uv[l].astype(BF16), bias_tiles, batch, seq)
        xf = _merge_outproj(z_big, y_gla, y_dsa, w_out, l, xf, mod, seq, min(256, seq))
        xf = _ffn(xf, mod, norm2_g, w_ff1, w_ff2, l, fg, seq, tm, 512, final=(l == depth - 1))
    return xf.reshape(batch, seq, d)
```

```python
import functools
import math

import numpy as np
import jax
import jax.numpy as jnp
from jax import lax
from jax.experimental import pallas as pl
from jax.experimental.pallas import tpu as pltpu

F32 = jnp.float32
BF16 = jnp.bfloat16

D_MODEL = 2048
GLA_HEADS = 4
GLA_HEAD_K = 256
GLA_HEAD_V = 512
GLA_GATE_RANK = 16
GLA_GATE_NORMALIZER = 16.0
DSA_HEADS = 16
DSA_HEAD_DIM = 128
DSA_LATENT = 256
DSA_TOPK = 256
IDX_HEADS = 16
IDX_DIM = 64
REL_BUCKETS = 32
REL_MAX_DIST = 128
D_FF = 4 * D_MODEL
N_MOD = 6
EPS = 1e-6

GLA_QK_W = GLA_HEADS * GLA_HEAD_K
GLA_V_W = GLA_HEADS * GLA_HEAD_V
DSA_Q_W = DSA_HEADS * DSA_LATENT
IN_SIZES = (GLA_QK_W, GLA_QK_W, GLA_V_W, GLA_GATE_RANK, GLA_V_W,
            DSA_Q_W, DSA_LATENT, IDX_HEADS * IDX_DIM, IDX_DIM, IDX_HEADS,
            D_MODEL, D_MODEL)

BIG_QLAT = 0
BIG_QG = BIG_QLAT + DSA_Q_W
BIG_KG = BIG_QG + GLA_QK_W
BIG_VG = BIG_KG + GLA_QK_W
BIG_RG = BIG_VG + GLA_V_W
BIG_GGLA = BIG_RG + GLA_V_W
BIG_GDSA = BIG_GGLA + D_MODEL
BIG_W = BIG_GDSA + D_MODEL
SM_IQ = 0
SM_KV = SM_IQ + IDX_HEADS * IDX_DIM
SM_MISC = SM_KV + DSA_LATENT
MISC_IK = 0
MISC_ALOW = MISC_IK + IDX_DIM
MISC_IW = MISC_ALOW + GLA_GATE_RANK
LANE = 128
SM_W = SM_MISC + LANE
INPROJ_TN = 512
INPROJ_ROWS = 256
SM_WP = -(-SM_W // INPROJ_TN) * INPROJ_TN

MOD_SH1, MOD_SC1, MOD_GT1, MOD_SH2, MOD_SC2, MOD_GT2 = range(6)

GLA_CHUNK = 256
GLA_SUB = 8
LOG2E = 1.4426950408889634
GLA_BLOCK = 256
GLA_HB = 4
DSA_TQ = 256
DSA_HB = 8
DSA_PAD = 16
INT_MIN = -2 ** 31
NEG_INF_KEY = 0x807FFFFF - 2 ** 32
NEG_BIG = -1e30

VMEM_LIMIT = 56 * 1024 * 1024


def _cparams(sem):
    return pltpu.CompilerParams(dimension_semantics=sem, vmem_limit_bytes=VMEM_LIMIT)


def _rms_mod(x, g, sc, sh):
    ms = jnp.mean(x * x, axis=-1, keepdims=True)
    return (x * lax.rsqrt(ms + EPS) * g) * (1.0 + sc) + sh


def _mod_kernel(ct_ref, w_ref, b_ref, o_ref, *, batch):
    ct = ct_ref[...]
    ct = ct * jax.nn.sigmoid(ct)
    d, tn = w_ref.shape[1], w_ref.shape[2]
    o_ref[...] = jnp.zeros_like(o_ref)
    for b in range(batch):
        cb = jnp.broadcast_to(ct[:, b:b + 1], (d, LANE))
        for t in range(tn // LANE):
            seg = jnp.sum(w_ref[0, :, t * LANE:(t + 1) * LANE] * cb, axis=0, keepdims=True)
            o_ref[0, b:b + 1, t * LANE:(t + 1) * LANE] = seg + b_ref[0, :, t * LANE:(t + 1) * LANE]


def _modulation(c, w_mod, b_mod):
    depth, d, n = w_mod.shape
    batch = c.shape[0]
    tn = 1024
    ct = jnp.zeros((d, LANE), F32).at[:, :batch].set(c.T)
    return pl.pallas_call(
        functools.partial(_mod_kernel, batch=batch),
        grid=(depth, n // tn),
        in_specs=[pl.BlockSpec((d, LANE), lambda l, j: (0, 0)),
                  pl.BlockSpec((1, d, tn), lambda l, j: (l, 0, j)),
                  pl.BlockSpec((1, 1, tn), lambda l, j: (l, 0, j))],
        out_specs=pl.BlockSpec((1, 8, tn), lambda l, j: (l, 0, j)),
        out_shape=jax.ShapeDtypeStruct((depth, 8, n), F32),
        compiler_params=_cparams(("arbitrary", "arbitrary")),
        name="adaln_mod",
    )(ct, w_mod, b_mod.reshape(depth, 1, n))


def _norm_mm_kernel(x_ref, mod_ref, g_ref, w_ref, big_ref, small_ref, h_ref, *, n_big):
    j = pl.program_id(1)

    @pl.when(j == 0)
    def _():
        rows = min(512, x_ref.shape[0])
        for r0 in range(0, x_ref.shape[0], rows):
            h = _rms_mod(x_ref[r0:r0 + rows, :], g_ref[...], mod_ref[0, MOD_SC1:MOD_SC1 + 1, :],
                         mod_ref[0, MOD_SH1:MOD_SH1 + 1, :])
            h_ref[r0:r0 + rows, :] = h.astype(BF16)

    rows = min(INPROJ_ROWS, h_ref.shape[0])

    @pl.when(j < n_big)
    def _():
        for r0 in range(0, h_ref.shape[0], rows):
            y = jnp.dot(h_ref[r0:r0 + rows, :], w_ref[0], preferred_element_type=F32)
            big_ref[r0:r0 + rows, :] = y.astype(big_ref.dtype)

    @pl.when(j >= n_big)
    def _():
        for r0 in range(0, h_ref.shape[0], rows):
            small_ref[r0:r0 + rows, :] = jnp.dot(h_ref[r0:r0 + rows, :], w_ref[0], preferred_element_type=F32)


def _norm_matmul(x, mod, g, w, layer, seq, tm, tn):
    m, d = x.shape
    n_big, n_small = BIG_W // tn, SM_WP // tn
    tpb = seq // tm
    return pl.pallas_call(
        functools.partial(_norm_mm_kernel, n_big=n_big),
        grid=(m // tm, n_big + n_small),
        in_specs=[pl.BlockSpec((tm, d), lambda i, j: (i, 0), pipeline_mode=pl.Buffered(1)),
                  pl.BlockSpec((1, N_MOD, d), lambda i, j: (i // tpb, 0, 0)),
                  pl.BlockSpec((1, d), lambda i, j: (0, 0)),
                  pl.BlockSpec((1, d, tn), lambda i, j: (layer, 0, j))],
        out_specs=[pl.BlockSpec((tm, tn), lambda i, j: (i, jnp.minimum(j, n_big - 1))),
                   pl.BlockSpec((tm, tn), lambda i, j: (i, jnp.maximum(j - n_big, 0)))],
        out_shape=[jax.ShapeDtypeStruct((m, BIG_W), BF16), jax.ShapeDtypeStruct((m, SM_WP), F32)],
        scratch_shapes=[pltpu.VMEM((tm, d), BF16)],
        compiler_params=_cparams(("arbitrary", "arbitrary")),
        name="norm_inproj",
    )(x, mod, g, w)


def _gla_kernel(q_ref, k_ref, v_ref, r_ref, misc_ref, wa_ref, ba_ref, gn_ref, o_ref,
                s_all_ref, a_all_ref, *, block, chunk):
    dk, dv, sub = GLA_HEAD_K, GLA_HEAD_V, GLA_SUB
    nsub = chunk // sub

    @pl.when(pl.program_id(2) == 0)
    def _():
        s_all_ref[...] = jnp.zeros_like(s_all_ref)

    a_all_ref[...] = jnp.zeros_like(a_all_ref)

    a_low = misc_ref[:, MISC_ALOW:MISC_ALOW + GLA_GATE_RANK].astype(BF16)
    xg = jnp.dot(a_low, wa_ref[...].astype(BF16), preferred_element_type=F32) + ba_ref[...]
    g_all = (jnp.minimum(xg, 0.0) - jnp.log1p(jnp.exp(-jnp.abs(xg)))) * (LOG2E / GLA_GATE_NORMALIZER)

    r_i = lax.broadcasted_iota(jnp.int32, (chunk, chunk), 0)
    c_i = lax.broadcasted_iota(jnp.int32, (chunk, chunk), 1)
    tril = (r_i >= c_i).astype(BF16)
    row_in_chunk = lax.broadcasted_iota(jnp.int32, (chunk, dk), 0)
    lane3 = lax.broadcasted_iota(jnp.int32, (nsub, sub, sub), 2)
    row3 = lax.broadcasted_iota(jnp.int32, (nsub, sub, sub), 1)
    gn = gn_ref[...]

    for c, hh in [(c, hh) for c in range(block // chunk) for hh in range(GLA_HB)]:
        lo = c * chunk
        s_ref, a_ref = s_all_ref.at[hh], a_all_ref.at[hh]
        qf = q_ref[lo:lo + chunk, hh * dk:(hh + 1) * dk].astype(F32) * (dk ** -0.5)
        kf = k_ref[lo:lo + chunk, hh * dk:(hh + 1) * dk].astype(F32)
        vb = v_ref[lo:lo + chunk, hh * dv:(hh + 1) * dv]
        g = g_all[lo:lo + chunk, hh * dk:(hh + 1) * dk]

        g_hi = g.astype(BF16)
        r1 = g - g_hi.astype(F32)
        g_mid = r1.astype(BF16)
        g_lo = (r1 - g_mid.astype(F32)).astype(BF16)
        b = (jnp.dot(tril, g_hi, preferred_element_type=F32)
             + jnp.dot(tril, g_mid, preferred_element_type=F32)
             + jnp.dot(tril, g_lo, preferred_element_type=F32))

        state = s_ref[...]
        o = jnp.dot((qf * jnp.exp2(b)).astype(BF16), state.astype(BF16), preferred_element_type=F32)

        m = chunk // 2
        while m >= sub:
            nseg = chunk // (2 * m)
            pieces = []
            for s in range(nseg):
                ref_row = b[s * 2 * m + m - 1:s * 2 * m + m, :]
                pieces.append(jnp.broadcast_to(ref_row, (2 * m, dk)))
            ref_full = pieces[0] if nseg == 1 else jnp.concatenate(pieces, axis=0)
            fac = jnp.exp2(-jnp.abs(b - ref_full))
            is_q = (row_in_chunk & m) != 0
            xs = jnp.where(is_q, qf, kf) * fac
            for s in range(nseg):
                base = s * 2 * m
                blk = lax.dot_general(xs[base + m:base + 2 * m, :].astype(BF16), xs[base:base + m, :].astype(BF16),
                                      (((1,), (1,)), ((), ())), preferred_element_type=F32)
                a_ref[base + m:base + 2 * m, base:base + m] = blk
            m //= 2

        q4 = qf.reshape(nsub, sub, dk)
        k4 = kf.reshape(nsub, sub, dk)
        b4 = b.reshape(nsub, sub, dk)
        ad = jnp.zeros((nsub, sub, sub), F32)
        for j in range(sub):
            e = jnp.exp2(b4 - b4[:, j:j + 1, :])
            col = jnp.sum(q4 * e * k4[:, j:j + 1, :], axis=-1, keepdims=True)
            ad = jnp.where(lane3 == j, col, ad)
        ad = jnp.where(row3 >= lane3, ad, 0.0)
        for i in range(nsub):
            a_ref[i * sub:(i + 1) * sub, i * sub:(i + 1) * sub] = ad[i]

        o = o + jnp.dot(a_ref[...].astype(BF16), vb, preferred_element_type=F32)

        b_last = b[chunk - 1:chunk, :]
        kd = (kf * jnp.exp2(b_last - b)).astype(BF16)
        upd = lax.dot_general(kd, vb, (((0,), (0,)), ((), ())), preferred_element_type=F32)
        decay_col = jnp.transpose(jnp.broadcast_to(jnp.exp2(b_last), (LANE, dk)))[:, 0:1]
        s_ref[...] = decay_col * state + upd

        o = o * lax.rsqrt(jnp.mean(o * o, axis=-1, keepdims=True) + EPS) * gn
        r = r_ref[lo:lo + chunk, hh * dv:(hh + 1) * dv].astype(F32)
        o_ref[lo:lo + chunk, hh * dv:(hh + 1) * dv] = (o * (r * jax.nn.sigmoid(r))).astype(o_ref.dtype)


def _gla(z_big, z_small, w_gate_up, b_gate, gla_norm_g, batch, seq):
    blk = min(GLA_BLOCK, seq)
    nb = seq // blk
    dk, dv = GLA_HB * GLA_HEAD_K, GLA_HB * GLA_HEAD_V
    row = lambda b, h, t: b * nb + t
    kern = functools.partial(_gla_kernel, block=blk, chunk=GLA_CHUNK)
    return pl.pallas_call(
        kern,
        grid=(batch, GLA_HEADS // GLA_HB, nb),
        in_specs=[pl.BlockSpec((blk, dk), lambda b, h, t: (row(b, h, t), BIG_QG // dk + h)),
                  pl.BlockSpec((blk, dk), lambda b, h, t: (row(b, h, t), BIG_KG // dk + h)),
                  pl.BlockSpec((blk, dv), lambda b, h, t: (row(b, h, t), BIG_VG // dv + h)),
                  pl.BlockSpec((blk, dv), lambda b, h, t: (row(b, h, t), BIG_RG // dv + h)),
                  pl.BlockSpec((blk, LANE), lambda b, h, t: (row(b, h, t), SM_MISC // LANE)),
                  pl.BlockSpec((GLA_GATE_RANK, dk), lambda b, h, t: (0, h)),
                  pl.BlockSpec((1, dk), lambda b, h, t: (0, h)),
                  pl.BlockSpec((1, GLA_HEAD_V), lambda b, h, t: (0, 0))],
        out_specs=pl.BlockSpec((blk, dv), lambda b, h, t: (row(b, h, t), h)),
        out_shape=jax.ShapeDtypeStruct((batch * seq, GLA_V_W), BF16),
        scratch_shapes=[pltpu.VMEM((GLA_HB, GLA_HEAD_K, GLA_HEAD_V), F32),
                        pltpu.VMEM((GLA_HB, GLA_CHUNK, GLA_CHUNK), F32)],
        compiler_params=_cparams(("arbitrary", "arbitrary", "arbitrary")),
        name="gla",
    )(z_big, z_big, z_big, z_big, z_small, w_gate_up, b_gate.reshape(1, -1), gla_norm_g.reshape(1, -1))


def _rel_bucket_np(d):
    max_exact = REL_BUCKETS // 2
    d = np.maximum(d, 0)
    df = np.maximum(d, 1).astype(np.float32)
    large = max_exact + (np.log(df / np.float32(max_exact)) / np.float32(math.log(REL_MAX_DIST / max_exact))
                         * np.float32(REL_BUCKETS - max_exact)).astype(np.int32)
    large = np.minimum(large, REL_BUCKETS - 1)
    return np.where(d < max_exact, d, large).astype(np.int32)


def _bias_kernel(rb_ref, bucket_ref, o_ref):
    h = pl.program_id(0)
    far = rb_ref[REL_BUCKETS - 1, h]
    for u in range(2):
        bk = bucket_ref[u]
        acc = jnp.zeros(bk.shape, F32)
        for bb in range(REL_BUCKETS):
            acc = jnp.where(bk == bb, rb_ref[bb, h], acc)
        o_ref[0, u] = acc - far


def _bias_tiles(rel_bias):
    j = np.arange(LANE)[:, None]
    i = np.arange(LANE)[None, :]
    bucket = np.stack([_rel_bucket_np(i - j), _rel_bucket_np(LANE + i - j)]).astype(np.int32)
    assert int(_rel_bucket_np(np.array([LANE]))[0]) == REL_BUCKETS - 1
    return pl.pallas_call(
        _bias_kernel,
        grid=(DSA_HEADS,),
        in_specs=[pl.BlockSpec(memory_space=pltpu.SMEM),
                  pl.BlockSpec((2, LANE, LANE), lambda h: (0, 0, 0))],
        out_specs=pl.BlockSpec((1, 2, LANE, LANE), lambda h: (h, 0, 0, 0)),
        out_shape=jax.ShapeDtypeStruct((DSA_HEADS, 2, LANE, LANE), F32),
        compiler_params=_cparams(("arbitrary",)),
        name="t5_bias_tiles",
    )(rel_bias, jnp.asarray(bucket))


def _key_to_float(key):
    return pltpu.bitcast(key ^ ((key >> 31) & 0x7FFFFFFF), F32)


def _dsa_kernel(q_ref, iq_ref, kv_ref, misc_all_ref, misc_q_ref, kvg_ref, wuv_ref, bias_ref, o_ref,
                kvn_ref, kvt_ref, ikb_ref, iqh_ref, sc_ref, hi_ref, qs_ref, lga_ref, lgb_ref, acc_ref, *, seq, k_sel):
    tq, hb, lat = DSA_TQ, DSA_HB, DSA_LATENT
    qi = pl.program_id(1)
    hg = pl.program_id(2)
    key_i = lax.broadcasted_iota(jnp.int32, (tq, tq), 0)
    qry_i = lax.broadcasted_iota(jnp.int32, (tq, tq), 1)
    causal = key_i <= qry_i
    nt = (((1,), (1,)), ((), ()))

    @pl.when((qi == 0) & (hg == 0))
    def _():
        kv = kv_ref[...]
        ms = jnp.mean(kv * kv, axis=-1, keepdims=True)
        kvn = kv * lax.rsqrt(ms + EPS) * kvg_ref[...]
        kvn_ref[...] = kvn.astype(BF16)
        ones_rows = (lax.broadcasted_iota(jnp.int32, (DSA_PAD, tq), 0) == 0).astype(BF16)
        for t in range(seq // tq):
            kvt_ref[t, 0:lat, :] = jnp.transpose(kvn[t * tq:(t + 1) * tq, :]).astype(BF16)
            kvt_ref[t, lat:lat + DSA_PAD, :] = ones_rows
        ikb_ref[...] = misc_all_ref[:, MISC_IK:MISC_IK + IDX_DIM].astype(BF16)

    @pl.when(hg == 0)
    def _():
        for h in range(IDX_HEADS):
            iqh_ref[h] = (iq_ref[:, h * IDX_DIM:(h + 1) * IDX_DIM] * (IDX_DIM ** -0.5)).astype(BF16)
        iw_t = jnp.transpose(misc_q_ref[...]) * (IDX_HEADS ** -0.5)

        def score_tile(kt):
            off = pl.multiple_of(kt * tq, tq)
            ik_t = ikb_ref[pl.ds(off, tq), :]
            sc = jnp.zeros((tq, tq), F32)
            for h in range(IDX_HEADS):
                y = lax.dot_general(ik_t, iqh_ref[h], nt, preferred_element_type=F32)
                sc = sc + jnp.maximum(y, 0.0) * iw_t[MISC_IW + h:MISC_IW + h + 1, :]
            return sc

        def put_scores(kt, sc):
            sc_ref[kt] = sc
            bits = pltpu.bitcast(sc, jnp.int32) & jnp.int32(-(1 << 16))
            hi_ref[kt] = pltpu.bitcast(bits, F32).astype(BF16)

        def far_body(kt, carry):
            put_scores(kt, score_tile(kt))
            return carry

        lax.fori_loop(0, qi, far_body, 0)
        put_scores(qi, jnp.where(causal, score_tile(qi), -jnp.inf))

        def count(pred):
            def body(kt, cnt):
                hit = jnp.where(pred(sc_ref[kt]), 1, 0)
                return cnt + jnp.sum(hit.reshape(tq // 8, 8, tq), axis=0)

            return jnp.sum(lax.fori_loop(0, qi + 1, body, jnp.zeros((8, tq), jnp.int32)), axis=0, keepdims=True)

        def bit_body(it, state):
            key, n_ge = state
            cand_key = key + (jnp.int32(1) << (31 - it))
            cand = _key_to_float(cand_key)
            tot = count(lambda sc: sc >= cand)
            take = (tot >= k_sel) | (cand_key <= NEG_INF_KEY)
            return jnp.where(take, cand_key, key), jnp.where(take, tot, n_ge)

        one_h, zero_h = jnp.ones((tq, tq), BF16), jnp.zeros((tq, tq), BF16)

        def hi_body(it, state):
            key, n_ge = state
            cand_key = key + (jnp.int32(1) << (31 - it))
            cand = pltpu.bitcast(cand_key ^ ((cand_key >> 31) & 0x7FFF0000), F32).astype(BF16)

            def body(kt, cnt):
                hit = jnp.where(hi_ref[kt] >= cand, one_h, zero_h).reshape(tq // 16, 16, tq)
                for g in range(tq // 16):
                    cnt = cnt + hit[g]
                return cnt

            cnt = lax.fori_loop(0, qi + 1, body, jnp.zeros((16, tq), BF16))
            tot = jnp.sum(cnt.astype(F32), axis=0, keepdims=True).astype(jnp.int32)
            take = (tot >= k_sel) | (cand_key <= (NEG_INF_KEY & -(1 << 16)))
            return jnp.where(take, cand_key, key), jnp.where(take, tot, n_ge)

        state = (jnp.full((1, tq), INT_MIN, jnp.int32), jnp.full((1, tq), k_sel, jnp.int32))
        state = lax.fori_loop(0, 16, hi_body, state)
        key, n_ge = lax.fori_loop(16, 32, bit_body, state)
        thr = _key_to_float(key)

        def select_all(cr):
            def body(kt, c):
                sc_ref[kt] = jnp.where(sc_ref[kt] >= thr, 0.0, NEG_BIG)
                return c

            lax.fori_loop(0, qi, body, 0)
            sc_ref[qi] = jnp.where(causal & (sc_ref[qi] >= thr), 0.0, NEG_BIG)
            return cr

        def select_ranked(cr):
            need = (k_sel - count(lambda sc: sc > thr)).astype(F32)
            tri = (qry_i <= key_i).astype(BF16)

            def body(kt, ties_before, extra=None):
                sc = sc_ref[kt]
                tie = sc == thr
                rank = ties_before + jnp.dot(tri, jnp.where(tie, 1.0, 0.0).astype(BF16),
                                             preferred_element_type=F32)
                keep = (sc > thr) | (tie & (rank <= need))
                if extra is not None:
                    keep = keep & extra
                sc_ref[kt] = jnp.where(keep, 0.0, NEG_BIG)
                return rank[tq - 1:tq, :]

            body(qi, lax.fori_loop(0, qi, body, jnp.zeros((1, tq), F32)), extra=causal)
            return cr

        lax.cond(jnp.max(n_ge) > k_sel, select_ranked, select_all, 0)

    zero_blk = jnp.zeros((LANE, LANE), F32)

    for hh in range(hb):
        qs_ref[hh] = q_ref[:, hh * lat:(hh + 1) * lat] * (lat ** -0.5)
    acc_ref[...] = jnp.zeros_like(acc_ref)

    def logits(kt, buf):
        off = pl.multiple_of(kt * tq, tq)
        kv_t = kvn_ref[pl.ds(off, tq), :]
        for hh in range(hb):
            buf[hh] = lax.dot_general(kv_t, qs_ref[hh], nt, preferred_element_type=F32)

    def tile(kt, carry, buf, near=None):
        kv_tt = kvt_ref[kt]
        mask = sc_ref[kt]
        out = []
        for hh in range(hb):
            m_run = carry[hh]
            s = buf[hh] + mask
            if near == "prev":
                s = s + jnp.concatenate([jnp.concatenate([zero_blk, zero_blk], axis=1),
                                         jnp.concatenate([bias_ref[hh, 1], zero_blk], axis=1)], axis=0)
            elif near == "diag":
                s = s + jnp.concatenate([jnp.concatenate([bias_ref[hh, 0], bias_ref[hh, 1]], axis=1),
                                         jnp.concatenate([zero_blk, bias_ref[hh, 0]], axis=1)], axis=0)
            m_new = jnp.maximum(m_run, jnp.max(s, axis=0, keepdims=True))
            alpha = jnp.exp(m_run - m_new)
            p = jnp.exp(s - m_new)
            acc_ref[hh] = alpha * acc_ref[hh] + jnp.dot(kv_tt, p.astype(BF16), preferred_element_type=F32)
            out.append(m_new)
        return tuple(out)

    carry = tuple(jnp.full((1, tq), NEG_BIG, F32) for _ in range(hb))
    n_far = jnp.maximum(qi - 1, 0)

    def diag_and_prev(cr):
        logits(qi - 1, lgb_ref)
        cr = tile(qi, cr, lga_ref, near="diag")
        logits(jnp.maximum(qi - 2, 0), lga_ref)
        return tile(qi - 1, cr, lgb_ref, near="prev")

    def far_pair(j, cr):
        kt_a = qi - 2 - 2 * j
        kt_b = kt_a - 1
        logits(kt_b, lgb_ref)
        cr = tile(kt_a, cr, lga_ref)
        logits(jnp.maximum(kt_b - 1, 0), lga_ref)
        return tile(kt_b, cr, lgb_ref)

    logits(qi, lga_ref)
    carry = lax.cond(qi > 0, diag_and_prev, lambda cr: tile(qi, cr, lga_ref, near="diag"), carry)
    carry = lax.fori_loop(0, n_far // 2, far_pair, carry)
    carry = lax.cond(n_far % 2 == 1, lambda cr: tile(jnp.int32(0), cr, lga_ref), lambda cr: cr, carry)

    for hh in range(hb):
        o_lat_t = (acc_ref[hh, 0:lat, :] / acc_ref[hh, lat:lat + 1, :]).astype(BF16)
        y_t = lax.dot_general(wuv_ref[hh], o_lat_t, (((0,), (0,)), ((), ())), preferred_element_type=F32)
        o_ref[:, hh * DSA_HEAD_DIM:(hh + 1) * DSA_HEAD_DIM] = jnp.transpose(y_t).astype(o_ref.dtype)


def _dsa(z_big, z_small, kv_norm_g, w_uv_bf16, bias_tiles, batch, seq):
    tq, hb, lat = DSA_TQ, DSA_HB, DSA_LATENT
    nq = seq // tq
    k_sel = min(DSA_TOPK, seq // 4)
    assert nq * 16 <= 256, "per-slot bf16 hit counts must stay exactly representable"
    kern = functools.partial(_dsa_kernel, seq=seq, k_sel=k_sel)
    return pl.pallas_call(
        kern,
        grid=(batch, nq, DSA_HEADS // hb),
        in_specs=[pl.BlockSpec((tq, hb * lat), lambda b, i, g: (b * nq + i, BIG_QLAT // (hb * lat) + g)),
                  pl.BlockSpec((tq, IDX_HEADS * IDX_DIM), lambda b, i, g: (b * nq + i, 0)),
                  pl.BlockSpec((seq, lat), lambda b, i, g: (b, SM_KV // lat)),
                  pl.BlockSpec((seq, LANE), lambda b, i, g: (b, SM_MISC // LANE)),
                  pl.BlockSpec((tq, LANE), lambda b, i, g: (b * nq + i, SM_MISC // LANE)),
                  pl.BlockSpec((1, lat), lambda b, i, g: (0, 0)),
                  pl.BlockSpec((hb, lat, DSA_HEAD_DIM), lambda b, i, g: (g, 0, 0)),
                  pl.BlockSpec((hb, 2, LANE, LANE), lambda b, i, g: (g, 0, 0, 0))],
        out_specs=pl.BlockSpec((tq, hb * DSA_HEAD_DIM), lambda b, i, g: (b * nq + i, g)),
        out_shape=jax.ShapeDtypeStruct((batch * seq, DSA_HEADS * DSA_HEAD_DIM), BF16),
        scratch_shapes=[pltpu.VMEM((seq, lat), BF16),
                        pltpu.VMEM((nq, lat + DSA_PAD, tq), BF16),
                        pltpu.VMEM((seq, IDX_DIM), BF16),
                        pltpu.VMEM((IDX_HEADS, tq, IDX_DIM), BF16),
                        pltpu.VMEM((nq, tq, tq), F32),
                        pltpu.VMEM((nq, tq, tq), BF16),
                        pltpu.VMEM((hb, tq, lat), BF16),
                        pltpu.VMEM((hb, tq, tq), F32),
                        pltpu.VMEM((hb, tq, tq), F32),
                        pltpu.VMEM((hb, lat + DSA_PAD, tq), F32)],
        compiler_params=_cparams(("arbitrary", "arbitrary", "arbitrary")),
        name="dsa",
    )(z_big, z_small, z_small, z_small, z_small, kv_norm_g.reshape(1, -1), w_uv_bf16, bias_tiles)


def _merge_mm_kernel(gg_ref, gd_ref, yg_ref, yd_ref, w_ref, x_ref, mod_ref, o_ref, wb_ref):
    @pl.when(pl.program_id(0) == 0)
    def _():
        wb_ref[...] = w_ref[0].astype(BF16)

    merged = (jax.nn.sigmoid(gg_ref[...].astype(F32)) * yg_ref[...].astype(F32)
              + jax.nn.sigmoid(gd_ref[...].astype(F32)) * yd_ref[...].astype(F32))
    y = jnp.dot(merged.astype(BF16), wb_ref[...], preferred_element_type=F32)
    o_ref[...] = x_ref[...] + mod_ref[0, MOD_GT1:MOD_GT1 + 1, :] * y


def _merge_outproj(z_big, y_gla, y_dsa, w_out, layer, x, mod, seq, tm):
    m, d = x.shape
    tpb = seq // tm
    return pl.pallas_call(
        _merge_mm_kernel,
        grid=(m // tm,),
        in_specs=[pl.BlockSpec((tm, d), lambda i: (i, BIG_GGLA // d)),
                  pl.BlockSpec((tm, d), lambda i: (i, BIG_GDSA // d)),
                  pl.BlockSpec((tm, d), lambda i: (i, 0)),
                  pl.BlockSpec((tm, d), lambda i: (i, 0)),
                  pl.BlockSpec((1, d, d), lambda i: (layer, 0, 0), pipeline_mode=pl.Buffered(1)),
                  pl.BlockSpec((tm, d), lambda i: (i, 0)),
                  pl.BlockSpec((1, N_MOD, d), lambda i: (i // tpb, 0, 0))],
        out_specs=pl.BlockSpec((tm, d), lambda i: (i, 0)),
        out_shape=jax.ShapeDtypeStruct((m, d), F32),
        scratch_shapes=[pltpu.VMEM((d, d), BF16)],
        compiler_params=_cparams(("arbitrary",)),
        name="merge_outproj",
    )(z_big, z_big, y_gla, y_dsa, w_out, x, mod)


def _ffn_kernel(x_ref, mod_ref, g_ref, w1_ref, w2_ref, fg_ref, o_ref, h_ref, *, final):
    j = pl.program_id(1)

    @pl.when(j == 0)
    def _():
        h = _rms_mod(x_ref[...], g_ref[0], mod_ref[0, MOD_SC2:MOD_SC2 + 1, :],
                     mod_ref[0, MOD_SH2:MOD_SH2 + 1, :])
        h_ref[...] = h.astype(BF16)
        o_ref[...] = jnp.zeros_like(o_ref)

    u = jnp.dot(h_ref[...], w1_ref[0].astype(BF16), preferred_element_type=F32)
    u = jnp.square(jnp.maximum(u, 0.0)).astype(BF16)
    tf = u.shape[1]
    for n0 in range(0, o_ref.shape[1], tf):
        o_ref[:, n0:n0 + tf] += jnp.dot(u, w2_ref[0, :, n0:n0 + tf].astype(BF16), preferred_element_type=F32)

    @pl.when(j == pl.num_programs(1) - 1)
    def _():
        out = x_ref[...] + mod_ref[0, MOD_GT2:MOD_GT2 + 1, :] * o_ref[...]
        if final:
            out = out * lax.rsqrt(jnp.mean(out * out, axis=-1, keepdims=True) + EPS) * fg_ref[...]
        o_ref[...] = out


def _ffn(x, mod, norm2_g, w1, w2, layer, final_g, seq, tm, tf, final):
    m, d = x.shape
    f = w1.shape[2]
    tpb = seq // tm
    return pl.pallas_call(
        functools.partial(_ffn_kernel, final=final),
        grid=(m // tm, f // tf),
        in_specs=[pl.BlockSpec((tm, d), lambda i, j: (i, 0), pipeline_mode=pl.Buffered(1)),
                  pl.BlockSpec((1, N_MOD, d), lambda i, j: (i // tpb, 0, 0)),
                  pl.BlockSpec((1, 1, d), lambda i, j: (layer, 0, 0)),
                  pl.BlockSpec((1, d, tf), lambda i, j: (layer, 0, j)),
                  pl.BlockSpec((1, tf, d), lambda i, j: (layer, j, 0)),
                  pl.BlockSpec((1, d), lambda i, j: (0, 0))],
        out_specs=pl.BlockSpec((tm, d), lambda i, j: (i, 0)),
        out_shape=jax.ShapeDtypeStruct((m, d), F32),
        scratch_shapes=[pltpu.VMEM((tm, d), BF16)],
        compiler_params=_cparams(("arbitrary", "arbitrary")),
        name="ffn",
    )(x, mod, norm2_g.reshape(norm2_g.shape[0], 1, d), w1, w2, final_g)


def _relayout_tables():
    src = dict(zip(("q_g", "k_g", "v_g", "a_low", "r_g", "q_lat", "kv_lat", "iq", "ik", "iw", "g_gla", "g_dsa"),
                   np.concatenate([[0], np.cumsum(IN_SIZES)[:-1]])))
    order = (("q_lat", DSA_Q_W), ("q_g", GLA_QK_W), ("k_g", GLA_QK_W), ("v_g", GLA_V_W), ("r_g", GLA_V_W),
             ("g_gla", D_MODEL), ("g_dsa", D_MODEL), ("iq", IDX_HEADS * IDX_DIM), ("kv_lat", INPROJ_TN))
    starts = np.array([src[name] + c for name, width in order for c in range(0, width, INPROJ_TN)], np.int32)
    assert len(starts) * INPROJ_TN == BIG_W + SM_WP and not (starts % 8).any()
    assert src["ik"] + IDX_DIM == src["iw"] and src["ik"] % 8 == 0 and src["a_low"] % 8 == 0
    return starts, int(src["ik"]), int(src["a_low"])


def _relayout_kernel(start_ref, w_ref, ikw_ref, al_ref, o_ref):
    del start_ref
    j = pl.program_id(1)
    d = w_ref.shape[2]
    x = w_ref[0]
    ikw = ikw_ref[0]
    tail = jnp.zeros((INPROJ_TN - DSA_LATENT - MISC_IW - IDX_HEADS, d), F32)
    misc = jnp.concatenate([ikw[:IDX_DIM], al_ref[0], ikw[IDX_DIM:], tail], axis=0)
    upper = jnp.where(j == pl.num_programs(1) - 1, misc, x[DSA_LATENT:])
    o_ref[0, :, :DSA_LATENT] = jnp.transpose(x[:DSA_LATENT]).astype(o_ref.dtype)
    o_ref[0, :, DSA_LATENT:] = jnp.transpose(upper).astype(o_ref.dtype)


def _relayout_w_in(w_in):
    depth, d, _ = w_in.shape
    starts, ik_row, al_row = _relayout_tables()
    w_t = jnp.swapaxes(w_in, 1, 2)
    grid_spec = pltpu.PrefetchScalarGridSpec(
        num_scalar_prefetch=1,
        grid=(depth, len(starts)),
        in_specs=[pl.BlockSpec((pl.Element(1), pl.Element(INPROJ_TN), pl.Element(d)),
                               lambda l, j, st: (l, st[j] * 8, 0)),
                  pl.BlockSpec((pl.Element(1), pl.Element(IDX_DIM + IDX_HEADS), pl.Element(d)),
                               lambda l, j, st: (l, ik_row, 0)),
                  pl.BlockSpec((pl.Element(1), pl.Element(GLA_GATE_RANK), pl.Element(d)),
                               lambda l, j, st: (l, al_row, 0))],
        out_specs=pl.BlockSpec((1, d, INPROJ_TN), lambda l, j, st: (l, 0, j)))
    return pl.pallas_call(
        _relayout_kernel,
        grid_spec=grid_spec,
        out_shape=jax.ShapeDtypeStruct((depth, d, BIG_W + SM_WP), BF16),
        compiler_params=_cparams(("arbitrary", "arbitrary")),
        name="w_in_relayout",
    )(jnp.asarray(starts // 8), w_t, w_t, w_t)


def kernel(x, c, w_mod, b_mod, norm1_g, w_in, w_gate_up, b_gate, gla_norm_g, kv_norm_g,
           w_uv, w_out, norm2_g, w_ff1, w_ff2, rel_bias, final_g):
    batch, seq, d = x.shape
    depth = w_mod.shape[0]
    m = batch * seq
    tm = min(1024, seq)

    mods = _modulation(c, w_mod, b_mod).reshape(depth, 8, N_MOD, d)
    bias_tiles = _bias_tiles(rel_bias)
    w_in_all = _relayout_w_in(w_in)
    xf = x.reshape(m, d)
    fg = final_g.reshape(1, d)

    for l in range(depth):
        mod = mods[l]
        z_big, z_small = _norm_matmul(xf, mod, norm1_g[l].reshape(1, d), w_in_all, l, seq, min(2048, seq), INPROJ_TN)
        y_gla = _gla(z_big, z_small, w_gate_up[l], b_gate[l], gla_norm_g[l], batch, seq)
        y_dsa = _dsa(z_big, z_small, kv_norm_g[l], w_uv[l].astype(BF16), bias_tiles, batch, seq)
        xf = _merge_outproj(z_big, y_gla, y_dsa, w_out, l, xf, mod, seq, min(256, seq))
        xf = _ffn(xf, mod, norm2_g, w_ff1, w_ff2, l, fg, seq, tm, 512, final=(l == depth - 1))
    return xf.reshape(batch, seq, d)
```

```python
import functools
import math

import numpy as np
import jax
import jax.numpy as jnp
from jax import lax
from jax.experimental import pallas as pl
from jax.experimental.pallas import tpu as pltpu

F32 = jnp.float32
BF16 = jnp.bfloat16

D_MODEL = 2048
GLA_HEADS = 4
GLA_HEAD_K = 256
GLA_HEAD_V = 512
GLA_GATE_RANK = 16
GLA_GATE_NORMALIZER = 16.0
DSA_HEADS = 16
DSA_HEAD_DIM = 128
DSA_LATENT = 256
DSA_TOPK = 256
IDX_HEADS = 16
IDX_DIM = 64
REL_BUCKETS = 32
REL_MAX_DIST = 128
D_FF = 4 * D_MODEL
N_MOD = 6
EPS = 1e-6

GLA_QK_W = GLA_HEADS * GLA_HEAD_K
GLA_V_W = GLA_HEADS * GLA_HEAD_V
DSA_Q_W = DSA_HEADS * DSA_LATENT
IN_SIZES = (GLA_QK_W, GLA_QK_W, GLA_V_W, GLA_GATE_RANK, GLA_V_W,
            DSA_Q_W, DSA_LATENT, IDX_HEADS * IDX_DIM, IDX_DIM, IDX_HEADS,
            D_MODEL, D_MODEL)

BIG_QLAT = 0
BIG_QG = BIG_QLAT + DSA_Q_W
BIG_KG = BIG_QG + GLA_QK_W
BIG_VG = BIG_KG + GLA_QK_W
BIG_RG = BIG_VG + GLA_V_W
BIG_GGLA = BIG_RG + GLA_V_W
BIG_GDSA = BIG_GGLA + D_MODEL
BIG_W = BIG_GDSA + D_MODEL
SM_IQ = 0
SM_KV = SM_IQ + IDX_HEADS * IDX_DIM
SM_MISC = SM_KV + DSA_LATENT
MISC_IK = 0
MISC_ALOW = MISC_IK + IDX_DIM
MISC_IW = MISC_ALOW + GLA_GATE_RANK
LANE = 128
SM_W = SM_MISC + LANE
INPROJ_TN = 512
INPROJ_ROWS = 256
SM_WP = -(-SM_W // INPROJ_TN) * INPROJ_TN

MOD_SH1, MOD_SC1, MOD_GT1, MOD_SH2, MOD_SC2, MOD_GT2 = range(6)

GLA_CHUNK = 256
GLA_SUB = 8
LOG2E = 1.4426950408889634
GLA_BLOCK = 256
GLA_HB = 4
DSA_TQ = 256
DSA_HB = 8
DSA_PAD = 16
INT_MIN = -2 ** 31
NEG_INF_KEY = 0x807FFFFF - 2 ** 32
NEG_BIG = -1e30

VMEM_LIMIT = 58 * 1024 * 1024


def _cparams(sem):
    return pltpu.CompilerParams(dimension_semantics=sem, vmem_limit_bytes=VMEM_LIMIT)


def _rms_mod(x, g, sc, sh):
    ms = jnp.mean(x * x, axis=-1, keepdims=True)
    return (x * lax.rsqrt(ms + EPS) * g) * (1.0 + sc) + sh


def _mod_kernel(ct_ref, w_ref, b_ref, o_ref, *, batch):
    ct = ct_ref[...]
    ct = ct * jax.nn.sigmoid(ct)
    d, tn = w_ref.shape[1], w_ref.shape[2]
    o_ref[...] = jnp.zeros_like(o_ref)
    for b in range(batch):
        cb = jnp.broadcast_to(ct[:, b:b + 1], (d, LANE))
        for t in range(tn // LANE):
            seg = jnp.sum(w_ref[0, :, t * LANE:(t + 1) * LANE] * cb, axis=0, keepdims=True)
            o_ref[0, b:b + 1, t * LANE:(t + 1) * LANE] = seg + b_ref[0, :, t * LANE:(t + 1) * LANE]


def _modulation(c, w_mod, b_mod):
    depth, d, n = w_mod.shape
    batch = c.shape[0]
    tn = 1024
    ct = jnp.zeros((d, LANE), F32).at[:, :batch].set(c.T)
    return pl.pallas_call(
        functools.partial(_mod_kernel, batch=batch),
        grid=(depth, n // tn),
        in_specs=[pl.BlockSpec((d, LANE), lambda l, j: (0, 0)),
                  pl.BlockSpec((1, d, tn), lambda l, j: (l, 0, j)),
                  pl.BlockSpec((1, 1, tn), lambda l, j: (l, 0, j))],
        out_specs=pl.BlockSpec((1, 8, tn), lambda l, j: (l, 0, j)),
        out_shape=jax.ShapeDtypeStruct((depth, 8, n), F32),
        compiler_params=_cparams(("arbitrary", "arbitrary")),
        name="adaln_mod",
    )(ct, w_mod, b_mod.reshape(depth, 1, n))


def _norm_mm_kernel(x_ref, mod_ref, g_ref, w_ref, big_ref, small_ref, h_ref, *, n_big):
    j = pl.program_id(1)

    @pl.when(j == 0)
    def _():
        rows = min(512, x_ref.shape[0])
        for r0 in range(0, x_ref.shape[0], rows):
            h = _rms_mod(x_ref[r0:r0 + rows, :], g_ref[...], mod_ref[0, MOD_SC1:MOD_SC1 + 1, :],
                         mod_ref[0, MOD_SH1:MOD_SH1 + 1, :])
            h_ref[r0:r0 + rows, :] = h.astype(BF16)

    rows = min(INPROJ_ROWS, h_ref.shape[0])

    @pl.when(j < n_big)
    def _():
        for r0 in range(0, h_ref.shape[0], rows):
            y = jnp.dot(h_ref[r0:r0 + rows, :], w_ref[0], preferred_element_type=F32)
            big_ref[r0:r0 + rows, :] = y.astype(big_ref.dtype)

    @pl.when(j >= n_big)
    def _():
        for r0 in range(0, h_ref.shape[0], rows):
            small_ref[r0:r0 + rows, :] = jnp.dot(h_ref[r0:r0 + rows, :], w_ref[0], preferred_element_type=F32)


def _norm_matmul(x, mod, g, w, layer, seq, tm, tn):
    m, d = x.shape
    n_big, n_small = BIG_W // tn, SM_WP // tn
    tpb = seq // tm
    return pl.pallas_call(
        functools.partial(_norm_mm_kernel, n_big=n_big),
        grid=(m // tm, n_big + n_small),
        in_specs=[pl.BlockSpec((tm, d), lambda i, j: (i, 0), pipeline_mode=pl.Buffered(1)),
                  pl.BlockSpec((1, N_MOD, d), lambda i, j: (i // tpb, 0, 0)),
                  pl.BlockSpec((1, d), lambda i, j: (0, 0)),
                  pl.BlockSpec((1, d, tn), lambda i, j: (layer, 0, j))],
        out_specs=[pl.BlockSpec((tm, tn), lambda i, j: (i, jnp.minimum(j, n_big - 1))),
                   pl.BlockSpec((tm, tn), lambda i, j: (i, jnp.maximum(j - n_big, 0)))],
        out_shape=[jax.ShapeDtypeStruct((m, BIG_W), BF16), jax.ShapeDtypeStruct((m, SM_WP), F32)],
        scratch_shapes=[pltpu.VMEM((tm, d), BF16)],
        compiler_params=_cparams(("arbitrary", "arbitrary")),
        name="norm_inproj",
    )(x, mod, g, w)


def _gla_kernel(q_ref, k_ref, v_ref, r_ref, misc_ref, wa_ref, ba_ref, gn_ref, o_ref,
                s_all_ref, a_all_ref, *, block, chunk):
    dk, dv, sub = GLA_HEAD_K, GLA_HEAD_V, GLA_SUB
    nsub = chunk // sub

    @pl.when(pl.program_id(2) == 0)
    def _():
        s_all_ref[...] = jnp.zeros_like(s_all_ref)

    a_all_ref[...] = jnp.zeros_like(a_all_ref)

    a_low = misc_ref[:, MISC_ALOW:MISC_ALOW + GLA_GATE_RANK].astype(BF16)
    xg = jnp.dot(a_low, wa_ref[...].astype(BF16), preferred_element_type=F32) + ba_ref[...]
    g_all = (jnp.minimum(xg, 0.0) - jnp.log1p(jnp.exp(-jnp.abs(xg)))) * (LOG2E / GLA_GATE_NORMALIZER)

    r_i = lax.broadcasted_iota(jnp.int32, (chunk, chunk), 0)
    c_i = lax.broadcasted_iota(jnp.int32, (chunk, chunk), 1)
    tril = (r_i >= c_i).astype(BF16)
    row_in_chunk = lax.broadcasted_iota(jnp.int32, (chunk, dk), 0)
    lane3 = lax.broadcasted_iota(jnp.int32, (nsub, sub, sub), 2)
    row3 = lax.broadcasted_iota(jnp.int32, (nsub, sub, sub), 1)
    gn = gn_ref[...]

    for c, hh in [(c, hh) for c in range(block // chunk) for hh in range(GLA_HB)]:
        lo = c * chunk
        s_ref, a_ref = s_all_ref.at[hh], a_all_ref.at[hh]
        qf = q_ref[lo:lo + chunk, hh * dk:(hh + 1) * dk].astype(F32) * (dk ** -0.5)
        kf = k_ref[lo:lo + chunk, hh * dk:(hh + 1) * dk].astype(F32)
        vb = v_ref[lo:lo + chunk, hh * dv:(hh + 1) * dv]
        g = g_all[lo:lo + chunk, hh * dk:(hh + 1) * dk]

        g_hi = g.astype(BF16)
        r1 = g - g_hi.astype(F32)
        g_mid = r1.astype(BF16)
        g_lo = (r1 - g_mid.astype(F32)).astype(BF16)
        b = (jnp.dot(tril, g_hi, preferred_element_type=F32)
             + jnp.dot(tril, g_mid, preferred_element_type=F32)
             + jnp.dot(tril, g_lo, preferred_element_type=F32))

        state = s_ref[...]
        o = jnp.dot((qf * jnp.exp2(b)).astype(BF16), state.astype(BF16), preferred_element_type=F32)

        m = chunk // 2
        while m >= sub:
            nseg = chunk // (2 * m)
            pieces = []
            for s in range(nseg):
                ref_row = b[s * 2 * m + m - 1:s * 2 * m + m, :]
                pieces.append(jnp.broadcast_to(ref_row, (2 * m, dk)))
            ref_full = pieces[0] if nseg == 1 else jnp.concatenate(pieces, axis=0)
            fac = jnp.exp2(-jnp.abs(b - ref_full))
            is_q = (row_in_chunk & m) != 0
            xs = jnp.where(is_q, qf, kf) * fac
            for s in range(nseg):
                base = s * 2 * m
                blk = lax.dot_general(xs[base + m:base + 2 * m, :].astype(BF16), xs[base:base + m, :].astype(BF16),
                                      (((1,), (1,)), ((), ())), preferred_element_type=F32)
                a_ref[base + m:base + 2 * m, base:base + m] = blk
            m //= 2

        q4 = qf.reshape(nsub, sub, dk)
        k4 = kf.reshape(nsub, sub, dk)
        b4 = b.reshape(nsub, sub, dk)
        ad = jnp.zeros((nsub, sub, sub), F32)
        for j in range(sub):
            e = jnp.exp2(b4 - b4[:, j:j + 1, :])
            col = jnp.sum(q4 * e * k4[:, j:j + 1, :], axis=-1, keepdims=True)
            ad = jnp.where(lane3 == j, col, ad)
        ad = jnp.where(row3 >= lane3, ad, 0.0)
        for i in range(nsub):
            a_ref[i * sub:(i + 1) * sub, i * sub:(i + 1) * sub] = ad[i]

        o = o + jnp.dot(a_ref[...].astype(BF16), vb, preferred_element_type=F32)

        b_last = b[chunk - 1:chunk, :]
        kd = (kf * jnp.exp2(b_last - b)).astype(BF16)
        upd = lax.dot_general(kd, vb, (((0,), (0,)), ((), ())), preferred_element_type=F32)
        decay_col = jnp.transpose(jnp.broadcast_to(jnp.exp2(b_last), (LANE, dk)))[:, 0:1]
        s_ref[...] = decay_col * state + upd

        o = o * lax.rsqrt(jnp.mean(o * o, axis=-1, keepdims=True) + EPS) * gn
        r = r_ref[lo:lo + chunk, hh * dv:(hh + 1) * dv].astype(F32)
        o_ref[lo:lo + chunk, hh * dv:(hh + 1) * dv] = (o * (r * jax.nn.sigmoid(r))).astype(o_ref.dtype)


def _gla(z_big, z_small, w_gate_up, b_gate, gla_norm_g, batch, seq):
    blk = min(GLA_BLOCK, seq)
    nb = seq // blk
    dk, dv = GLA_HB * GLA_HEAD_K, GLA_HB * GLA_HEAD_V
    row = lambda b, h, t: b * nb + t
    kern = functools.partial(_gla_kernel, block=blk, chunk=GLA_CHUNK)
    return pl.pallas_call(
        kern,
        grid=(batch, GLA_HEADS // GLA_HB, nb),
        in_specs=[pl.BlockSpec((blk, dk), lambda b, h, t: (row(b, h, t), BIG_QG // dk + h)),
                  pl.BlockSpec((blk, dk), lambda b, h, t: (row(b, h, t), BIG_KG // dk + h)),
                  pl.BlockSpec((blk, dv), lambda b, h, t: (row(b, h, t), BIG_VG // dv + h)),
                  pl.BlockSpec((blk, dv), lambda b, h, t: (row(b, h, t), BIG_RG // dv + h)),
                  pl.BlockSpec((blk, LANE), lambda b, h, t: (row(b, h, t), SM_MISC // LANE)),
                  pl.BlockSpec((GLA_GATE_RANK, dk), lambda b, h, t: (0, h)),
                  pl.BlockSpec((1, dk), lambda b, h, t: (0, h)),
                  pl.BlockSpec((1, GLA_HEAD_V), lambda b, h, t: (0, 0))],
        out_specs=pl.BlockSpec((blk, dv), lambda b, h, t: (row(b, h, t), h)),
        out_shape=jax.ShapeDtypeStruct((batch * seq, GLA_V_W), BF16),
        scratch_shapes=[pltpu.VMEM((GLA_HB, GLA_HEAD_K, GLA_HEAD_V), F32),
                        pltpu.VMEM((GLA_HB, GLA_CHUNK, GLA_CHUNK), F32)],
        compiler_params=_cparams(("arbitrary", "arbitrary", "arbitrary")),
        name="gla",
    )(z_big, z_big, z_big, z_big, z_small, w_gate_up, b_gate.reshape(1, -1), gla_norm_g.reshape(1, -1))


def _rel_bucket_np(d):
    max_exact = REL_BUCKETS // 2
    d = np.maximum(d, 0)
    df = np.maximum(d, 1).astype(np.float32)
    large = max_exact + (np.log(df / np.float32(max_exact)) / np.float32(math.log(REL_MAX_DIST / max_exact))
                         * np.float32(REL_BUCKETS - max_exact)).astype(np.int32)
    large = np.minimum(large, REL_BUCKETS - 1)
    return np.where(d < max_exact, d, large).astype(np.int32)


def _bias_kernel(rb_ref, bucket_ref, o_ref):
    h = pl.program_id(0)
    far = rb_ref[REL_BUCKETS - 1, h]
    for u in range(2):
        bk = bucket_ref[u]
        acc = jnp.zeros(bk.shape, F32)
        for bb in range(REL_BUCKETS):
            acc = jnp.where(bk == bb, rb_ref[bb, h], acc)
        o_ref[0, u] = acc - far


def _bias_tiles(rel_bias):
    j = np.arange(LANE)[:, None]
    i = np.arange(LANE)[None, :]
    bucket = np.stack([_rel_bucket_np(i - j), _rel_bucket_np(LANE + i - j)]).astype(np.int32)
    assert int(_rel_bucket_np(np.array([LANE]))[0]) == REL_BUCKETS - 1
    return pl.pallas_call(
        _bias_kernel,
        grid=(DSA_HEADS,),
        in_specs=[pl.BlockSpec(memory_space=pltpu.SMEM),
                  pl.BlockSpec((2, LANE, LANE), lambda h: (0, 0, 0))],
        out_specs=pl.BlockSpec((1, 2, LANE, LANE), lambda h: (h, 0, 0, 0)),
        out_shape=jax.ShapeDtypeStruct((DSA_HEADS, 2, LANE, LANE), F32),
        compiler_params=_cparams(("arbitrary",)),
        name="t5_bias_tiles",
    )(rel_bias, jnp.asarray(bucket))


def _key_to_float(key):
    return pltpu.bitcast(key ^ ((key >> 31) & 0x7FFFFFFF), F32)


def _dsa_kernel(q_ref, iq_ref, kv_ref, misc_all_ref, misc_q_ref, kvg_ref, wuv_ref, bias_ref, w1_ref, w2_ref,
                o_ref, w1b_ref, w2b_ref, kvn_ref, kvt_ref, ikb_ref, iqh_ref, sc_ref, hi_ref, qs_ref, lga_ref, lgb_ref, acc_ref, *, seq, k_sel):
    tq, hb, lat = DSA_TQ, DSA_HB, DSA_LATENT
    qi = pl.program_id(1)
    hg = pl.program_id(2)
    key_i = lax.broadcasted_iota(jnp.int32, (tq, tq), 0)
    qry_i = lax.broadcasted_iota(jnp.int32, (tq, tq), 1)
    causal = key_i <= qry_i
    nt = (((1,), (1,)), ((), ()))

    w1b_ref[...] = w1_ref[0].astype(BF16)
    w2b_ref[...] = w2_ref[0].astype(BF16)

    @pl.when((qi == 0) & (hg == 0))
    def _():
        kv = kv_ref[...]
        ms = jnp.mean(kv * kv, axis=-1, keepdims=True)
        kvn = kv * lax.rsqrt(ms + EPS) * kvg_ref[...]
        kvn_ref[...] = kvn.astype(BF16)
        ones_rows = (lax.broadcasted_iota(jnp.int32, (DSA_PAD, tq), 0) == 0).astype(BF16)
        for t in range(seq // tq):
            kvt_ref[t, 0:lat, :] = jnp.transpose(kvn[t * tq:(t + 1) * tq, :]).astype(BF16)
            kvt_ref[t, lat:lat + DSA_PAD, :] = ones_rows
        ikb_ref[...] = misc_all_ref[:, MISC_IK:MISC_IK + IDX_DIM].astype(BF16)

    @pl.when(hg == 0)
    def _():
        for h in range(IDX_HEADS):
            iqh_ref[h] = (iq_ref[:, h * IDX_DIM:(h + 1) * IDX_DIM] * (IDX_DIM ** -0.5)).astype(BF16)
        iw_t = jnp.transpose(misc_q_ref[...]) * (IDX_HEADS ** -0.5)

        def score_tile(kt):
            off = pl.multiple_of(kt * tq, tq)
            ik_t = ikb_ref[pl.ds(off, tq), :]
            sc = jnp.zeros((tq, tq), F32)
            for h in range(IDX_HEADS):
                y = lax.dot_general(ik_t, iqh_ref[h], nt, preferred_element_type=F32)
                sc = sc + jnp.maximum(y, 0.0) * iw_t[MISC_IW + h:MISC_IW + h + 1, :]
            return sc

        def put_scores(kt, sc):
            sc_ref[kt] = sc
            bits = pltpu.bitcast(sc, jnp.int32) & jnp.int32(-(1 << 16))
            hi_ref[kt] = pltpu.bitcast(bits, F32).astype(BF16)

        def far_body(kt, carry):
            put_scores(kt, score_tile(kt))
            return carry

        lax.fori_loop(0, qi, far_body, 0)
        put_scores(qi, jnp.where(causal, score_tile(qi), -jnp.inf))

        def count(pred):
            def body(kt, cnt):
                hit = jnp.where(pred(sc_ref[kt]), 1, 0)
                return cnt + jnp.sum(hit.reshape(tq // 8, 8, tq), axis=0)

            return jnp.sum(lax.fori_loop(0, qi + 1, body, jnp.zeros((8, tq), jnp.int32)), axis=0, keepdims=True)

        def bit_body(it, state):
            key, n_ge = state
            cand_key = key + (jnp.int32(1) << (31 - it))
            cand = _key_to_float(cand_key)
            tot = count(lambda sc: sc >= cand)
            take = (tot >= k_sel) | (cand_key <= NEG_INF_KEY)
            return jnp.where(take, cand_key, key), jnp.where(take, tot, n_ge)

        one_h, zero_h = jnp.ones((tq, tq), BF16), jnp.zeros((tq, tq), BF16)

        def hi_body(it, state):
            key, n_ge = state
            cand_key = key + (jnp.int32(1) << (31 - it))
            cand = pltpu.bitcast(cand_key ^ ((cand_key >> 31) & 0x7FFF0000), F32).astype(BF16)

            def body(kt, cnt):
                hit = jnp.where(hi_ref[kt] >= cand, one_h, zero_h).reshape(tq // 16, 16, tq)
                for g in range(tq // 16):
                    cnt = cnt + hit[g]
                return cnt

            cnt = lax.fori_loop(0, qi + 1, body, jnp.zeros((16, tq), BF16))
            tot = jnp.sum(cnt.astype(F32), axis=0, keepdims=True).astype(jnp.int32)
            take = (tot >= k_sel) | (cand_key <= (NEG_INF_KEY & -(1 << 16)))
            return jnp.where(take, cand_key, key), jnp.where(take, tot, n_ge)

        state = (jnp.full((1, tq), INT_MIN, jnp.int32), jnp.full((1, tq), k_sel, jnp.int32))
        state = lax.fori_loop(0, 16, hi_body, state)
        key, n_ge = lax.fori_loop(16, 32, bit_body, state)
        thr = _key_to_float(key)

        def select_all(cr):
            def body(kt, c):
                sc_ref[kt] = jnp.where(sc_ref[kt] >= thr, 0.0, NEG_BIG)
                return c

            lax.fori_loop(0, qi, body, 0)
            sc_ref[qi] = jnp.where(causal & (sc_ref[qi] >= thr), 0.0, NEG_BIG)
            return cr

        def select_ranked(cr):
            need = (k_sel - count(lambda sc: sc > thr)).astype(F32)
            tri = (qry_i <= key_i).astype(BF16)

            def body(kt, ties_before, extra=None):
                sc = sc_ref[kt]
                tie = sc == thr
                rank = ties_before + jnp.dot(tri, jnp.where(tie, 1.0, 0.0).astype(BF16),
                                             preferred_element_type=F32)
                keep = (sc > thr) | (tie & (rank <= need))
                if extra is not None:
                    keep = keep & extra
                sc_ref[kt] = jnp.where(keep, 0.0, NEG_BIG)
                return rank[tq - 1:tq, :]

            body(qi, lax.fori_loop(0, qi, body, jnp.zeros((1, tq), F32)), extra=causal)
            return cr

        lax.cond(jnp.max(n_ge) > k_sel, select_ranked, select_all, 0)

    zero_blk = jnp.zeros((LANE, LANE), F32)

    for hh in range(hb):
        qs_ref[hh] = q_ref[:, hh * lat:(hh + 1) * lat] * (lat ** -0.5)
    acc_ref[...] = jnp.zeros_like(acc_ref)

    def logits(kt, buf):
        off = pl.multiple_of(kt * tq, tq)
        kv_t = kvn_ref[pl.ds(off, tq), :]
        for hh in range(hb):
            buf[hh] = lax.dot_general(kv_t, qs_ref[hh], nt, preferred_element_type=F32)

    def tile(kt, carry, buf, near=None):
        kv_tt = kvt_ref[kt]
        mask = sc_ref[kt]
        out = []
        for hh in range(hb):
            m_run = carry[hh]
            s = buf[hh] + mask
            if near == "prev":
                s = s + jnp.concatenate([jnp.concatenate([zero_blk, zero_blk], axis=1),
                                         jnp.concatenate([bias_ref[hh, 1], zero_blk], axis=1)], axis=0)
            elif near == "diag":
                s = s + jnp.concatenate([jnp.concatenate([bias_ref[hh, 0], bias_ref[hh, 1]], axis=1),
                                         jnp.concatenate([zero_blk, bias_ref[hh, 0]], axis=1)], axis=0)
            m_new = jnp.maximum(m_run, jnp.max(s, axis=0, keepdims=True))
            alpha = jnp.exp(m_run - m_new)
            p = jnp.exp(s - m_new)
            acc_ref[hh] = alpha * acc_ref[hh] + jnp.dot(kv_tt, p.astype(BF16), preferred_element_type=F32)
            out.append(m_new)
        return tuple(out)

    carry = tuple(jnp.full((1, tq), NEG_BIG, F32) for _ in range(hb))
    n_far = jnp.maximum(qi - 1, 0)

    def diag_and_prev(cr):
        logits(qi - 1, lgb_ref)
        cr = tile(qi, cr, lga_ref, near="diag")
        logits(jnp.maximum(qi - 2, 0), lga_ref)
        return tile(qi - 1, cr, lgb_ref, near="prev")

    def far_pair(j, cr):
        kt_a = qi - 2 - 2 * j
        kt_b = kt_a - 1
        logits(kt_b, lgb_ref)
        cr = tile(kt_a, cr, lga_ref)
        logits(jnp.maximum(kt_b - 1, 0), lga_ref)
        return tile(kt_b, cr, lgb_ref)

    logits(qi, lga_ref)
    carry = lax.cond(qi > 0, diag_and_prev, lambda cr: tile(qi, cr, lga_ref, near="diag"), carry)
    carry = lax.fori_loop(0, n_far // 2, far_pair, carry)
    carry = lax.cond(n_far % 2 == 1, lambda cr: tile(jnp.int32(0), cr, lga_ref), lambda cr: cr, carry)

    for hh in range(hb):
        o_lat_t = (acc_ref[hh, 0:lat, :] / acc_ref[hh, lat:lat + 1, :]).astype(BF16)
        y_t = lax.dot_general(wuv_ref[hh], o_lat_t, (((0,), (0,)), ((), ())), preferred_element_type=F32)
        o_ref[:, hh * DSA_HEAD_DIM:(hh + 1) * DSA_HEAD_DIM] = jnp.transpose(y_t).astype(o_ref.dtype)


def _dsa(z_big, z_small, kv_norm_g, w_uv_bf16, bias_tiles, w_ff1, w_ff2, layer, batch, seq):
    tq, hb, lat = DSA_TQ, DSA_HB, DSA_LATENT
    nq, ng = seq // tq, DSA_HEADS // hb
    k_sel = min(DSA_TOPK, seq // 4)
    assert nq * 16 <= 256, "per-slot bf16 hit counts must stay exactly representable"
    d, f = w_ff1.shape[1], w_ff1.shape[2]
    slab = f // (batch * nq * ng)
    assert slab * batch * nq * ng == f and slab % LANE == 0
    step = lambda b, i, g: (b * nq + i) * ng + g
    kern = functools.partial(_dsa_kernel, seq=seq, k_sel=k_sel)
    return pl.pallas_call(
        kern,
        grid=(batch, nq, ng),
        in_specs=[pl.BlockSpec((tq, hb * lat), lambda b, i, g: (b * nq + i, BIG_QLAT // (hb * lat) + g)),
                  pl.BlockSpec((tq, IDX_HEADS * IDX_DIM), lambda b, i, g: (b * nq + i, 0)),
                  pl.BlockSpec((seq, lat), lambda b, i, g: (b, SM_KV // lat)),
                  pl.BlockSpec((seq, LANE), lambda b, i, g: (b, SM_MISC // LANE)),
                  pl.BlockSpec((tq, LANE), lambda b, i, g: (b * nq + i, SM_MISC // LANE)),
                  pl.BlockSpec((1, lat), lambda b, i, g: (0, 0)),
                  pl.BlockSpec((hb, lat, DSA_HEAD_DIM), lambda b, i, g: (g, 0, 0)),
                  pl.BlockSpec((hb, 2, LANE, LANE), lambda b, i, g: (g, 0, 0, 0)),
                  pl.BlockSpec((1, d, slab), lambda b, i, g: (layer, 0, step(b, i, g))),
                  pl.BlockSpec((1, slab, d), lambda b, i, g: (layer, step(b, i, g), 0))],
        out_specs=[pl.BlockSpec((tq, hb * DSA_HEAD_DIM), lambda b, i, g: (b * nq + i, g)),
                   pl.BlockSpec((d, slab), lambda b, i, g: (0, step(b, i, g))),
                   pl.BlockSpec((slab, d), lambda b, i, g: (step(b, i, g), 0))],
        out_shape=[jax.ShapeDtypeStruct((batch * seq, DSA_HEADS * DSA_HEAD_DIM), BF16),
                   jax.ShapeDtypeStruct((d, f), BF16), jax.ShapeDtypeStruct((f, d), BF16)],
        scratch_shapes=[pltpu.VMEM((seq, lat), BF16),
                        pltpu.VMEM((nq, lat + DSA_PAD, tq), BF16),
                        pltpu.VMEM((seq, IDX_DIM), BF16),
                        pltpu.VMEM((IDX_HEADS, tq, IDX_DIM), BF16),
                        pltpu.VMEM((nq, tq, tq), F32),
                        pltpu.VMEM((nq, tq, tq), BF16),
                        pltpu.VMEM((hb, tq, lat), BF16),
                        pltpu.VMEM((hb, tq, tq), F32),
                        pltpu.VMEM((hb, tq, tq), F32),
                        pltpu.VMEM((hb, lat + DSA_PAD, tq), F32)],
        compiler_params=_cparams(("arbitrary", "arbitrary", "arbitrary")),
        name="dsa",
    )(z_big, z_small, z_small, z_small, z_small, kv_norm_g.reshape(1, -1), w_uv_bf16, bias_tiles, w_ff1, w_ff2)


def _merge_mm_kernel(gg_ref, gd_ref, yg_ref, yd_ref, w_ref, x_ref, mod_ref, o_ref, wb_ref):
    @pl.when(pl.program_id(0) == 0)
    def _():
        wb_ref[...] = w_ref[0].astype(BF16)

    merged = (jax.nn.sigmoid(gg_ref[...].astype(F32)) * yg_ref[...].astype(F32)
              + jax.nn.sigmoid(gd_ref[...].astype(F32)) * yd_ref[...].astype(F32))
    y = jnp.dot(merged.astype(BF16), wb_ref[...], preferred_element_type=F32)
    o_ref[...] = x_ref[...] + mod_ref[0, MOD_GT1:MOD_GT1 + 1, :] * y


def _merge_outproj(z_big, y_gla, y_dsa, w_out, layer, x, mod, seq, tm):
    m, d = x.shape
    tpb = seq // tm
    return pl.pallas_call(
        _merge_mm_kernel,
        grid=(m // tm,),
        in_specs=[pl.BlockSpec((tm, d), lambda i: (i, BIG_GGLA // d)),
                  pl.BlockSpec((tm, d), lambda i: (i, BIG_GDSA // d)),
                  pl.BlockSpec((tm, d), lambda i: (i, 0)),
                  pl.BlockSpec((tm, d), lambda i: (i, 0)),
                  pl.BlockSpec((1, d, d), lambda i: (layer, 0, 0), pipeline_mode=pl.Buffered(1)),
                  pl.BlockSpec((tm, d), lambda i: (i, 0)),
                  pl.BlockSpec((1, N_MOD, d), lambda i: (i // tpb, 0, 0))],
        out_specs=pl.BlockSpec((tm, d), lambda i: (i, 0)),
        out_shape=jax.ShapeDtypeStruct((m, d), F32),
        scratch_shapes=[pltpu.VMEM((d, d), BF16)],
        compiler_params=_cparams(("arbitrary",)),
        name="merge_outproj",
    )(z_big, z_big, y_gla, y_dsa, w_out, x, mod)


def _ffn_kernel(x_ref, mod_ref, g_ref, w1_ref, w2_ref, fg_ref, o_ref, h_ref, *, final):
    j = pl.program_id(1)

    @pl.when(j == 0)
    def _():
        h = _rms_mod(x_ref[...], g_ref[0], mod_ref[0, MOD_SC2:MOD_SC2 + 1, :],
                     mod_ref[0, MOD_SH2:MOD_SH2 + 1, :])
        h_ref[...] = h.astype(BF16)
        o_ref[...] = jnp.zeros_like(o_ref)

    u = jnp.dot(h_ref[...], w1_ref[...], preferred_element_type=F32)
    u = jnp.square(jnp.maximum(u, 0.0)).astype(BF16)
    tf = u.shape[1]
    for n0 in range(0, o_ref.shape[1], tf):
        o_ref[:, n0:n0 + tf] += jnp.dot(u, w2_ref[:, n0:n0 + tf], preferred_element_type=F32)

    @pl.when(j == pl.num_programs(1) - 1)
    def _():
        out = x_ref[...] + mod_ref[0, MOD_GT2:MOD_GT2 + 1, :] * o_ref[...]
        if final:
            out = out * lax.rsqrt(jnp.mean(out * out, axis=-1, keepdims=True) + EPS) * fg_ref[...]
        o_ref[...] = out


def _ffn(x, mod, norm2_g, w1, w2, layer, final_g, seq, tm, tf, final):
    m, d = x.shape
    f = w1.shape[1]
    tpb = seq // tm
    return pl.pallas_call(
        functools.partial(_ffn_kernel, final=final),
        grid=(m // tm, f // tf),
        in_specs=[pl.BlockSpec((tm, d), lambda i, j: (i, 0), pipeline_mode=pl.Buffered(1)),
                  pl.BlockSpec((1, N_MOD, d), lambda i, j: (i // tpb, 0, 0)),
                  pl.BlockSpec((1, 1, d), lambda i, j: (layer, 0, 0)),
                  pl.BlockSpec((d, tf), lambda i, j: (0, j)),
                  pl.BlockSpec((tf, d), lambda i, j: (j, 0)),
                  pl.BlockSpec((1, d), lambda i, j: (0, 0))],
        out_specs=pl.BlockSpec((tm, d), lambda i, j: (i, 0)),
        out_shape=jax.ShapeDtypeStruct((m, d), F32),
        scratch_shapes=[pltpu.VMEM((tm, d), BF16)],
        compiler_params=_cparams(("arbitrary", "arbitrary")),
        name="ffn",
    )(x, mod, norm2_g.reshape(norm2_g.shape[0], 1, d), w1, w2, final_g)


def _relayout_tables():
    src = dict(zip(("q_g", "k_g", "v_g", "a_low", "r_g", "q_lat", "kv_lat", "iq", "ik", "iw", "g_gla", "g_dsa"),
                   np.concatenate([[0], np.cumsum(IN_SIZES)[:-1]])))
    order = (("q_lat", DSA_Q_W), ("q_g", GLA_QK_W), ("k_g", GLA_QK_W), ("v_g", GLA_V_W), ("r_g", GLA_V_W),
             ("g_gla", D_MODEL), ("g_dsa", D_MODEL), ("iq", IDX_HEADS * IDX_DIM), ("kv_lat", INPROJ_TN))
    starts = np.array([src[name] + c for name, width in order for c in range(0, width, INPROJ_TN)], np.int32)
    assert len(starts) * INPROJ_TN == BIG_W + SM_WP and not (starts % 8).any()
    assert src["ik"] + IDX_DIM == src["iw"] and src["ik"] % 8 == 0 and src["a_low"] % 8 == 0
    return starts, int(src["ik"]), int(src["a_low"])


def _relayout_kernel(start_ref, w_ref, ikw_ref, al_ref, o_ref):
    del start_ref
    j = pl.program_id(1)
    d = w_ref.shape[2]
    x = w_ref[0]
    ikw = ikw_ref[0]
    tail = jnp.zeros((INPROJ_TN - DSA_LATENT - MISC_IW - IDX_HEADS, d), F32)
    misc = jnp.concatenate([ikw[:IDX_DIM], al_ref[0], ikw[IDX_DIM:], tail], axis=0)
    upper = jnp.where(j == pl.num_programs(1) - 1, misc, x[DSA_LATENT:])
    o_ref[0, :, :DSA_LATENT] = jnp.transpose(x[:DSA_LATENT]).astype(o_ref.dtype)
    o_ref[0, :, DSA_LATENT:] = jnp.transpose(upper).astype(o_ref.dtype)


def _relayout_w_in(w_in):
    depth, d, _ = w_in.shape
    starts, ik_row, al_row = _relayout_tables()
    w_t = jnp.swapaxes(w_in, 1, 2)
    grid_spec = pltpu.PrefetchScalarGridSpec(
        num_scalar_prefetch=1,
        grid=(depth, len(starts)),
        in_specs=[pl.BlockSpec((pl.Element(1), pl.Element(INPROJ_TN), pl.Element(d)),
                               lambda l, j, st: (l, st[j] * 8, 0)),
                  pl.BlockSpec((pl.Element(1), pl.Element(IDX_DIM + IDX_HEADS), pl.Element(d)),
                               lambda l, j, st: (l, ik_row, 0)),
                  pl.BlockSpec((pl.Element(1), pl.Element(GLA_GATE_RANK), pl.Element(d)),
                               lambda l, j, st: (l, al_row, 0))],
        out_specs=pl.BlockSpec((1, d, INPROJ_TN), lambda l, j, st: (l, 0, j)))
    return pl.pallas_call(
        _relayout_kernel,
        grid_spec=grid_spec,
        out_shape=jax.ShapeDtypeStruct((depth, d, BIG_W + SM_WP), BF16),
        compiler_params=_cparams(("arbitrary", "arbitrary")),
        name="w_in_relayout",
    )(jnp.asarray(starts // 8), w_t, w_t, w_t)


def kernel(x, c, w_mod, b_mod, norm1_g, w_in, w_gate_up, b_gate, gla_norm_g, kv_norm_g,
           w_uv, w_out, norm2_g, w_ff1, w_ff2, rel_bias, final_g):
    batch, seq, d = x.shape
    depth = w_mod.shape[0]
    m = batch * seq
    tm = min(1024, seq)

    mods = _modulation(c, w_mod, b_mod).reshape(depth, 8, N_MOD, d)
    bias_tiles = _bias_tiles(rel_bias)
    w_in_all = _relayout_w_in(w_in)
    xf = x.reshape(m, d)
    fg = final_g.reshape(1, d)

    for l in range(depth):
        mod = mods[l]
        z_big, z_small = _norm_matmul(xf, mod, norm1_g[l].reshape(1, d), w_in_all, l, seq, min(2048, seq), INPROJ_TN)
        y_gla = _gla(z_big, z_small, w_gate_up[l], b_gate[l], gla_norm_g[l], batch, seq)
        y_dsa, w1_bf, w2_bf = _dsa(z_big, z_small, kv_norm_g[l], w_uv[l].astype(BF16), bias_tiles,
                                   w_ff1, w_ff2, l, batch, seq)
        xf = _merge_outproj(z_big, y_gla, y_dsa, w_out, l, xf, mod, seq, min(256, seq))
        xf = _ffn(xf, mod, norm2_g, w1_bf, w2_bf, l, fg, seq, tm, 1024, final=(l == depth - 1))
    return xf.reshape(batch, seq, d)
```

```python
import functools
import math

import numpy as np
import jax
import jax.numpy as jnp
from jax import lax
from jax.experimental import pallas as pl
from jax.experimental.pallas import tpu as pltpu

F32 = jnp.float32
BF16 = jnp.bfloat16

D_MODEL = 2048
GLA_HEADS = 4
GLA_HEAD_K = 256
GLA_HEAD_V = 512
GLA_GATE_RANK = 16
GLA_GATE_NORMALIZER = 16.0
DSA_HEADS = 16
DSA_HEAD_DIM = 128
DSA_LATENT = 256
DSA_TOPK = 256
IDX_HEADS = 16
IDX_DIM = 64
REL_BUCKETS = 32
REL_MAX_DIST = 128
D_FF = 4 * D_MODEL
N_MOD = 6
EPS = 1e-6

GLA_QK_W = GLA_HEADS * GLA_HEAD_K
GLA_V_W = GLA_HEADS * GLA_HEAD_V
DSA_Q_W = DSA_HEADS * DSA_LATENT
IN_SIZES = (GLA_QK_W, GLA_QK_W, GLA_V_W, GLA_GATE_RANK, GLA_V_W,
            DSA_Q_W, DSA_LATENT, IDX_HEADS * IDX_DIM, IDX_DIM, IDX_HEADS,
            D_MODEL, D_MODEL)

BIG_QLAT = 0
BIG_QG = BIG_QLAT + DSA_Q_W
BIG_KG = BIG_QG + GLA_QK_W
BIG_VG = BIG_KG + GLA_QK_W
BIG_RG = BIG_VG + GLA_V_W
BIG_GGLA = BIG_RG + GLA_V_W
BIG_GDSA = BIG_GGLA + D_MODEL
BIG_W = BIG_GDSA + D_MODEL
SM_IQ = 0
SM_KV = SM_IQ + IDX_HEADS * IDX_DIM
SM_MISC = SM_KV + DSA_LATENT
MISC_IK = 0
MISC_ALOW = MISC_IK + IDX_DIM
MISC_IW = MISC_ALOW + GLA_GATE_RANK
LANE = 128
SM_W = SM_MISC + LANE
INPROJ_TN = 512
INPROJ_ROWS = 256
SM_WP = -(-SM_W // INPROJ_TN) * INPROJ_TN

MOD_SH1, MOD_SC1, MOD_GT1, MOD_SH2, MOD_SC2, MOD_GT2 = range(6)

GLA_CHUNK = 256
GLA_SUB = 8
LOG2E = 1.4426950408889634
GLA_BLOCK = 256
GLA_HB = 4
DSA_TQ = 256
DSA_HB = 8
DSA_PAD = 16
INT_MIN = -2 ** 31
NEG_INF_KEY = 0x807FFFFF - 2 ** 32
NEG_BIG = -1e30

VMEM_LIMIT = 58 * 1024 * 1024


def _cparams(sem):
    return pltpu.CompilerParams(dimension_semantics=sem, vmem_limit_bytes=VMEM_LIMIT)


def _rms_mod(x, g, sc, sh):
    ms = jnp.mean(x * x, axis=-1, keepdims=True)
    return (x * lax.rsqrt(ms + EPS) * g) * (1.0 + sc) + sh


def _mod_kernel(ct_ref, w_ref, b_ref, o_ref, *, batch):
    ct = ct_ref[...]
    ct = ct * jax.nn.sigmoid(ct)
    d, tn = w_ref.shape[1], w_ref.shape[2]
    o_ref[...] = jnp.zeros_like(o_ref)
    for b in range(batch):
        cb = jnp.broadcast_to(ct[:, b:b + 1], (d, LANE))
        for t in range(tn // LANE):
            seg = jnp.sum(w_ref[0, :, t * LANE:(t + 1) * LANE] * cb, axis=0, keepdims=True)
            o_ref[0, b:b + 1, t * LANE:(t + 1) * LANE] = seg + b_ref[0, :, t * LANE:(t + 1) * LANE]


def _modulation(c, w_mod, b_mod):
    depth, d, n = w_mod.shape
    batch = c.shape[0]
    tn = 1024
    ct = jnp.zeros((d, LANE), F32).at[:, :batch].set(c.T)
    return pl.pallas_call(
        functools.partial(_mod_kernel, batch=batch),
        grid=(depth, n // tn),
        in_specs=[pl.BlockSpec((d, LANE), lambda l, j: (0, 0)),
                  pl.BlockSpec((1, d, tn), lambda l, j: (l, 0, j)),
                  pl.BlockSpec((1, 1, tn), lambda l, j: (l, 0, j))],
        out_specs=pl.BlockSpec((1, 8, tn), lambda l, j: (l, 0, j)),
        out_shape=jax.ShapeDtypeStruct((depth, 8, n), F32),
        compiler_params=_cparams(("arbitrary", "arbitrary")),
        name="adaln_mod",
    )(ct, w_mod, b_mod.reshape(depth, 1, n))


def _norm_mm_kernel(x_ref, mod_ref, g_ref, w_ref, big_ref, small_ref, h_ref, *, n_big):
    j = pl.program_id(1)

    @pl.when(j == 0)
    def _():
        rows = min(512, x_ref.shape[0])
        for r0 in range(0, x_ref.shape[0], rows):
            h = _rms_mod(x_ref[r0:r0 + rows, :], g_ref[...], mod_ref[0, MOD_SC1:MOD_SC1 + 1, :],
                         mod_ref[0, MOD_SH1:MOD_SH1 + 1, :])
            h_ref[r0:r0 + rows, :] = h.astype(BF16)

    rows = min(INPROJ_ROWS, h_ref.shape[0])

    @pl.when(j < n_big)
    def _():
        for r0 in range(0, h_ref.shape[0], rows):
            y = jnp.dot(h_ref[r0:r0 + rows, :], w_ref[0], preferred_element_type=F32)
            big_ref[r0:r0 + rows, :] = y.astype(big_ref.dtype)

    @pl.when(j >= n_big)
    def _():
        for r0 in range(0, h_ref.shape[0], rows):
            small_ref[r0:r0 + rows, :] = jnp.dot(h_ref[r0:r0 + rows, :], w_ref[0], preferred_element_type=F32)


def _norm_matmul(x, mod, g, w, layer, seq, tm, tn):
    m, d = x.shape
    n_big, n_small = BIG_W // tn, SM_WP // tn
    tpb = seq // tm
    return pl.pallas_call(
        functools.partial(_norm_mm_kernel, n_big=n_big),
        grid=(m // tm, n_big + n_small),
        in_specs=[pl.BlockSpec((tm, d), lambda i, j: (i, 0), pipeline_mode=pl.Buffered(1)),
                  pl.BlockSpec((1, N_MOD, d), lambda i, j: (i // tpb, 0, 0)),
                  pl.BlockSpec((1, d), lambda i, j: (0, 0)),
                  pl.BlockSpec((1, d, tn), lambda i, j: (layer, 0, j))],
        out_specs=[pl.BlockSpec((tm, tn), lambda i, j: (i, jnp.minimum(j, n_big - 1))),
                   pl.BlockSpec((tm, tn), lambda i, j: (i, jnp.maximum(j - n_big, 0)))],
        out_shape=[jax.ShapeDtypeStruct((m, BIG_W), BF16), jax.ShapeDtypeStruct((m, SM_WP), F32)],
        scratch_shapes=[pltpu.VMEM((tm, d), BF16)],
        compiler_params=_cparams(("arbitrary", "arbitrary")),
        name="norm_inproj",
    )(x, mod, g, w)


def _gla_kernel(q_ref, k_ref, v_ref, r_ref, misc_ref, wa_ref, ba_ref, gn_ref, o_ref,
                s_all_ref, a_all_ref, *, block, chunk):
    dk, dv, sub = GLA_HEAD_K, GLA_HEAD_V, GLA_SUB
    nsub = chunk // sub

    @pl.when(pl.program_id(2) == 0)
    def _():
        s_all_ref[...] = jnp.zeros_like(s_all_ref)

    a_all_ref[...] = jnp.zeros_like(a_all_ref)

    a_low = misc_ref[:, MISC_ALOW:MISC_ALOW + GLA_GATE_RANK].astype(BF16)
    xg = jnp.dot(a_low, wa_ref[...].astype(BF16), preferred_element_type=F32) + ba_ref[...]
    g_all = (jnp.minimum(xg, 0.0) - jnp.log1p(jnp.exp(-jnp.abs(xg)))) * (LOG2E / GLA_GATE_NORMALIZER)

    r_i = lax.broadcasted_iota(jnp.int32, (chunk, chunk), 0)
    c_i = lax.broadcasted_iota(jnp.int32, (chunk, chunk), 1)
    tril = (r_i >= c_i).astype(BF16)
    row_in_chunk = lax.broadcasted_iota(jnp.int32, (chunk, dk), 0)
    lane3 = lax.broadcasted_iota(jnp.int32, (nsub, sub, sub), 2)
    row3 = lax.broadcasted_iota(jnp.int32, (nsub, sub, sub), 1)
    gn = gn_ref[...]

    for c, hh in [(c, hh) for c in range(block // chunk) for hh in range(GLA_HB)]:
        lo = c * chunk
        s_ref, a_ref = s_all_ref.at[hh], a_all_ref.at[hh]
        qf = q_ref[lo:lo + chunk, hh * dk:(hh + 1) * dk].astype(F32) * (dk ** -0.5)
        kf = k_ref[lo:lo + chunk, hh * dk:(hh + 1) * dk].astype(F32)
        vb = v_ref[lo:lo + chunk, hh * dv:(hh + 1) * dv]
        g = g_all[lo:lo + chunk, hh * dk:(hh + 1) * dk]

        g_hi = g.astype(BF16)
        r1 = g - g_hi.astype(F32)
        g_mid = r1.astype(BF16)
        g_lo = (r1 - g_mid.astype(F32)).astype(BF16)
        b = (jnp.dot(tril, g_hi, preferred_element_type=F32)
             + jnp.dot(tril, g_mid, preferred_element_type=F32)
             + jnp.dot(tril, g_lo, preferred_element_type=F32))

        state = s_ref[...]
        o = jnp.dot((qf * jnp.exp2(b)).astype(BF16), state.astype(BF16), preferred_element_type=F32)

        m = chunk // 2
        while m >= sub:
            nseg = chunk // (2 * m)
            pieces = []
            for s in range(nseg):
                ref_row = b[s * 2 * m + m - 1:s * 2 * m + m, :]
                pieces.append(jnp.broadcast_to(ref_row, (2 * m, dk)))
            ref_full = pieces[0] if nseg == 1 else jnp.concatenate(pieces, axis=0)
            fac = jnp.exp2(-jnp.abs(b - ref_full))
            is_q = (row_in_chunk & m) != 0
            xs = jnp.where(is_q, qf, kf) * fac
            for s in range(nseg):
                base = s * 2 * m
                blk = lax.dot_general(xs[base + m:base + 2 * m, :].astype(BF16), xs[base:base + m, :].astype(BF16),
                                      (((1,), (1,)), ((), ())), preferred_element_type=F32)
                a_ref[base + m:base + 2 * m, base:base + m] = blk
            m //= 2

        q4 = qf.reshape(nsub, sub, dk)
        k4 = kf.reshape(nsub, sub, dk)
        b4 = b.reshape(nsub, sub, dk)
        ad = jnp.zeros((nsub, sub, sub), F32)
        for j in range(sub):
            e = jnp.exp2(b4 - b4[:, j:j + 1, :])
            col = jnp.sum(q4 * e * k4[:, j:j + 1, :], axis=-1, keepdims=True)
            ad = jnp.where(lane3 == j, col, ad)
        ad = jnp.where(row3 >= lane3, ad, 0.0)
        for i in range(nsub):
            a_ref[i * sub:(i + 1) * sub, i * sub:(i + 1) * sub] = ad[i]

        o = o + jnp.dot(a_ref[...].astype(BF16), vb, preferred_element_type=F32)

        b_last = b[chunk - 1:chunk, :]
        kd = (kf * jnp.exp2(b_last - b)).astype(BF16)
        upd = lax.dot_general(kd, vb, (((0,), (0,)), ((), ())), preferred_element_type=F32)
        decay_col = jnp.transpose(jnp.broadcast_to(jnp.exp2(b_last), (LANE, dk)))[:, 0:1]
        s_ref[...] = decay_col * state + upd

        o = o * lax.rsqrt(jnp.mean(o * o, axis=-1, keepdims=True) + EPS) * gn
        r = r_ref[lo:lo + chunk, hh * dv:(hh + 1) * dv].astype(F32)
        o_ref[lo:lo + chunk, hh * dv:(hh + 1) * dv] = (o * (r * jax.nn.sigmoid(r))).astype(o_ref.dtype)


def _gla(z_big, z_small, w_gate_up, b_gate, gla_norm_g, batch, seq):
    blk = min(GLA_BLOCK, seq)
    nb = seq // blk
    dk, dv = GLA_HB * GLA_HEAD_K, GLA_HB * GLA_HEAD_V
    row = lambda b, h, t: b * nb + t
    kern = functools.partial(_gla_kernel, block=blk, chunk=GLA_CHUNK)
    return pl.pallas_call(
        kern,
        grid=(batch, GLA_HEADS // GLA_HB, nb),
        in_specs=[pl.BlockSpec((blk, dk), lambda b, h, t: (row(b, h, t), BIG_QG // dk + h)),
                  pl.BlockSpec((blk, dk), lambda b, h, t: (row(b, h, t), BIG_KG // dk + h)),
                  pl.BlockSpec((blk, dv), lambda b, h, t: (row(b, h, t), BIG_VG // dv + h)),
                  pl.BlockSpec((blk, dv), lambda b, h, t: (row(b, h, t), BIG_RG // dv + h)),
                  pl.BlockSpec((blk, LANE), lambda b, h, t: (row(b, h, t), SM_MISC // LANE)),
                  pl.BlockSpec((GLA_GATE_RANK, dk), lambda b, h, t: (0, h)),
                  pl.BlockSpec((1, dk), lambda b, h, t: (0, h)),
                  pl.BlockSpec((1, GLA_HEAD_V), lambda b, h, t: (0, 0))],
        out_specs=pl.BlockSpec((blk, dv), lambda b, h, t: (row(b, h, t), h)),
        out_shape=jax.ShapeDtypeStruct((batch * seq, GLA_V_W), BF16),
        scratch_shapes=[pltpu.VMEM((GLA_HB, GLA_HEAD_K, GLA_HEAD_V), F32),
                        pltpu.VMEM((GLA_HB, GLA_CHUNK, GLA_CHUNK), F32)],
        compiler_params=_cparams(("arbitrary", "arbitrary", "arbitrary")),
        name="gla",
    )(z_big, z_big, z_big, z_big, z_small, w_gate_up, b_gate.reshape(1, -1), gla_norm_g.reshape(1, -1))


def _rel_bucket_np(d):
    max_exact = REL_BUCKETS // 2
    d = np.maximum(d, 0)
    df = np.maximum(d, 1).astype(np.float32)
    large = max_exact + (np.log(df / np.float32(max_exact)) / np.float32(math.log(REL_MAX_DIST / max_exact))
                         * np.float32(REL_BUCKETS - max_exact)).astype(np.int32)
    large = np.minimum(large, REL_BUCKETS - 1)
    return np.where(d < max_exact, d, large).astype(np.int32)


def _bias_kernel(rb_ref, bucket_ref, o_ref):
    h = pl.program_id(0)
    far = rb_ref[REL_BUCKETS - 1, h]
    for u in range(2):
        bk = bucket_ref[u]
        acc = jnp.zeros(bk.shape, F32)
        for bb in range(REL_BUCKETS):
            acc = jnp.where(bk == bb, rb_ref[bb, h], acc)
        o_ref[0, u] = acc - far


def _bias_tiles(rel_bias):
    j = np.arange(LANE)[:, None]
    i = np.arange(LANE)[None, :]
    bucket = np.stack([_rel_bucket_np(i - j), _rel_bucket_np(LANE + i - j)]).astype(np.int32)
    assert int(_rel_bucket_np(np.array([LANE]))[0]) == REL_BUCKETS - 1
    return pl.pallas_call(
        _bias_kernel,
        grid=(DSA_HEADS,),
        in_specs=[pl.BlockSpec(memory_space=pltpu.SMEM),
                  pl.BlockSpec((2, LANE, LANE), lambda h: (0, 0, 0))],
        out_specs=pl.BlockSpec((1, 2, LANE, LANE), lambda h: (h, 0, 0, 0)),
        out_shape=jax.ShapeDtypeStruct((DSA_HEADS, 2, LANE, LANE), F32),
        compiler_params=_cparams(("arbitrary",)),
        name="t5_bias_tiles",
    )(rel_bias, jnp.asarray(bucket))


def _key_to_float(key):
    return pltpu.bitcast(key ^ ((key >> 31) & 0x7FFFFFFF), F32)


def _dsa_kernel(q_ref, iq_ref, kv_ref, misc_all_ref, misc_q_ref, kvg_ref, wuv_ref, bias_ref, w1_ref, w2_ref, wo_ref,
                o_ref, w1b_ref, w2b_ref, wob_ref, kvn_ref, kvt_ref, ikb_ref, iqh_ref, sc_ref, hi_ref, qs_ref, lga_ref, lgb_ref, acc_ref, *, seq, k_sel):
    tq, hb, lat = DSA_TQ, DSA_HB, DSA_LATENT
    qi = pl.program_id(1)
    hg = pl.program_id(2)
    key_i = lax.broadcasted_iota(jnp.int32, (tq, tq), 0)
    qry_i = lax.broadcasted_iota(jnp.int32, (tq, tq), 1)
    causal = key_i <= qry_i
    nt = (((1,), (1,)), ((), ()))

    w1b_ref[...] = w1_ref[0].astype(BF16)
    w2b_ref[...] = w2_ref[0].astype(BF16)
    wob_ref[...] = wo_ref[0].astype(BF16)

    @pl.when((qi == 0) & (hg == 0))
    def _():
        kv = kv_ref[...]
        ms = jnp.mean(kv * kv, axis=-1, keepdims=True)
        kvn = kv * lax.rsqrt(ms + EPS) * kvg_ref[...]
        kvn_ref[...] = kvn.astype(BF16)
        ones_rows = (lax.broadcasted_iota(jnp.int32, (DSA_PAD, tq), 0) == 0).astype(BF16)
        for t in range(seq // tq):
            kvt_ref[t, 0:lat, :] = jnp.transpose(kvn[t * tq:(t + 1) * tq, :]).astype(BF16)
            kvt_ref[t, lat:lat + DSA_PAD, :] = ones_rows
        ikb_ref[...] = misc_all_ref[:, MISC_IK:MISC_IK + IDX_DIM].astype(BF16)

    @pl.when(hg == 0)
    def _():
        for h in range(IDX_HEADS):
            iqh_ref[h] = (iq_ref[:, h * IDX_DIM:(h + 1) * IDX_DIM] * (IDX_DIM ** -0.5)).astype(BF16)
        iw_t = jnp.transpose(misc_q_ref[...]) * (IDX_HEADS ** -0.5)

        def score_tile(kt):
            off = pl.multiple_of(kt * tq, tq)
            ik_t = ikb_ref[pl.ds(off, tq), :]
            sc = jnp.zeros((tq, tq), F32)
            for h in range(IDX_HEADS):
                y = lax.dot_general(ik_t, iqh_ref[h], nt, preferred_element_type=F32)
                sc = sc + jnp.maximum(y, 0.0) * iw_t[MISC_IW + h:MISC_IW + h + 1, :]
            return sc

        def put_scores(kt, sc):
            sc_ref[kt] = sc
            bits = pltpu.bitcast(sc, jnp.int32) & jnp.int32(-(1 << 16))
            hi_ref[kt] = pltpu.bitcast(bits, F32).astype(BF16)

        def far_body(kt, carry):
            put_scores(kt, score_tile(kt))
            return carry

        lax.fori_loop(0, qi, far_body, 0)
        put_scores(qi, jnp.where(causal, score_tile(qi), -jnp.inf))

        def count(pred):
            def body(kt, cnt):
                hit = jnp.where(pred(sc_ref[kt]), 1, 0)
                return cnt + jnp.sum(hit.reshape(tq // 8, 8, tq), axis=0)

            return jnp.sum(lax.fori_loop(0, qi + 1, body, jnp.zeros((8, tq), jnp.int32)), axis=0, keepdims=True)

        def bit_body(it, state):
            key, n_ge = state
            cand_key = key + (jnp.int32(1) << (31 - it))
            cand = _key_to_float(cand_key)
            tot = count(lambda sc: sc >= cand)
            take = (tot >= k_sel) | (cand_key <= NEG_INF_KEY)
            return jnp.where(take, cand_key, key), jnp.where(take, tot, n_ge)

        one_h, zero_h = jnp.ones((tq, tq), BF16), jnp.zeros((tq, tq), BF16)

        def hi_body(it, state):
            key, n_ge = state
            cand_key = key + (jnp.int32(1) << (31 - it))
            cand = pltpu.bitcast(cand_key ^ ((cand_key >> 31) & 0x7FFF0000), F32).astype(BF16)

            def body(kt, cnt):
                hit = jnp.where(hi_ref[kt] >= cand, one_h, zero_h).reshape(tq // 16, 16, tq)
                for g in range(tq // 16):
                    cnt = cnt + hit[g]
                return cnt

            cnt = lax.fori_loop(0, qi + 1, body, jnp.zeros((16, tq), BF16))
            tot = jnp.sum(cnt.astype(F32), axis=0, keepdims=True).astype(jnp.int32)
            take = (tot >= k_sel) | (cand_key <= (NEG_INF_KEY & -(1 << 16)))
            return jnp.where(take, cand_key, key), jnp.where(take, tot, n_ge)

        state = (jnp.full((1, tq), INT_MIN, jnp.int32), jnp.full((1, tq), k_sel, jnp.int32))
        state = lax.fori_loop(0, 16, hi_body, state)
        key, n_ge = lax.fori_loop(16, 32, bit_body, state)
        thr = _key_to_float(key)

        def select_all(cr):
            def body(kt, c):
                sc_ref[kt] = jnp.where(sc_ref[kt] >= thr, 0.0, NEG_BIG)
                return c

            lax.fori_loop(0, qi, body, 0)
            sc_ref[qi] = jnp.where(causal & (sc_ref[qi] >= thr), 0.0, NEG_BIG)
            return cr

        def select_ranked(cr):
            need = (k_sel - count(lambda sc: sc > thr)).astype(F32)
            tri = (qry_i <= key_i).astype(BF16)

            def body(kt, ties_before, extra=None):
                sc = sc_ref[kt]
                tie = sc == thr
                rank = ties_before + jnp.dot(tri, jnp.where(tie, 1.0, 0.0).astype(BF16),
                                             preferred_element_type=F32)
                keep = (sc > thr) | (tie & (rank <= need))
                if extra is not None:
                    keep = keep & extra
                sc_ref[kt] = jnp.where(keep, 0.0, NEG_BIG)
                return rank[tq - 1:tq, :]

            body(qi, lax.fori_loop(0, qi, body, jnp.zeros((1, tq), F32)), extra=causal)
            return cr

        lax.cond(jnp.max(n_ge) > k_sel, select_ranked, select_all, 0)

    zero_blk = jnp.zeros((LANE, LANE), F32)

    for hh in range(hb):
        qs_ref[hh] = q_ref[:, hh * lat:(hh + 1) * lat] * (lat ** -0.5)
    acc_ref[...] = jnp.zeros_like(acc_ref)

    def logits(kt, buf):
        off = pl.multiple_of(kt * tq, tq)
        kv_t = kvn_ref[pl.ds(off, tq), :]
        for hh in range(hb):
            buf[hh] = lax.dot_general(kv_t, qs_ref[hh], nt, preferred_element_type=F32)

    def tile(kt, carry, buf, near=None):
        kv_tt = kvt_ref[kt]
        mask = sc_ref[kt]
        out = []
        for hh in range(hb):
            m_run = carry[hh]
            s = buf[hh] + mask
            if near == "prev":
                s = s + jnp.concatenate([jnp.concatenate([zero_blk, zero_blk], axis=1),
                                         jnp.concatenate([bias_ref[hh, 1], zero_blk], axis=1)], axis=0)
            elif near == "diag":
                s = s + jnp.concatenate([jnp.concatenate([bias_ref[hh, 0], bias_ref[hh, 1]], axis=1),
                                         jnp.concatenate([zero_blk, bias_ref[hh, 0]], axis=1)], axis=0)
            m_new = jnp.maximum(m_run, jnp.max(s, axis=0, keepdims=True))
            alpha = jnp.exp(m_run - m_new)
            p = jnp.exp(s - m_new)
            acc_ref[hh] = alpha * acc_ref[hh] + jnp.dot(kv_tt, p.astype(BF16), preferred_element_type=F32)
            out.append(m_new)
        return tuple(out)

    carry = tuple(jnp.full((1, tq), NEG_BIG, F32) for _ in range(hb))
    n_far = jnp.maximum(qi - 1, 0)

    def diag_and_prev(cr):
        logits(qi - 1, lgb_ref)
        cr = tile(qi, cr, lga_ref, near="diag")
        logits(jnp.maximum(qi - 2, 0), lga_ref)
        return tile(qi - 1, cr, lgb_ref, near="prev")

    def far_pair(j, cr):
        kt_a = qi - 2 - 2 * j
        kt_b = kt_a - 1
        logits(kt_b, lgb_ref)
        cr = tile(kt_a, cr, lga_ref)
        logits(jnp.maximum(kt_b - 1, 0), lga_ref)
        return tile(kt_b, cr, lgb_ref)

    logits(qi, lga_ref)
    carry = lax.cond(qi > 0, diag_and_prev, lambda cr: tile(qi, cr, lga_ref, near="diag"), carry)
    carry = lax.fori_loop(0, n_far // 2, far_pair, carry)
    carry = lax.cond(n_far % 2 == 1, lambda cr: tile(jnp.int32(0), cr, lga_ref), lambda cr: cr, carry)

    for hh in range(hb):
        o_lat_t = (acc_ref[hh, 0:lat, :] / acc_ref[hh, lat:lat + 1, :]).astype(BF16)
        y_t = lax.dot_general(wuv_ref[hh], o_lat_t, (((0,), (0,)), ((), ())), preferred_element_type=F32)
        o_ref[:, hh * DSA_HEAD_DIM:(hh + 1) * DSA_HEAD_DIM] = jnp.transpose(y_t).astype(o_ref.dtype)


def _dsa(z_big, z_small, kv_norm_g, w_uv_bf16, bias_tiles, w_ff1, w_ff2, w_out, layer, batch, seq):
    tq, hb, lat = DSA_TQ, DSA_HB, DSA_LATENT
    nq, ng = seq // tq, DSA_HEADS // hb
    k_sel = min(DSA_TOPK, seq // 4)
    assert nq * 16 <= 256, "per-slot bf16 hit counts must stay exactly representable"
    d, f = w_ff1.shape[1], w_ff1.shape[2]
    slab = f // (batch * nq * ng)
    slab_o = d // (batch * nq * ng)
    assert slab * batch * nq * ng == f and slab % LANE == 0 and slab_o * batch * nq * ng == d and slab_o % 16 == 0
    step = lambda b, i, g: (b * nq + i) * ng + g
    kern = functools.partial(_dsa_kernel, seq=seq, k_sel=k_sel)
    return pl.pallas_call(
        kern,
        grid=(batch, nq, ng),
        in_specs=[pl.BlockSpec((tq, hb * lat), lambda b, i, g: (b * nq + i, BIG_QLAT // (hb * lat) + g)),
                  pl.BlockSpec((tq, IDX_HEADS * IDX_DIM), lambda b, i, g: (b * nq + i, 0)),
                  pl.BlockSpec((seq, lat), lambda b, i, g: (b, SM_KV // lat)),
                  pl.BlockSpec((seq, LANE), lambda b, i, g: (b, SM_MISC // LANE)),
                  pl.BlockSpec((tq, LANE), lambda b, i, g: (b * nq + i, SM_MISC // LANE)),
                  pl.BlockSpec((1, lat), lambda b, i, g: (0, 0)),
                  pl.BlockSpec((hb, lat, DSA_HEAD_DIM), lambda b, i, g: (g, 0, 0)),
                  pl.BlockSpec((hb, 2, LANE, LANE), lambda b, i, g: (g, 0, 0, 0)),
                  pl.BlockSpec((1, d, slab), lambda b, i, g: (layer, 0, step(b, i, g))),
                  pl.BlockSpec((1, slab, d), lambda b, i, g: (layer, step(b, i, g), 0)),
                  pl.BlockSpec((1, slab_o, d), lambda b, i, g: (layer, step(b, i, g), 0))],
        out_specs=[pl.BlockSpec((tq, hb * DSA_HEAD_DIM), lambda b, i, g: (b * nq + i, g)),
                   pl.BlockSpec((d, slab), lambda b, i, g: (0, step(b, i, g))),
                   pl.BlockSpec((slab, d), lambda b, i, g: (step(b, i, g), 0)),
                   pl.BlockSpec((slab_o, d), lambda b, i, g: (step(b, i, g), 0))],
        out_shape=[jax.ShapeDtypeStruct((batch * seq, DSA_HEADS * DSA_HEAD_DIM), BF16),
                   jax.ShapeDtypeStruct((d, f), BF16), jax.ShapeDtypeStruct((f, d), BF16),
                   jax.ShapeDtypeStruct((d, d), BF16)],
        scratch_shapes=[pltpu.VMEM((seq, lat), BF16),
                        pltpu.VMEM((nq, lat + DSA_PAD, tq), BF16),
                        pltpu.VMEM((seq, IDX_DIM), BF16),
                        pltpu.VMEM((IDX_HEADS, tq, IDX_DIM), BF16),
                        pltpu.VMEM((nq, tq, tq), F32),
                        pltpu.VMEM((nq, tq, tq), BF16),
                        pltpu.VMEM((hb, tq, lat), BF16),
                        pltpu.VMEM((hb, tq, tq), F32),
                        pltpu.VMEM((hb, tq, tq), F32),
                        pltpu.VMEM((hb, lat + DSA_PAD, tq), F32)],
        compiler_params=_cparams(("arbitrary", "arbitrary", "arbitrary")),
        name="dsa",
    )(z_big, z_small, z_small, z_small, z_small, kv_norm_g.reshape(1, -1), w_uv_bf16, bias_tiles, w_ff1, w_ff2, w_out)


def _merge_mm_kernel(gg_ref, gd_ref, yg_ref, yd_ref, w_ref, x_ref, mod_ref, o_ref):
    merged = (jax.nn.sigmoid(gg_ref[...].astype(F32)) * yg_ref[...].astype(F32)
              + jax.nn.sigmoid(gd_ref[...].astype(F32)) * yd_ref[...].astype(F32))
    y = jnp.dot(merged.astype(BF16), w_ref[...], preferred_element_type=F32)
    o_ref[...] = x_ref[...] + mod_ref[0, MOD_GT1:MOD_GT1 + 1, :] * y


def _merge_outproj(z_big, y_gla, y_dsa, w_out_bf16, x, mod, seq, tm):
    m, d = x.shape
    tpb = seq // tm
    return pl.pallas_call(
        _merge_mm_kernel,
        grid=(m // tm,),
        in_specs=[pl.BlockSpec((tm, d), lambda i: (i, BIG_GGLA // d)),
                  pl.BlockSpec((tm, d), lambda i: (i, BIG_GDSA // d)),
                  pl.BlockSpec((tm, d), lambda i: (i, 0)),
                  pl.BlockSpec((tm, d), lambda i: (i, 0)),
                  pl.BlockSpec((d, d), lambda i: (0, 0), pipeline_mode=pl.Buffered(1)),
                  pl.BlockSpec((tm, d), lambda i: (i, 0)),
                  pl.BlockSpec((1, N_MOD, d), lambda i: (i // tpb, 0, 0))],
        out_specs=pl.BlockSpec((tm, d), lambda i: (i, 0)),
        out_shape=jax.ShapeDtypeStruct((m, d), F32),
        compiler_params=_cparams(("arbitrary",)),
        name="merge_outproj",
    )(z_big, z_big, y_gla, y_dsa, w_out_bf16, x, mod)


def _ffn_kernel(x_ref, mod_ref, g_ref, w1_ref, w2_ref, fg_ref, o_ref, h_ref, *, final):
    j = pl.program_id(1)

    @pl.when(j == 0)
    def _():
        h = _rms_mod(x_ref[...], g_ref[0], mod_ref[0, MOD_SC2:MOD_SC2 + 1, :],
                     mod_ref[0, MOD_SH2:MOD_SH2 + 1, :])
        h_ref[...] = h.astype(BF16)
        o_ref[...] = jnp.zeros_like(o_ref)

    u = jnp.dot(h_ref[...], w1_ref[...], preferred_element_type=F32)
    u = jnp.square(jnp.maximum(u, 0.0)).astype(BF16)
    tf = u.shape[1]
    for n0 in range(0, o_ref.shape[1], tf):
        o_ref[:, n0:n0 + tf] += jnp.dot(u, w2_ref[:, n0:n0 + tf], preferred_element_type=F32)

    @pl.when(j == pl.num_programs(1) - 1)
    def _():
        out = x_ref[...] + mod_ref[0, MOD_GT2:MOD_GT2 + 1, :] * o_ref[...]
        if final:
            out = out * lax.rsqrt(jnp.mean(out * out, axis=-1, keepdims=True) + EPS) * fg_ref[...]
        o_ref[...] = out


def _ffn(x, mod, norm2_g, w1, w2, layer, final_g, seq, tm, tf, final):
    m, d = x.shape
    f = w1.shape[1]
    tpb = seq // tm
    return pl.pallas_call(
        functools.partial(_ffn_kernel, final=final),
        grid=(m // tm, f // tf),
        in_specs=[pl.BlockSpec((tm, d), lambda i, j: (i, 0), pipeline_mode=pl.Buffered(1)),
                  pl.BlockSpec((1, N_MOD, d), lambda i, j: (i // tpb, 0, 0)),
                  pl.BlockSpec((1, 1, d), lambda i, j: (layer, 0, 0)),
                  pl.BlockSpec((d, tf), lambda i, j: (0, j)),
                  pl.BlockSpec((tf, d), lambda i, j: (j, 0)),
                  pl.BlockSpec((1, d), lambda i, j: (0, 0))],
        out_specs=pl.BlockSpec((tm, d), lambda i, j: (i, 0)),
        out_shape=jax.ShapeDtypeStruct((m, d), F32),
        scratch_shapes=[pltpu.VMEM((tm, d), BF16)],
        compiler_params=_cparams(("arbitrary", "arbitrary")),
        name="ffn",
    )(x, mod, norm2_g.reshape(norm2_g.shape[0], 1, d), w1, w2, final_g)


def _relayout_tables():
    src = dict(zip(("q_g", "k_g", "v_g", "a_low", "r_g", "q_lat", "kv_lat", "iq", "ik", "iw", "g_gla", "g_dsa"),
                   np.concatenate([[0], np.cumsum(IN_SIZES)[:-1]])))
    order = (("q_lat", DSA_Q_W), ("q_g", GLA_QK_W), ("k_g", GLA_QK_W), ("v_g", GLA_V_W), ("r_g", GLA_V_W),
             ("g_gla", D_MODEL), ("g_dsa", D_MODEL), ("iq", IDX_HEADS * IDX_DIM), ("kv_lat", INPROJ_TN))
    starts = np.array([src[name] + c for name, width in order for c in range(0, width, INPROJ_TN)], np.int32)
    assert len(starts) * INPROJ_TN == BIG_W + SM_WP and not (starts % 8).any()
    assert src["ik"] + IDX_DIM == src["iw"] and src["ik"] % 8 == 0 and src["a_low"] % 8 == 0
    return starts, int(src["ik"]), int(src["a_low"])


def _relayout_kernel(start_ref, w_ref, ikw_ref, al_ref, o_ref):
    del start_ref
    j = pl.program_id(1)
    d = w_ref.shape[2]
    x = w_ref[0]
    ikw = ikw_ref[0]
    tail = jnp.zeros((INPROJ_TN - DSA_LATENT - MISC_IW - IDX_HEADS, d), F32)
    misc = jnp.concatenate([ikw[:IDX_DIM], al_ref[0], ikw[IDX_DIM:], tail], axis=0)
    upper = jnp.where(j == pl.num_programs(1) - 1, misc, x[DSA_LATENT:])
    o_ref[0, :, :DSA_LATENT] = jnp.transpose(x[:DSA_LATENT]).astype(o_ref.dtype)
    o_ref[0, :, DSA_LATENT:] = jnp.transpose(upper).astype(o_ref.dtype)


def _relayout_w_in(w_in):
    depth, d, _ = w_in.shape
    starts, ik_row, al_row = _relayout_tables()
    w_t = jnp.swapaxes(w_in, 1, 2)
    grid_spec = pltpu.PrefetchScalarGridSpec(
        num_scalar_prefetch=1,
        grid=(depth, len(starts)),
        in_specs=[pl.BlockSpec((pl.Element(1), pl.Element(INPROJ_TN), pl.Element(d)),
                               lambda l, j, st: (l, st[j] * 8, 0)),
                  pl.BlockSpec((pl.Element(1), pl.Element(IDX_DIM + IDX_HEADS), pl.Element(d)),
                               lambda l, j, st: (l, ik_row, 0)),
                  pl.BlockSpec((pl.Element(1), pl.Element(GLA_GATE_RANK), pl.Element(d)),
                               lambda l, j, st: (l, al_row, 0))],
        out_specs=pl.BlockSpec((1, d, INPROJ_TN), lambda l, j, st: (l, 0, j)))
    return pl.pallas_call(
        _relayout_kernel,
        grid_spec=grid_spec,
        out_shape=jax.ShapeDtypeStruct((depth, d, BIG_W + SM_WP), BF16),
        compiler_params=_cparams(("arbitrary", "arbitrary")),
        name="w_in_relayout",
    )(jnp.asarray(starts // 8), w_t, w_t, w_t)


def kernel(x, c, w_mod, b_mod, norm1_g, w_in, w_gate_up, b_gate, gla_norm_g, kv_norm_g,
           w_uv, w_out, norm2_g, w_ff1, w_ff2, rel_bias, final_g):
    batch, seq, d = x.shape
    depth = w_mod.shape[0]
    m = batch * seq
    tm = min(1024, seq)

    mods = _modulation(c, w_mod, b_mod).reshape(depth, 8, N_MOD, d)
    bias_tiles = _bias_tiles(rel_bias)
    w_in_all = _relayout_w_in(w_in)
    xf = x.reshape(m, d)
    fg = final_g.reshape(1, d)

    for l in range(depth):
        mod = mods[l]
        z_big, z_small = _norm_matmul(xf, mod, norm1_g[l].reshape(1, d), w_in_all, l, seq, min(2048, seq), INPROJ_TN)
        y_gla = _gla(z_big, z_small, w_gate_up[l], b_gate[l], gla_norm_g[l], batch, seq)
        y_dsa, w1_bf, w2_bf, wo_bf = _dsa(z_big, z_small, kv_norm_g[l], w_uv[l].astype(BF16), bias_tiles,
                                          w_ff1, w_ff2, w_out, l, batch, seq)
        xf = _merge_outproj(z_big, y_gla, y_dsa, wo_bf, xf, mod, seq, min(512, seq))
        xf = _ffn(xf, mod, norm2_g, w1_bf, w2_bf, l, fg, seq, tm, 1024, final=(l == depth - 1))
    return xf.reshape(batch, seq, d)
```

```python
import functools
import math

import numpy as np
import jax
import jax.numpy as jnp
from jax import lax
from jax.experimental import pallas as pl
from jax.experimental.pallas import tpu as pltpu

F32 = jnp.float32
BF16 = jnp.bfloat16

D_MODEL = 2048
GLA_HEADS = 4
GLA_HEAD_K = 256
GLA_HEAD_V = 512
GLA_GATE_RANK = 16
GLA_GATE_NORMALIZER = 16.0
DSA_HEADS = 16
DSA_HEAD_DIM = 128
DSA_LATENT = 256
DSA_TOPK = 256
IDX_HEADS = 16
IDX_DIM = 64
REL_BUCKETS = 32
REL_MAX_DIST = 128
D_FF = 4 * D_MODEL
N_MOD = 6
EPS = 1e-6

GLA_QK_W = GLA_HEADS * GLA_HEAD_K
GLA_V_W = GLA_HEADS * GLA_HEAD_V
DSA_Q_W = DSA_HEADS * DSA_LATENT
IN_SIZES = (GLA_QK_W, GLA_QK_W, GLA_V_W, GLA_GATE_RANK, GLA_V_W,
            DSA_Q_W, DSA_LATENT, IDX_HEADS * IDX_DIM, IDX_DIM, IDX_HEADS,
            D_MODEL, D_MODEL)

BIG_QLAT = 0
BIG_QG = BIG_QLAT + DSA_Q_W
BIG_KG = BIG_QG + GLA_QK_W
BIG_VG = BIG_KG + GLA_QK_W
BIG_RG = BIG_VG + GLA_V_W
BIG_GGLA = BIG_RG + GLA_V_W
BIG_GDSA = BIG_GGLA + D_MODEL
BIG_W = BIG_GDSA + D_MODEL
SM_IQ = 0
SM_KV = SM_IQ + IDX_HEADS * IDX_DIM
SM_MISC = SM_KV + DSA_LATENT
MISC_IK = 0
MISC_ALOW = MISC_IK + IDX_DIM
MISC_IW = MISC_ALOW + GLA_GATE_RANK
LANE = 128
SM_W = SM_MISC + LANE
INPROJ_TN = 512
INPROJ_ROWS = 256
SM_WP = -(-SM_W // INPROJ_TN) * INPROJ_TN

MOD_SH1, MOD_SC1, MOD_GT1, MOD_SH2, MOD_SC2, MOD_GT2 = range(6)

GLA_CHUNK = 256
GLA_SUB = 8
LOG2E = 1.4426950408889634
GLA_BLOCK = 256
GLA_HB = 4
DSA_TQ = 256
DSA_HB = 8
DSA_PAD = 16
INT_MIN = -2 ** 31
NEG_INF_KEY = 0x807FFFFF - 2 ** 32
NEG_BIG = -1e30

VMEM_LIMIT = 58 * 1024 * 1024


def _cparams(sem):
    return pltpu.CompilerParams(dimension_semantics=sem, vmem_limit_bytes=VMEM_LIMIT)


def _rms_mod(x, g, sc, sh):
    ms = jnp.mean(x * x, axis=-1, keepdims=True)
    return (x * lax.rsqrt(ms + EPS) * g) * (1.0 + sc) + sh


def _mod_kernel(ct_ref, w_ref, b_ref, o_ref, *, batch):
    ct = ct_ref[...]
    ct = ct * jax.nn.sigmoid(ct)
    d, tn = w_ref.shape[1], w_ref.shape[2]
    o_ref[...] = jnp.zeros_like(o_ref)
    for b in range(batch):
        cb = jnp.broadcast_to(ct[:, b:b + 1], (d, LANE))
        for t in range(tn // LANE):
            seg = jnp.sum(w_ref[0, :, t * LANE:(t + 1) * LANE] * cb, axis=0, keepdims=True)
            o_ref[0, b:b + 1, t * LANE:(t + 1) * LANE] = seg + b_ref[0, :, t * LANE:(t + 1) * LANE]


def _modulation(c, w_mod, b_mod):
    depth, d, n = w_mod.shape
    batch = c.shape[0]
    tn = 1024
    ct = jnp.zeros((d, LANE), F32).at[:, :batch].set(c.T)
    return pl.pallas_call(
        functools.partial(_mod_kernel, batch=batch),
        grid=(depth, n // tn),
        in_specs=[pl.BlockSpec((d, LANE), lambda l, j: (0, 0)),
                  pl.BlockSpec((1, d, tn), lambda l, j: (l, 0, j)),
                  pl.BlockSpec((1, 1, tn), lambda l, j: (l, 0, j))],
        out_specs=pl.BlockSpec((1, 8, tn), lambda l, j: (l, 0, j)),
        out_shape=jax.ShapeDtypeStruct((depth, 8, n), F32),
        compiler_params=_cparams(("arbitrary", "arbitrary")),
        name="adaln_mod",
    )(ct, w_mod, b_mod.reshape(depth, 1, n))


def _norm_mm_kernel(x_ref, mod_ref, g_ref, w_ref, big_ref, small_ref, h_ref, *, n_big):
    j = pl.program_id(1)

    @pl.when(j == 0)
    def _():
        rows = min(512, x_ref.shape[0])
        for r0 in range(0, x_ref.shape[0], rows):
            h = _rms_mod(x_ref[r0:r0 + rows, :], g_ref[...], mod_ref[0, MOD_SC1:MOD_SC1 + 1, :],
                         mod_ref[0, MOD_SH1:MOD_SH1 + 1, :])
            h_ref[r0:r0 + rows, :] = h.astype(BF16)

    rows = min(INPROJ_ROWS, h_ref.shape[0])

    @pl.when(j < n_big)
    def _():
        for r0 in range(0, h_ref.shape[0], rows):
            y = jnp.dot(h_ref[r0:r0 + rows, :], w_ref[0], preferred_element_type=F32)
            big_ref[r0:r0 + rows, :] = y.astype(big_ref.dtype)

    @pl.when(j >= n_big)
    def _():
        for r0 in range(0, h_ref.shape[0], rows):
            small_ref[r0:r0 + rows, :] = jnp.dot(h_ref[r0:r0 + rows, :], w_ref[0], preferred_element_type=F32)


def _norm_matmul(x, mod, g, w, layer, seq, tm, tn):
    m, d = x.shape
    n_big, n_small = BIG_W // tn, SM_WP // tn
    tpb = seq // tm
    return pl.pallas_call(
        functools.partial(_norm_mm_kernel, n_big=n_big),
        grid=(m // tm, n_big + n_small),
        in_specs=[pl.BlockSpec((tm, d), lambda i, j: (i, 0), pipeline_mode=pl.Buffered(1)),
                  pl.BlockSpec((1, N_MOD, d), lambda i, j: (i // tpb, 0, 0)),
                  pl.BlockSpec((1, d), lambda i, j: (0, 0)),
                  pl.BlockSpec((1, d, tn), lambda i, j: (layer, 0, j))],
        out_specs=[pl.BlockSpec((tm, tn), lambda i, j: (i, jnp.minimum(j, n_big - 1))),
                   pl.BlockSpec((tm, tn), lambda i, j: (i, jnp.maximum(j - n_big, 0)))],
        out_shape=[jax.ShapeDtypeStruct((m, BIG_W), BF16), jax.ShapeDtypeStruct((m, SM_WP), F32)],
        scratch_shapes=[pltpu.VMEM((tm, d), BF16)],
        compiler_params=_cparams(("arbitrary", "arbitrary")),
        name="norm_inproj",
    )(x, mod, g, w)


def _gla_kernel(q_ref, k_ref, v_ref, r_ref, misc_ref, wa_ref, ba_ref, gn_ref, o_ref,
                s_all_ref, a_all_ref, *, block, chunk):
    dk, dv, sub = GLA_HEAD_K, GLA_HEAD_V, GLA_SUB
    nsub = chunk // sub

    @pl.when(pl.program_id(2) == 0)
    def _():
        s_all_ref[...] = jnp.zeros_like(s_all_ref)

    a_all_ref[...] = jnp.zeros_like(a_all_ref)

    a_low = misc_ref[:, MISC_ALOW:MISC_ALOW + GLA_GATE_RANK].astype(BF16)
    xg = jnp.dot(a_low, wa_ref[...].astype(BF16), preferred_element_type=F32) + ba_ref[...]
    g_all = (jnp.minimum(xg, 0.0) - jnp.log1p(jnp.exp(-jnp.abs(xg)))) * (LOG2E / GLA_GATE_NORMALIZER)

    r_i = lax.broadcasted_iota(jnp.int32, (chunk, chunk), 0)
    c_i = lax.broadcasted_iota(jnp.int32, (chunk, chunk), 1)
    tril = (r_i >= c_i).astype(BF16)
    row_in_chunk = lax.broadcasted_iota(jnp.int32, (chunk, dk), 0)
    lane3 = lax.broadcasted_iota(jnp.int32, (nsub, sub, sub), 2)
    row3 = lax.broadcasted_iota(jnp.int32, (nsub, sub, sub), 1)
    gn = gn_ref[...]

    for c, hh in [(c, hh) for c in range(block // chunk) for hh in range(GLA_HB)]:
        lo = c * chunk
        s_ref, a_ref = s_all_ref.at[hh], a_all_ref.at[hh]
        qf = q_ref[lo:lo + chunk, hh * dk:(hh + 1) * dk].astype(F32) * (dk ** -0.5)
        kf = k_ref[lo:lo + chunk, hh * dk:(hh + 1) * dk].astype(F32)
        vb = v_ref[lo:lo + chunk, hh * dv:(hh + 1) * dv]
        g = g_all[lo:lo + chunk, hh * dk:(hh + 1) * dk]

        g_hi = g.astype(BF16)
        r1 = g - g_hi.astype(F32)
        g_mid = r1.astype(BF16)
        g_lo = (r1 - g_mid.astype(F32)).astype(BF16)
        b = (jnp.dot(tril, g_hi, preferred_element_type=F32)
             + jnp.dot(tril, g_mid, preferred_element_type=F32)
             + jnp.dot(tril, g_lo, preferred_element_type=F32))

        state = s_ref[...]
        o = jnp.dot((qf * jnp.exp2(b)).astype(BF16), state.astype(BF16), preferred_element_type=F32)

        m = chunk // 2
        while m >= sub:
            nseg = chunk // (2 * m)
            pieces = []
            for s in range(nseg):
                ref_row = b[s * 2 * m + m - 1:s * 2 * m + m, :]
                pieces.append(jnp.broadcast_to(ref_row, (2 * m, dk)))
            ref_full = pieces[0] if nseg == 1 else jnp.concatenate(pieces, axis=0)
            fac = jnp.exp2(-jnp.abs(b - ref_full))
            is_q = (row_in_chunk & m) != 0
            xs = jnp.where(is_q, qf, kf) * fac
            for s in range(nseg):
                base = s * 2 * m
                blk = lax.dot_general(xs[base + m:base + 2 * m, :].astype(BF16), xs[base:base + m, :].astype(BF16),
                                      (((1,), (1,)), ((), ())), preferred_element_type=F32)
                a_ref[base + m:base + 2 * m, base:base + m] = blk
            m //= 2

        q4 = qf.reshape(nsub, sub, dk)
        k4 = kf.reshape(nsub, sub, dk)
        b4 = b.reshape(nsub, sub, dk)
        ad = jnp.zeros((nsub, sub, sub), F32)
        for j in range(sub):
            e = jnp.exp2(b4 - b4[:, j:j + 1, :])
            col = jnp.sum(q4 * e * k4[:, j:j + 1, :], axis=-1, keepdims=True)
            ad = jnp.where(lane3 == j, col, ad)
        ad = jnp.where(row3 >= lane3, ad, 0.0)
        for i in range(nsub):
            a_ref[i * sub:(i + 1) * sub, i * sub:(i + 1) * sub] = ad[i]

        o = o + jnp.dot(a_ref[...].astype(BF16), vb, preferred_element_type=F32)

        b_last = b[chunk - 1:chunk, :]
        kd = (kf * jnp.exp2(b_last - b)).astype(BF16)
        upd = lax.dot_general(kd, vb, (((0,), (0,)), ((), ())), preferred_element_type=F32)
        decay_col = jnp.transpose(jnp.broadcast_to(jnp.exp2(b_last), (LANE, dk)))[:, 0:1]
        s_ref[...] = decay_col * state + upd

        o = o * lax.rsqrt(jnp.mean(o * o, axis=-1, keepdims=True) + EPS) * gn
        r = r_ref[lo:lo + chunk, hh * dv:(hh + 1) * dv].astype(F32)
        o_ref[lo:lo + chunk, hh * dv:(hh + 1) * dv] = (o * (r * jax.nn.sigmoid(r))).astype(o_ref.dtype)


def _gla(z_big, z_small, w_gate_up, b_gate, gla_norm_g, batch, seq):
    blk = min(GLA_BLOCK, seq)
    nb = seq // blk
    dk, dv = GLA_HB * GLA_HEAD_K, GLA_HB * GLA_HEAD_V
    row = lambda b, h, t: b * nb + t
    kern = functools.partial(_gla_kernel, block=blk, chunk=GLA_CHUNK)
    return pl.pallas_call(
        kern,
        grid=(batch, GLA_HEADS // GLA_HB, nb),
        in_specs=[pl.BlockSpec((blk, dk), lambda b, h, t: (row(b, h, t), BIG_QG // dk + h)),
                  pl.BlockSpec((blk, dk), lambda b, h, t: (row(b, h, t), BIG_KG // dk + h)),
                  pl.BlockSpec((blk, dv), lambda b, h, t: (row(b, h, t), BIG_VG // dv + h)),
                  pl.BlockSpec((blk, dv), lambda b, h, t: (row(b, h, t), BIG_RG // dv + h)),
                  pl.BlockSpec((blk, LANE), lambda b, h, t: (row(b, h, t), SM_MISC // LANE)),
                  pl.BlockSpec((GLA_GATE_RANK, dk), lambda b, h, t: (0, h)),
                  pl.BlockSpec((1, dk), lambda b, h, t: (0, h)),
                  pl.BlockSpec((1, GLA_HEAD_V), lambda b, h, t: (0, 0))],
        out_specs=pl.BlockSpec((blk, dv), lambda b, h, t: (row(b, h, t), h)),
        out_shape=jax.ShapeDtypeStruct((batch * seq, GLA_V_W), BF16),
        scratch_shapes=[pltpu.VMEM((GLA_HB, GLA_HEAD_K, GLA_HEAD_V), F32),
                        pltpu.VMEM((GLA_HB, GLA_CHUNK, GLA_CHUNK), F32)],
        compiler_params=_cparams(("arbitrary", "arbitrary", "arbitrary")),
        name="gla",
    )(z_big, z_big, z_big, z_big, z_small, w_gate_up, b_gate.reshape(1, -1), gla_norm_g.reshape(1, -1))


def _rel_bucket_np(d):
    max_exact = REL_BUCKETS // 2
    d = np.maximum(d, 0)
    df = np.maximum(d, 1).astype(np.float32)
    large = max_exact + (np.log(df / np.float32(max_exact)) / np.float32(math.log(REL_MAX_DIST / max_exact))
                         * np.float32(REL_BUCKETS - max_exact)).astype(np.int32)
    large = np.minimum(large, REL_BUCKETS - 1)
    return np.where(d < max_exact, d, large).astype(np.int32)


def _bias_kernel(rb_ref, bucket_ref, o_ref):
    h = pl.program_id(0)
    far = rb_ref[REL_BUCKETS - 1, h]
    for u in range(2):
        bk = bucket_ref[u]
        acc = jnp.zeros(bk.shape, F32)
        for bb in range(REL_BUCKETS):
            acc = jnp.where(bk == bb, rb_ref[bb, h], acc)
        o_ref[0, u] = acc - far


def _bias_tiles(rel_bias):
    j = np.arange(LANE)[:, None]
    i = np.arange(LANE)[None, :]
    bucket = np.stack([_rel_bucket_np(i - j), _rel_bucket_np(LANE + i - j)]).astype(np.int32)
    assert int(_rel_bucket_np(np.array([LANE]))[0]) == REL_BUCKETS - 1
    return pl.pallas_call(
        _bias_kernel,
        grid=(DSA_HEADS,),
        in_specs=[pl.BlockSpec(memory_space=pltpu.SMEM),
                  pl.BlockSpec((2, LANE, LANE), lambda h: (0, 0, 0))],
        out_specs=pl.BlockSpec((1, 2, LANE, LANE), lambda h: (h, 0, 0, 0)),
        out_shape=jax.ShapeDtypeStruct((DSA_HEADS, 2, LANE, LANE), F32),
        compiler_params=_cparams(("arbitrary",)),
        name="t5_bias_tiles",
    )(rel_bias, jnp.asarray(bucket))


def _key_to_float(key):
    return pltpu.bitcast(key ^ ((key >> 31) & 0x7FFFFFFF), F32)


def _dsa_kernel(q_ref, iq_ref, kv_ref, misc_all_ref, misc_q_ref, kvg_ref, wuv_ref, bias_ref, w1_ref, w2_ref, wo_ref,
                o_ref, w1b_ref, w2b_ref, wob_ref, kvn_ref, kvt_ref, ikb_ref, iqh_ref, sc_ref, hi_ref, qs_ref, lga_ref, lgb_ref, acc_ref, *, seq, k_sel):
    tq, hb, lat = DSA_TQ, DSA_HB, DSA_LATENT
    qi = pl.program_id(1)
    hg = pl.program_id(2)
    key_i = lax.broadcasted_iota(jnp.int32, (tq, tq), 0)
    qry_i = lax.broadcasted_iota(jnp.int32, (tq, tq), 1)
    causal = key_i <= qry_i
    nt = (((1,), (1,)), ((), ()))

    w1b_ref[...] = w1_ref[0].astype(BF16)
    w2b_ref[...] = w2_ref[0].astype(BF16)
    wob_ref[...] = wo_ref[0].astype(BF16)

    @pl.when((qi == 0) & (hg == 0))
    def _():
        kv = kv_ref[...]
        ms = jnp.mean(kv * kv, axis=-1, keepdims=True)
        kvn = kv * lax.rsqrt(ms + EPS) * kvg_ref[...]
        kvn_ref[...] = kvn.astype(BF16)
        ones_rows = (lax.broadcasted_iota(jnp.int32, (DSA_PAD, tq), 0) == 0).astype(BF16)
        for t in range(seq // tq):
            kvt_ref[t, 0:lat, :] = jnp.transpose(kvn[t * tq:(t + 1) * tq, :]).astype(BF16)
            kvt_ref[t, lat:lat + DSA_PAD, :] = ones_rows
        ikb_ref[...] = misc_all_ref[:, MISC_IK:MISC_IK + IDX_DIM].astype(BF16)

    @pl.when(hg == 0)
    def _():
        for h in range(IDX_HEADS):
            iqh_ref[h] = (iq_ref[:, h * IDX_DIM:(h + 1) * IDX_DIM] * (IDX_DIM ** -0.5)).astype(BF16)
        iw_t = jnp.transpose(misc_q_ref[...]) * (IDX_HEADS ** -0.5)

        def score_tile(kt):
            off = pl.multiple_of(kt * tq, tq)
            ik_t = ikb_ref[pl.ds(off, tq), :]
            sc = jnp.zeros((tq, tq), F32)
            for h in range(IDX_HEADS):
                y = lax.dot_general(ik_t, iqh_ref[h], nt, preferred_element_type=F32)
                sc = sc + jnp.maximum(y, 0.0) * iw_t[MISC_IW + h:MISC_IW + h + 1, :]
            return sc

        def put_scores(kt, sc):
            sc_ref[kt] = sc
            bits = pltpu.bitcast(sc, jnp.int32) & jnp.int32(-(1 << 16))
            hi_ref[kt] = pltpu.bitcast(bits, F32).astype(BF16)

        def far_body(kt, carry):
            put_scores(kt, score_tile(kt))
            return carry

        lax.fori_loop(0, qi, far_body, 0)
        put_scores(qi, jnp.where(causal, score_tile(qi), -jnp.inf))

        def count(pred):
            def body(kt, cnt):
                hit = jnp.where(pred(sc_ref[kt]), 1, 0)
                return cnt + jnp.sum(hit.reshape(tq // 8, 8, tq), axis=0)

            return jnp.sum(lax.fori_loop(0, qi + 1, body, jnp.zeros((8, tq), jnp.int32)), axis=0, keepdims=True)

        def bit_body(it, state):
            key, n_ge = state
            cand_key = key + (jnp.int32(1) << (31 - it))
            cand = _key_to_float(cand_key)
            tot = count(lambda sc: sc >= cand)
            take = (tot >= k_sel) | (cand_key <= NEG_INF_KEY)
            return jnp.where(take, cand_key, key), jnp.where(take, tot, n_ge)

        one_h, zero_h = jnp.ones((tq, tq), BF16), jnp.zeros((tq, tq), BF16)

        def hi_body(it, state):
            key, n_ge = state
            cand_key = key + (jnp.int32(1) << (31 - it))
            cand = pltpu.bitcast(cand_key ^ ((cand_key >> 31) & 0x7FFF0000), F32).astype(BF16)

            def body(kt, cnt):
                hit = jnp.where(hi_ref[kt] >= cand, one_h, zero_h).reshape(tq // 16, 16, tq)
                for g in range(tq // 16):
                    cnt = cnt + hit[g]
                return cnt

            cnt = lax.fori_loop(0, qi + 1, body, jnp.zeros((16, tq), BF16))
            tot = jnp.sum(cnt.astype(F32), axis=0, keepdims=True).astype(jnp.int32)
            take = (tot >= k_sel) | (cand_key <= (NEG_INF_KEY & -(1 << 16)))
            return jnp.where(take, cand_key, key), jnp.where(take, tot, n_ge)

        state = (jnp.full((1, tq), INT_MIN, jnp.int32), jnp.full((1, tq), k_sel, jnp.int32))
        state = lax.fori_loop(0, 16, hi_body, state)
        key, n_ge = lax.fori_loop(16, 32, bit_body, state)
        thr = _key_to_float(key)

        def select_all(cr):
            def body(kt, c):
                sc_ref[kt] = jnp.where(sc_ref[kt] >= thr, 0.0, NEG_BIG)
                return c

            lax.fori_loop(0, qi, body, 0)
            sc_ref[qi] = jnp.where(causal & (sc_ref[qi] >= thr), 0.0, NEG_BIG)
            return cr

        def select_ranked(cr):
            need = (k_sel - count(lambda sc: sc > thr)).astype(F32)
            tri = (qry_i <= key_i).astype(BF16)

            def body(kt, ties_before, extra=None):
                sc = sc_ref[kt]
                tie = sc == thr
                rank = ties_before + jnp.dot(tri, jnp.where(tie, 1.0, 0.0).astype(BF16),
                                             preferred_element_type=F32)
                keep = (sc > thr) | (tie & (rank <= need))
                if extra is not None:
                    keep = keep & extra
                sc_ref[kt] = jnp.where(keep, 0.0, NEG_BIG)
                return rank[tq - 1:tq, :]

            body(qi, lax.fori_loop(0, qi, body, jnp.zeros((1, tq), F32)), extra=causal)
            return cr

        lax.cond(jnp.max(n_ge) > k_sel, select_ranked, select_all, 0)

    zero_blk = jnp.zeros((LANE, LANE), F32)

    for hh in range(hb):
        qs_ref[hh] = q_ref[:, hh * lat:(hh + 1) * lat] * (lat ** -0.5)
    acc_ref[...] = jnp.zeros_like(acc_ref)

    def logits(kt, buf):
        off = pl.multiple_of(kt * tq, tq)
        kv_t = kvn_ref[pl.ds(off, tq), :]
        for hh in range(hb):
            buf[hh] = lax.dot_general(kv_t, qs_ref[hh], nt, preferred_element_type=F32)

    def tile(kt, carry, buf, near=None):
        kv_tt = kvt_ref[kt]
        mask = sc_ref[kt]
        out = []
        for hh in range(hb):
            m_run = carry[hh]
            s = buf[hh] + mask
            if near == "prev":
                s = s + jnp.concatenate([jnp.concatenate([zero_blk, zero_blk], axis=1),
                                         jnp.concatenate([bias_ref[hh, 1], zero_blk], axis=1)], axis=0)
            elif near == "diag":
                s = s + jnp.concatenate([jnp.concatenate([bias_ref[hh, 0], bias_ref[hh, 1]], axis=1),
                                         jnp.concatenate([zero_blk, bias_ref[hh, 0]], axis=1)], axis=0)
            m_new = jnp.maximum(m_run, jnp.max(s, axis=0, keepdims=True))
            alpha = jnp.exp(m_run - m_new)
            p = jnp.exp(s - m_new)
            acc_ref[hh] = alpha * acc_ref[hh] + jnp.dot(kv_tt, p.astype(BF16), preferred_element_type=F32)
            out.append(m_new)
        return tuple(out)

    carry = tuple(jnp.full((1, tq), NEG_BIG, F32) for _ in range(hb))
    n_far = jnp.maximum(qi - 1, 0)

    def diag_and_prev(cr):
        logits(qi - 1, lgb_ref)
        cr = tile(qi, cr, lga_ref, near="diag")
        logits(jnp.maximum(qi - 2, 0), lga_ref)
        return tile(qi - 1, cr, lgb_ref, near="prev")

    def far_pair(j, cr):
        kt_a = qi - 2 - 2 * j
        kt_b = kt_a - 1
        logits(kt_b, lgb_ref)
        cr = tile(kt_a, cr, lga_ref)
        logits(jnp.maximum(kt_b - 1, 0), lga_ref)
        return tile(kt_b, cr, lgb_ref)

    logits(qi, lga_ref)
    carry = lax.cond(qi > 0, diag_and_prev, lambda cr: tile(qi, cr, lga_ref, near="diag"), carry)
    carry = lax.fori_loop(0, n_far // 2, far_pair, carry)
    carry = lax.cond(n_far % 2 == 1, lambda cr: tile(jnp.int32(0), cr, lga_ref), lambda cr: cr, carry)

    for hh in range(hb):
        o_lat_t = (acc_ref[hh, 0:lat, :] / acc_ref[hh, lat:lat + 1, :]).astype(BF16)
        y_t = lax.dot_general(wuv_ref[hh], o_lat_t, (((0,), (0,)), ((), ())), preferred_element_type=F32)
        o_ref[:, hh * DSA_HEAD_DIM:(hh + 1) * DSA_HEAD_DIM] = jnp.transpose(y_t).astype(o_ref.dtype)


def _dsa(z_big, z_small, kv_norm_g, w_uv_bf16, bias_tiles, w_ff1, w_ff2, w_out, layer, batch, seq):
    tq, hb, lat = DSA_TQ, DSA_HB, DSA_LATENT
    nq, ng = seq // tq, DSA_HEADS // hb
    k_sel = min(DSA_TOPK, seq // 4)
    assert nq * 16 <= 256, "per-slot bf16 hit counts must stay exactly representable"
    d, f = w_ff1.shape[1], w_ff1.shape[2]
    slab = f // (batch * nq * ng)
    slab_o = d // (batch * nq * ng)
    assert slab * batch * nq * ng == f and slab % LANE == 0 and slab_o * batch * nq * ng == d and slab_o % 16 == 0
    step = lambda b, i, g: (b * nq + i) * ng + g
    kern = functools.partial(_dsa_kernel, seq=seq, k_sel=k_sel)
    return pl.pallas_call(
        kern,
        grid=(batch, nq, ng),
        in_specs=[pl.BlockSpec((tq, hb * lat), lambda b, i, g: (b * nq + i, BIG_QLAT // (hb * lat) + g)),
                  pl.BlockSpec((tq, IDX_HEADS * IDX_DIM), lambda b, i, g: (b * nq + i, 0)),
                  pl.BlockSpec((seq, lat), lambda b, i, g: (b, SM_KV // lat)),
                  pl.BlockSpec((seq, LANE), lambda b, i, g: (b, SM_MISC // LANE)),
                  pl.BlockSpec((tq, LANE), lambda b, i, g: (b * nq + i, SM_MISC // LANE)),
                  pl.BlockSpec((1, lat), lambda b, i, g: (0, 0)),
                  pl.BlockSpec((hb, lat, DSA_HEAD_DIM), lambda b, i, g: (g, 0, 0)),
                  pl.BlockSpec((hb, 2, LANE, LANE), lambda b, i, g: (g, 0, 0, 0)),
                  pl.BlockSpec((1, d, slab), lambda b, i, g: (layer, 0, step(b, i, g))),
                  pl.BlockSpec((1, slab, d), lambda b, i, g: (layer, step(b, i, g), 0)),
                  pl.BlockSpec((1, slab_o, d), lambda b, i, g: (layer, step(b, i, g), 0))],
        out_specs=[pl.BlockSpec((tq, hb * DSA_HEAD_DIM), lambda b, i, g: (b * nq + i, g)),
                   pl.BlockSpec((d, slab), lambda b, i, g: (0, step(b, i, g))),
                   pl.BlockSpec((slab, d), lambda b, i, g: (step(b, i, g), 0)),
                   pl.BlockSpec((slab_o, d), lambda b, i, g: (step(b, i, g), 0))],
        out_shape=[jax.ShapeDtypeStruct((batch * seq, DSA_HEADS * DSA_HEAD_DIM), BF16),
                   jax.ShapeDtypeStruct((d, f), BF16), jax.ShapeDtypeStruct((f, d), BF16),
                   jax.ShapeDtypeStruct((d, d), BF16)],
        scratch_shapes=[pltpu.VMEM((seq, lat), BF16),
                        pltpu.VMEM((nq, lat + DSA_PAD, tq), BF16),
                        pltpu.VMEM((seq, IDX_DIM), BF16),
                        pltpu.VMEM((IDX_HEADS, tq, IDX_DIM), BF16),
                        pltpu.VMEM((nq, tq, tq), F32),
                        pltpu.VMEM((nq, tq, tq), BF16),
                        pltpu.VMEM((hb, tq, lat), BF16),
                        pltpu.VMEM((hb, tq, tq), F32),
                        pltpu.VMEM((hb, tq, tq), F32),
                        pltpu.VMEM((hb, lat + DSA_PAD, tq), F32)],
        compiler_params=_cparams(("arbitrary", "arbitrary", "arbitrary")),
        name="dsa",
    )(z_big, z_small, z_small, z_small, z_small, kv_norm_g.reshape(1, -1), w_uv_bf16, bias_tiles, w_ff1, w_ff2, w_out)


def _merge_mm_kernel(gg_ref, gd_ref, yg_ref, yd_ref, w_ref, x_ref, mod_ref, g2_ref, o_ref, h2_ref):
    merged = (jax.nn.sigmoid(gg_ref[...].astype(F32)) * yg_ref[...].astype(F32)
              + jax.nn.sigmoid(gd_ref[...].astype(F32)) * yd_ref[...].astype(F32))
    y = jnp.dot(merged.astype(BF16), w_ref[...], preferred_element_type=F32)
    x_new = x_ref[...] + mod_ref[0, MOD_GT1:MOD_GT1 + 1, :] * y
    o_ref[...] = x_new
    h2_ref[...] = _rms_mod(x_new, g2_ref[0], mod_ref[0, MOD_SC2:MOD_SC2 + 1, :],
                           mod_ref[0, MOD_SH2:MOD_SH2 + 1, :]).astype(BF16)


def _merge_outproj(z_big, y_gla, y_dsa, w_out_bf16, x, mod, norm2_g, layer, seq, tm):
    m, d = x.shape
    tpb = seq // tm
    return pl.pallas_call(
        _merge_mm_kernel,
        grid=(m // tm,),
        in_specs=[pl.BlockSpec((tm, d), lambda i: (i, BIG_GGLA // d)),
                  pl.BlockSpec((tm, d), lambda i: (i, BIG_GDSA // d)),
                  pl.BlockSpec((tm, d), lambda i: (i, 0)),
                  pl.BlockSpec((tm, d), lambda i: (i, 0)),
                  pl.BlockSpec((d, d), lambda i: (0, 0), pipeline_mode=pl.Buffered(1)),
                  pl.BlockSpec((tm, d), lambda i: (i, 0)),
                  pl.BlockSpec((1, N_MOD, d), lambda i: (i // tpb, 0, 0)),
                  pl.BlockSpec((1, 1, d), lambda i: (layer, 0, 0))],
        out_specs=[pl.BlockSpec((tm, d), lambda i: (i, 0)), pl.BlockSpec((tm, d), lambda i: (i, 0))],
        out_shape=[jax.ShapeDtypeStruct((m, d), F32), jax.ShapeDtypeStruct((m, d), BF16)],
        compiler_params=_cparams(("arbitrary",)),
        name="merge_outproj",
    )(z_big, z_big, y_gla, y_dsa, w_out_bf16, x, mod, norm2_g.reshape(norm2_g.shape[0], 1, d))


def _ffn_kernel(x_ref, h_ref, mod_ref, w1_ref, w2_ref, fg_ref, o_ref, *, final):
    j = pl.program_id(1)

    @pl.when(j == 0)
    def _():
        o_ref[...] = jnp.zeros_like(o_ref)

    u = jnp.dot(h_ref[...], w1_ref[...], preferred_element_type=F32)
    u = jnp.square(jnp.maximum(u, 0.0)).astype(BF16)
    tf = u.shape[1]
    for n0 in range(0, o_ref.shape[1], tf):
        o_ref[:, n0:n0 + tf] += jnp.dot(u, w2_ref[:, n0:n0 + tf], preferred_element_type=F32)

    @pl.when(j == pl.num_programs(1) - 1)
    def _():
        out = x_ref[...] + mod_ref[0, MOD_GT2:MOD_GT2 + 1, :] * o_ref[...]
        if final:
            out = out * lax.rsqrt(jnp.mean(out * out, axis=-1, keepdims=True) + EPS) * fg_ref[...]
        o_ref[...] = out


def _ffn(x, h, mod, w1, w2, final_g, seq, tm, tf, final):
    m, d = x.shape
    f = w1.shape[1]
    tpb = seq // tm
    return pl.pallas_call(
        functools.partial(_ffn_kernel, final=final),
        grid=(m // tm, f // tf),
        in_specs=[pl.BlockSpec((tm, d), lambda i, j: (i, 0), pipeline_mode=pl.Buffered(1)),
                  pl.BlockSpec((tm, d), lambda i, j: (i, 0), pipeline_mode=pl.Buffered(1)),
                  pl.BlockSpec((1, N_MOD, d), lambda i, j: (i // tpb, 0, 0)),
                  pl.BlockSpec((d, tf), lambda i, j: (0, j)),
                  pl.BlockSpec((tf, d), lambda i, j: (j, 0)),
                  pl.BlockSpec((1, d), lambda i, j: (0, 0))],
        out_specs=pl.BlockSpec((tm, d), lambda i, j: (i, 0)),
        out_shape=jax.ShapeDtypeStruct((m, d), F32),
        compiler_params=_cparams(("arbitrary", "arbitrary")),
        name="ffn",
    )(x, h, mod, w1, w2, final_g)


def _relayout_tables():
    src = dict(zip(("q_g", "k_g", "v_g", "a_low", "r_g", "q_lat", "kv_lat", "iq", "ik", "iw", "g_gla", "g_dsa"),
                   np.concatenate([[0], np.cumsum(IN_SIZES)[:-1]])))
    order = (("q_lat", DSA_Q_W), ("q_g", GLA_QK_W), ("k_g", GLA_QK_W), ("v_g", GLA_V_W), ("r_g", GLA_V_W),
             ("g_gla", D_MODEL), ("g_dsa", D_MODEL), ("iq", IDX_HEADS * IDX_DIM), ("kv_lat", INPROJ_TN))
    starts = np.array([src[name] + c for name, width in order for c in range(0, width, INPROJ_TN)], np.int32)
    assert len(starts) * INPROJ_TN == BIG_W + SM_WP and not (starts % 8).any()
    assert src["ik"] + IDX_DIM == src["iw"] and src["ik"] % 8 == 0 and src["a_low"] % 8 == 0
    return starts, int(src["ik"]), int(src["a_low"])


def _relayout_kernel(start_ref, w_ref, ikw_ref, al_ref, o_ref):
    del start_ref
    j = pl.program_id(1)
    d = w_ref.shape[2]
    x = w_ref[0]
    ikw = ikw_ref[0]
    tail = jnp.zeros((INPROJ_TN - DSA_LATENT - MISC_IW - IDX_HEADS, d), F32)
    misc = jnp.concatenate([ikw[:IDX_DIM], al_ref[0], ikw[IDX_DIM:], tail], axis=0)
    upper = jnp.where(j == pl.num_programs(1) - 1, misc, x[DSA_LATENT:])
    o_ref[0, :, :DSA_LATENT] = jnp.transpose(x[:DSA_LATENT]).astype(o_ref.dtype)
    o_ref[0, :, DSA_LATENT:] = jnp.transpose(upper).astype(o_ref.dtype)


def _relayout_w_in(w_in):
    depth, d, _ = w_in.shape
    starts, ik_row, al_row = _relayout_tables()
    w_t = jnp.swapaxes(w_in, 1, 2)
    grid_spec = pltpu.PrefetchScalarGridSpec(
        num_scalar_prefetch=1,
        grid=(depth, len(starts)),
        in_specs=[pl.BlockSpec((pl.Element(1), pl.Element(INPROJ_TN), pl.Element(d)),
                               lambda l, j, st: (l, st[j] * 8, 0)),
                  pl.BlockSpec((pl.Element(1), pl.Element(IDX_DIM + IDX_HEADS), pl.Element(d)),
                               lambda l, j, st: (l, ik_row, 0)),
                  pl.BlockSpec((pl.Element(1), pl.Element(GLA_GATE_RANK), pl.Element(d)),
                               lambda l, j, st: (l, al_row, 0))],
        out_specs=pl.BlockSpec((1, d, INPROJ_TN), lambda l, j, st: (l, 0, j)))
    return pl.pallas_call(
        _relayout_kernel,
        grid_spec=grid_spec,
        out_shape=jax.ShapeDtypeStruct((depth, d, BIG_W + SM_WP), BF16),
        compiler_params=_cparams(("arbitrary", "arbitrary")),
        name="w_in_relayout",
    )(jnp.asarray(starts // 8), w_t, w_t, w_t)


def kernel(x, c, w_mod, b_mod, norm1_g, w_in, w_gate_up, b_gate, gla_norm_g, kv_norm_g,
           w_uv, w_out, norm2_g, w_ff1, w_ff2, rel_bias, final_g):
    batch, seq, d = x.shape
    depth = w_mod.shape[0]
    m = batch * seq
    tm = min(1024, seq)

    mods = _modulation(c, w_mod, b_mod).reshape(depth, 8, N_MOD, d)
    bias_tiles = _bias_tiles(rel_bias)
    w_in_all = _relayout_w_in(w_in)
    xf = x.reshape(m, d)
    fg = final_g.reshape(1, d)

    for l in range(depth):
        mod = mods[l]
        z_big, z_small = _norm_matmul(xf, mod, norm1_g[l].reshape(1, d), w_in_all, l, seq, min(2048, seq), INPROJ_TN)
        y_gla = _gla(z_big, z_small, w_gate_up[l], b_gate[l], gla_norm_g[l], batch, seq)
        y_dsa, w1_bf, w2_bf, wo_bf = _dsa(z_big, z_small, kv_norm_g[l], w_uv[l].astype(BF16), bias_tiles,
                                          w_ff1, w_ff2, w_out, l, batch, seq)
        xf, h2 = _merge_outproj(z_big, y_gla, y_dsa, wo_bf, xf, mod, norm2_g, l, seq, min(512, seq))
        xf = _ffn(xf, h2, mod, w1_bf, w2_bf, fg, seq, tm, 1024, final=(l == depth - 1))
    return xf.reshape(batch, seq, d)
```
